```python
import jax, jax.numpy as jnp
from jax import lax
import numpy as np

D_MODEL = 1024
BATCH = 32
SEQ = 2048
DEPTH = 2
DEC_BATCH = 128
DEC_SEQ = 4
PAST_LEN = 16384
PAGE_SIZE = 128

MIX_WIDTH = D_MODEL
MLA_DIM = MIX_WIDTH // 2
RWKV_DIM = MIX_WIDTH // 4
CONV_DIM = MIX_WIDTH - MLA_DIM - RWKV_DIM
MLA_HEADS = 8
MLA_V_DIM = MLA_DIM // MLA_HEADS
MLA_NOPE_DIM = 64
MLA_ROPE_DIM = 32
MLA_Q_RANK = 192
MLA_KV_RANK = 128
ROPE_THETA = 10000.0
ATTN_SCALE = (MLA_NOPE_DIM + MLA_ROPE_DIM) ** -0.5
Q_BLOCK = 128
RWKV_HEAD_DIM = 64
RWKV_HEADS = RWKV_DIM // RWKV_HEAD_DIM
DECAY_LORA = 64
ICLR_LORA = 64
GATE_LORA = 128
GN_EPS = 64e-5
CONV_WIDTH = 3
MLA_PROJ = MLA_Q_RANK + MLA_KV_RANK + MLA_ROPE_DIM
RWKV_PROJ = 3 * RWKV_DIM + DECAY_LORA + ICLR_LORA + GATE_LORA
CONV_PROJ = 3 * CONV_DIM
IN_COLS = MLA_PROJ + RWKV_PROJ + CONV_PROJ
N_GROUPS = 4
EXPERTS_PER_GROUP = 8
N_EXPERTS = N_GROUPS * EXPERTS_PER_GROUP
TOP_K = 2
EXPERT_FF = 256
MOE_BLOCK = 128
ALPHA = (2 * DEPTH) ** 0.25
BETA = (8 * DEPTH) ** -0.25
LN_EPS = 1e-5
RMS_EPS = 1e-6

kernel_name = 'hybrid_mla_rwkv7_shortconv_hiermoe_step'


def _split(x, sizes):
    return jnp.split(x, np.cumsum(sizes)[:-1].tolist(), axis=-1)


def _layer_norm(x, g, b):
    xf = x.astype(jnp.float32)
    mu = jnp.mean(xf, -1, keepdims=True)
    var = jnp.mean(jnp.square(xf - mu), -1, keepdims=True)
    return ((xf - mu) * lax.rsqrt(var + LN_EPS) * g + b).astype(x.dtype)


def _rms_norm(x, g):
    xf = x.astype(jnp.float32)
    return (xf * lax.rsqrt(jnp.mean(xf * xf, -1, keepdims=True) + RMS_EPS) * g).astype(x.dtype)


def _rope(x, pos):
    half = x.shape[-1] // 2
    freqs = ROPE_THETA ** (-jnp.arange(half, dtype=jnp.float32) / half)
    ang = pos.astype(jnp.float32)[:, None] * freqs
    ang = ang.reshape((1, ang.shape[0]) + (1,) * (x.ndim - 3) + (half,))
    cos, sin = jnp.cos(ang), jnp.sin(ang)
    xf = x.astype(jnp.float32)
    x1, x2 = xf[..., :half], xf[..., half:]
    return jnp.concatenate([x1 * cos - x2 * sin, x1 * sin + x2 * cos], -1).astype(x.dtype)


def _mla_prompt(q_lat, q_rope, ckv, kr):
    b_, s_ = q_lat.shape[:2]
    nb = s_ // Q_BLOCK
    blocks = lambda t: jnp.moveaxis(t.reshape((b_, nb, Q_BLOCK) + t.shape[2:]), 1, 0)
    kpos = jnp.arange(s_)

    def one_block(args):
        i, ql, qr = args
        s = (jnp.einsum('bqhc,bkc->bhqk', ql, ckv)
             + jnp.einsum('bqhr,bkr->bhqk', qr, kr)).astype(jnp.float32) * ATTN_SCALE
        qpos = i * Q_BLOCK + jnp.arange(Q_BLOCK)
        s = jnp.where(kpos[None, :] <= qpos[:, None], s, -jnp.inf)
        pr = jax.nn.softmax(s, -1).astype(ckv.dtype)
        return jnp.einsum('bhqk,bkc->bqhc', pr, ckv)

    o = lax.map(one_block, (jnp.arange(nb), blocks(q_lat), blocks(q_rope)))
    return jnp.moveaxis(o, 0, 1).reshape(q_lat.shape)


def _mla_sample(q_lat, q_rope, ckv, kr, ckv_pages, kr_pages, page_table):
    bd, t_ = q_lat.shape[:2]
    ckv_past = ckv_pages[page_table].reshape(bd, -1, MLA_KV_RANK)
    kr_past = kr_pages[page_table].reshape(bd, -1, MLA_ROPE_DIM)
    n_past = ckv_past.shape[1]
    s_past = (jnp.einsum('bqhc,bkc->bhqk', q_lat, ckv_past)
              + jnp.einsum('bqhr,bkr->bhqk', q_rope, kr_past)).astype(jnp.float32) * ATTN_SCALE
    s_new = (jnp.einsum('bqhc,bkc->bhqk', q_lat, ckv)
             + jnp.einsum('bqhr,bkr->bhqk', q_rope, kr)).astype(jnp.float32) * ATTN_SCALE
    causal = jnp.arange(t_)[None, :] <= jnp.arange(t_)[:, None]
    s_new = jnp.where(causal, s_new, -jnp.inf)
    pr = jax.nn.softmax(jnp.concatenate([s_past, s_new], -1), -1).astype(ckv.dtype)
    return (jnp.einsum('bhqk,bkc->bqhc', pr[..., :n_past], ckv_past)
            + jnp.einsum('bhqk,bkc->bqhc', pr[..., n_past:], ckv))


def _rwkv7(p, shift0, wkv0, mu_shift, w0, w_decay, a0, w_iclr, w_gate_out, k_k, k_a, r_k, lnx_g, lnx_b):
    b_, t_, _ = p.shape
    pf = p.astype(jnp.float32)
    prev = jnp.concatenate([shift0.astype(jnp.float32)[:, None], pf[:, :-1]], 1)
    xs = pf + (prev - pf) * mu_shift
    r, k, v, xw, xa, xg = _split(xs, [RWKV_DIM] * 3 + [DECAY_LORA, ICLR_LORA, GATE_LORA])
    w_log = -jax.nn.softplus(-(w0 + jnp.tanh(xw) @ w_decay)) - 0.5
    decay = jnp.exp(-jnp.exp(w_log))
    a = jax.nn.sigmoid(a0 + xa @ w_iclr)
    g = jax.nn.sigmoid(xg) @ w_gate_out
    heads = lambda t: t.reshape(b_, t_, RWKV_HEADS, RWKV_HEAD_DIM)
    kk = heads(k * k_k)
    kk = kk / jnp.maximum(jnp.sqrt(jnp.sum(kk * kk, -1, keepdims=True)), 1e-12)
    k = heads(k * (1.0 + (a - 1.0) * k_a))
    r, v, decay, a = heads(r), heads(v), heads(decay), heads(a)

    def step(s, inp):
        r_t, w_t, k_t, v_t, kk_t, a_t = inp
        sa = jnp.einsum('bhvk,bhk->bhv', s, -kk_t)
        s = (s * w_t[:, :, None, :] + sa[..., None] * (kk_t * a_t)[:, :, None, :]
             + v_t[..., None] * k_t[:, :, None, :])
        return s, jnp.einsum('bhvk,bhk->bhv', s, r_t)

    tm = lambda t: jnp.moveaxis(t, 1, 0)
    s_fin, y = lax.scan(step, wkv0.astype(jnp.float32),
                        (tm(r), tm(decay), tm(k), tm(v), tm(kk), tm(a)))
    y = jnp.moveaxis(y, 0, 1)
    m = jnp.mean(y, -1, keepdims=True)
    var = jnp.mean(jnp.square(y - m), -1, keepdims=True)
    yn = ((y - m) * lax.rsqrt(var + GN_EPS)).reshape(b_, t_, RWKV_DIM) * lnx_g + lnx_b
    bonus = (jnp.sum(r * k * r_k, -1, keepdims=True) * v).reshape(b_, t_, RWKV_DIM)
    out = (yn + bonus) * g
    return out.astype(p.dtype), s_fin.astype(wkv0.dtype), p[:, -1]


def _short_conv(p, buf0, conv_w):
    t_ = p.shape[1]
    gate_b, gate_c, u = _split(p, [CONV_DIM] * 3)
    padded = jnp.concatenate([buf0.astype(p.dtype), gate_c * u], 1)
    y = conv_w[0] * padded[:, 0:t_]
    for j in range(1, CONV_WIDTH):
        y = y + conv_w[j] * padded[:, j:j + t_]
    return gate_b * y, padded[:, -(CONV_WIDTH - 1):]


def _token_mix(h, pos, past, shift0, wkv0, conv0, w_in, g_qn, w_uq, g_kvn, w_uk, w_uv,
               mu_shift, w0, w_decay, a0, w_iclr, w_gate_out, k_k, k_a, r_k, lnx_g, lnx_b, conv_w):
    p_mla, p_rwkv, p_conv = _split(h @ w_in, [MLA_PROJ, RWKV_PROJ, CONV_PROJ])
    cq, c_kv, k_r = _split(p_mla, [MLA_Q_RANK, MLA_KV_RANK, MLA_ROPE_DIM])
    q = jnp.einsum('btr,rhd->bthd', _rms_norm(cq, g_qn), w_uq)
    q_nope = q[..., :MLA_NOPE_DIM]
    q_rope = _rope(q[..., MLA_NOPE_DIM:], pos)
    q_lat = jnp.einsum('bthn,chn->bthc', q_nope, w_uk)
    ckv = _rms_norm(c_kv, g_kvn)
    kr = _rope(k_r, pos)
    if past is None:
        o_lat = _mla_prompt(q_lat, q_rope, ckv, kr)
    else:
        o_lat = _mla_sample(q_lat, q_rope, ckv, kr, *past)
    a_out = jnp.einsum('bthc,chv->bthv', o_lat, w_uv).reshape(h.shape[0], h.shape[1], MLA_DIM)
    r_out, wkv_new, shift_new = _rwkv7(p_rwkv, shift0, wkv0, mu_shift, w0, w_decay, a0, w_iclr,
                                       w_gate_out, k_k, k_a, r_k, lnx_g, lnx_b)
    c_out, conv_new = _short_conv(p_conv, conv0, conv_w)
    return jnp.concatenate([a_out, r_out, c_out], -1), ckv, kr, wkv_new, shift_new, conv_new


def _hier_moe(x, w_gr, b_gr, w_er, b_er, w_g, w_u, w_d):
    shp = x.shape
    xf = x.reshape(-1, shp[-1])
    n_tok = xf.shape[0]
    tok = jnp.arange(n_tok)
    pg = jax.nn.softmax((xf @ w_gr + b_gr).astype(jnp.float32), -1)
    g_idx = jnp.argmax(pg, -1).astype(jnp.int32)
    g_p = jnp.max(pg, -1, keepdims=True)
    le = (xf @ w_er + b_er).astype(jnp.float32).reshape(n_tok, N_GROUPS, EXPERTS_PER_GROUP)
    pe = jax.nn.softmax(le[tok, g_idx], -1)
    top_p, top_i = lax.top_k(pe, TOP_K)
    gate = g_p * top_p / jnp.sum(top_p, -1, keepdims=True)
    eid = g_idx[:, None] * EXPERTS_PER_GROUP + top_i.astype(jnp.int32)
    n_asg = n_tok * TOP_K
    flat_e = eid.reshape(n_asg)
    flat_t = jnp.repeat(tok, TOP_K).astype(jnp.int32)
    flat_w = gate.reshape(n_asg)
    order = jnp.argsort(flat_e)
    se = flat_e[order]
    counts = jnp.zeros((N_EXPERTS,), jnp.int32).at[flat_e].add(1)
    starts = jnp.cumsum(counts) - counts
    pcounts = (counts + MOE_BLOCK - 1) // MOE_BLOCK * MOE_BLOCK
    pends = jnp.cumsum(pcounts)
    pstarts = pends - pcounts
    dest = pstarts[se] + jnp.arange(n_asg) - starts[se]
    n_blocks = -(-n_asg // MOE_BLOCK) + N_EXPERTS
    cap = n_blocks * MOE_BLOCK
    buf_t = jnp.full((cap,), n_tok, jnp.int32).at[dest].set(flat_t[order])
    buf_w = jnp.zeros((cap,), x.dtype).at[dest].set(flat_w[order].astype(x.dtype))
    blk_e = jnp.minimum(jnp.searchsorted(pends, jnp.arange(n_blocks) * MOE_BLOCK, side='right'),
                        N_EXPERTS - 1)
    xpad = jnp.concatenate([xf, jnp.zeros((1, shp[-1]), x.dtype)], 0)
    xb = xpad[buf_t].reshape(n_blocks, MOE_BLOCK, shp[-1])

    def expert_block(args):
        xe, e = args
        return (jax.nn.silu(xe @ w_g[e]) * (xe @ w_u[e])) @ w_d[e]

    yb = lax.map(expert_block, (xb, blk_e)).reshape(cap, shp[-1])
    y = jax.ops.segment_sum(yb * buf_w[:, None], buf_t, num_segments=n_tok + 1)[:n_tok]
    return y.reshape(shp)


def _layer_out(h, mix, w_out, ln1_g, ln1_b, w_gr, b_gr, w_er, b_er, w_g, w_u, w_d, ln2_g, ln2_b):
    x = _layer_norm(ALPHA * h + mix @ w_out, ln1_g, ln1_b)
    return _layer_norm(ALPHA * x + _hier_moe(x, w_gr, b_gr, w_er, b_er, w_g, w_u, w_d), ln2_g, ln2_b)


def setup_inputs(seed: int = 0) -> dict:
    key = jax.random.key(seed)
    ks = iter(jax.random.split(key, 64))
    nrm = lambda shape, scale: scale * jax.random.normal(next(ks), shape, jnp.float32)
    gain = lambda shape: 1.0 + nrm(shape, 0.02)
    n_pages = PAST_LEN // PAGE_SIZE
    n_pool = (DEC_BATCH * n_pages * 5) // 4
    L = DEPTH
    x_prompt = nrm((BATCH, SEQ, D_MODEL), 1.0)
    x_sample = nrm((DEC_BATCH, DEC_SEQ, D_MODEL), 1.0)
    cache_ckv = nrm((L, n_pool, PAGE_SIZE, MLA_KV_RANK), 1.0)
    cache_krope = nrm((L, n_pool, PAGE_SIZE, MLA_ROPE_DIM), 1.0)
    state_wkv = nrm((L, DEC_BATCH, RWKV_HEADS, RWKV_HEAD_DIM, RWKV_HEAD_DIM), 0.3)
    state_shift = nrm((L, DEC_BATCH, RWKV_PROJ), 1.0)
    state_conv = nrm((L, DEC_BATCH, CONV_WIDTH - 1, CONV_DIM), 0.5)
    page_table = jax.random.permutation(next(ks), n_pool)[:DEC_BATCH * n_pages]
    page_table = page_table.reshape(DEC_BATCH, n_pages).astype(jnp.int32)
    return {
        'x_prompt': x_prompt, 'x_sample': x_sample,
        'cache_ckv': cache_ckv, 'cache_krope': cache_krope,
        'state_wkv': state_wkv, 'state_shift': state_shift, 'state_conv': state_conv,
        'page_table': page_table,
        'ln_in_g': gain((D_MODEL,)), 'ln_in_b': nrm((D_MODEL,), 0.02),
        'w_in': nrm((L, D_MODEL, IN_COLS), D_MODEL ** -0.5),
        'g_qn': gain((L, MLA_Q_RANK)),
        'w_uq': nrm((L, MLA_Q_RANK, MLA_HEADS, MLA_NOPE_DIM + MLA_ROPE_DIM), MLA_Q_RANK ** -0.5),
        'g_kvn': gain((L, MLA_KV_RANK)),
        'w_uk': nrm((L, MLA_KV_RANK, MLA_HEADS, MLA_NOPE_DIM), MLA_KV_RANK ** -0.5),
        'w_uv': nrm((L, MLA_KV_RANK, MLA_HEADS, MLA_V_DIM), MLA_KV_RANK ** -0.5),
        'mu_shift': jax.random.uniform(next(ks), (L, RWKV_PROJ), jnp.float32),
        'w0': jax.random.uniform(next(ks), (L, RWKV_DIM), jnp.float32, -6.5, -1.0),
        'w_decay': nrm((L, DECAY_LORA, RWKV_DIM), 0.1 * DECAY_LORA ** -0.5),
        'a0': nrm((L, RWKV_DIM), 0.1),
        'w_iclr': nrm((L, ICLR_LORA, RWKV_DIM), ICLR_LORA ** -0.5),
        'w_gate_out': nrm((L, GATE_LORA, RWKV_DIM), GATE_LORA ** -0.5),
        'k_k': 0.85 + nrm((L, RWKV_DIM), 0.02),
        'k_a': gain((L, RWKV_DIM)),
        'r_k': nrm((L, RWKV_HEADS, RWKV_HEAD_DIM), 0.1),
        'lnx_g': gain((L, RWKV_DIM)), 'lnx_b': nrm((L, RWKV_DIM), 0.02),
        'conv_w': nrm((L, CONV_WIDTH, CONV_DIM), CONV_WIDTH ** -0.5),
        'w_out': nrm((L, MIX_WIDTH, D_MODEL), BETA * MIX_WIDTH ** -0.5),
        'ln1_g': gain((L, D_MODEL)), 'ln1_b': nrm((L, D_MODEL), 0.02),
        'w_group_router': nrm((L, D_MODEL, N_GROUPS), D_MODEL ** -0.5),
        'b_group_router': nrm((L, N_GROUPS), 0.01),
        'w_expert_router': nrm((L, D_MODEL, N_EXPERTS), D_MODEL ** -0.5),
        'b_expert_router': nrm((L, N_EXPERTS), 0.01),
        'w_exp_gate': nrm((L, N_EXPERTS, D_MODEL, EXPERT_FF), D_MODEL ** -0.5),
        'w_exp_up': nrm((L, N_EXPERTS, D_MODEL, EXPERT_FF), D_MODEL ** -0.5),
        'w_exp_down': nrm((L, N_EXPERTS, EXPERT_FF, D_MODEL), BETA * EXPERT_FF ** -0.5),
        'ln2_g': gain((L, D_MODEL)), 'ln2_b': nrm((L, D_MODEL), 0.02),
    }


def reference(x_prompt, x_sample, cache_ckv, cache_krope, state_wkv, state_shift, state_conv,
              page_table, ln_in_g, ln_in_b, w_in, g_qn, w_uq, g_kvn, w_uk, w_uv, mu_shift, w0,
              w_decay, a0, w_iclr, w_gate_out, k_k, k_a, r_k, lnx_g, lnx_b, conv_w, w_out,
              ln1_g, ln1_b, w_group_router, b_group_router, w_expert_router, b_expert_router,
              w_exp_gate, w_exp_up, w_exp_down, ln2_g, ln2_b):
    bp = x_prompt.shape[0]
    hp = _layer_norm(x_prompt, ln_in_g, ln_in_b)
    hs = _layer_norm(x_sample, ln_in_g, ln_in_b)
    pos_p = jnp.arange(x_prompt.shape[1])
    pos_s = PAST_LEN + jnp.arange(x_sample.shape[1])
    zero_shift = jnp.zeros((bp, RWKV_PROJ), hp.dtype)
    zero_wkv = jnp.zeros((bp, RWKV_HEADS, RWKV_HEAD_DIM, RWKV_HEAD_DIM), hp.dtype)
    zero_conv = jnp.zeros((bp, CONV_WIDTH - 1, CONV_DIM), hp.dtype)
    ckv_p, kr_p, wkv_p, sh_p, cv_p = [], [], [], [], []
    ckv_s, kr_s, wkv_s, sh_s, cv_s = [], [], [], [], []
    for l in range(DEPTH):
        mix_w = (w_in[l], g_qn[l], w_uq[l], g_kvn[l], w_uk[l], w_uv[l], mu_shift[l], w0[l],
                 w_decay[l], a0[l], w_iclr[l], w_gate_out[l], k_k[l], k_a[l], r_k[l],
                 lnx_g[l], lnx_b[l], conv_w[l])
        out_w = (w_out[l], ln1_g[l], ln1_b[l], w_group_router[l], b_group_router[l],
                 w_expert_router[l], b_expert_router[l], w_exp_gate[l], w_exp_up[l],
                 w_exp_down[l], ln2_g[l], ln2_b[l])
        mix, ckv, kr, wkv, sh, cv = _token_mix(hp, pos_p, None, zero_shift, zero_wkv, zero_conv, *mix_w)
        hp = _layer_out(hp, mix, *out_w)
        ckv_p.append(ckv.reshape(-1, PAGE_SIZE, MLA_KV_RANK))
        kr_p.append(kr.reshape(-1, PAGE_SIZE, MLA_ROPE_DIM))
        wkv_p.append(wkv)
        sh_p.append(sh)
        cv_p.append(cv)
        past = (cache_ckv[l], cache_krope[l], page_table)
        mix, ckv, kr, wkv, sh, cv = _token_mix(hs, pos_s, past, state_shift[l], state_wkv[l],
                                               state_conv[l], *mix_w)
        hs = _layer_out(hs, mix, *out_w)
        ckv_s.append(ckv)
        kr_s.append(kr)
        wkv_s.append(wkv)
        sh_s.append(sh)
        cv_s.append(cv)
    return (hp, hs, jnp.stack(ckv_p), jnp.stack(kr_p), jnp.stack(wkv_p), jnp.stack(sh_p),
            jnp.stack(cv_p), jnp.stack(ckv_s), jnp.stack(kr_s), jnp.stack(wkv_s),
            jnp.stack(sh_s), jnp.stack(cv_s))
```

```python
import functools

import numpy as np
import jax
import jax.numpy as jnp
from jax import lax
from jax.experimental import pallas as pl
from jax.experimental.pallas import tpu as pltpu

F32 = jnp.float32
BF16 = jnp.bfloat16

MLA_HEADS = 8
MLA_NOPE_DIM = 64
MLA_ROPE_DIM = 32
MLA_Q_RANK = 192
MLA_KV_RANK = 128
MLA_V_DIM = 64
ROPE_THETA = 10000.0
ATTN_SCALE = (MLA_NOPE_DIM + MLA_ROPE_DIM) ** -0.5
RWKV_DIM = 256
RWKV_HEADS = 4
RWKV_HEAD_DIM = 64
RWKV_PROJ = 3 * RWKV_DIM + 64 + 64 + 128
CONV_DIM = 256
GN_EPS = 64e-5
N_GROUPS = 4
EXPERTS_PER_GROUP = 8
N_EXPERTS = N_GROUPS * EXPERTS_PER_GROUP
EXPERT_FF = 256
DEPTH = 2
ALPHA = (2 * DEPTH) ** 0.25
LN_EPS = 1e-5
RMS_EPS = 1e-6
PAGE_SIZE = 128

QPAD = 256
QHEAD = 256
KCAT = 256
COL_CKV = QPAD
COL_KR = COL_CKV + MLA_KV_RANK
COL_RWKV = COL_KR + 128
COL_CONV = COL_RWKV + RWKV_PROJ
IN_PAD = COL_CONV + 3 * CONV_DIM

LANES = 128
ATT_TQ = 128
ATT_TK = 512
PAGES_PER_STEP = 16
NEW_PAD = 16
MOE_ROWS = 256
VMEM_LIMIT = 56 * 1024 * 1024


def _cparams(sem):
    return pltpu.CompilerParams(dimension_semantics=sem, vmem_limit_bytes=VMEM_LIMIT)


def _ln_rows(x, g, b):
    mu = jnp.mean(x, -1, keepdims=True)
    xc = x - mu
    var = jnp.mean(xc * xc, -1, keepdims=True)
    return xc * lax.rsqrt(var + LN_EPS) * g + b


def _sigmoid(x):
    return 1.0 / (1.0 + jnp.exp(-x))


def _ln_in_kernel(x_ref, g_ref, b_ref, o_ref):
    o_ref[...] = _ln_rows(x_ref[...], g_ref[...], b_ref[...])


def _ln_in(x, g, b, tm):
    n, d = x.shape
    row = pl.BlockSpec((tm, d), lambda i: (i, 0))
    vec = pl.BlockSpec((1, d), lambda i: (0, 0))
    return pl.pallas_call(
        _ln_in_kernel, grid=(n // tm,), in_specs=[row, vec, vec], out_specs=row,
        out_shape=jax.ShapeDtypeStruct((n, d), F32), compiler_params=_cparams(("parallel",)),
    )(x, g.reshape(1, d), b.reshape(1, d))


def _ln2_kernel(x_ref, y_ref, g_ref, b_ref, o_ref):
    d = x_ref.shape[1]
    y = y_ref[:, :d].astype(F32) + y_ref[:, d:].astype(F32)
    o_ref[...] = _ln_rows(ALPHA * x_ref[...] + y, g_ref[...], b_ref[...])


def _ln2(x, y2, g, b, tm):
    n, d = x.shape
    row = pl.BlockSpec((tm, d), lambda i: (i, 0))
    vec = pl.BlockSpec((1, d), lambda i: (0, 0))
    return pl.pallas_call(
        _ln2_kernel, grid=(n // tm,),
        in_specs=[row, pl.BlockSpec((tm, 2 * d), lambda i: (i, 0)), vec, vec], out_specs=row,
        out_shape=jax.ShapeDtypeStruct((n, d), F32), compiler_params=_cparams(("parallel",)),
    )(x, y2, g.reshape(1, d), b.reshape(1, d))


def _in_proj_kernel(h_ref, w_ref, gq_ref, wq_ref, gkv_ref, ct_ref, st_ref,
                    q_ref, kcat_ref, ckv_ref, kr_ref, pr_ref, pc_ref):
    p = jnp.dot(h_ref[...].astype(BF16), w_ref[...], preferred_element_type=F32)
    ct = ct_ref[...]
    st = st_ref[...]

    def rope(tile):
        return tile * ct + pltpu.roll(tile, LANES - MLA_ROPE_DIM, 1) * st

    cq = p[:, :QPAD]
    ms = jnp.sum(cq * cq, -1, keepdims=True) * (1.0 / MLA_Q_RANK)
    cqn = (cq * lax.rsqrt(ms + RMS_EPS) * gq_ref[...]).astype(BF16)
    q = jnp.dot(cqn, wq_ref[...], preferred_element_type=F32)
    for h in range(MLA_HEADS):
        base = h * QHEAD
        q_ref[h, :, :LANES] = q[:, base:base + LANES].astype(BF16)
        q_ref[h, :, LANES:] = rope(q[:, base + LANES:base + QHEAD]).astype(BF16)

    c = p[:, COL_CKV:COL_CKV + MLA_KV_RANK]
    ckv = c * lax.rsqrt(jnp.mean(c * c, -1, keepdims=True) + RMS_EPS) * gkv_ref[...]
    ckv_ref[...] = ckv
    kr = rope(p[:, COL_KR:COL_KR + LANES])
    kr_ref[...] = kr[:, :MLA_ROPE_DIM]
    kcat_ref[:, :LANES] = ckv.astype(BF16)
    kcat_ref[:, LANES:] = kr.astype(BF16)
    pr_ref[...] = p[:, COL_RWKV:COL_RWKV + RWKV_PROJ]
    pc_ref[...] = p[:, COL_CONV:COL_CONV + 3 * CONV_DIM]


def _in_proj(h, w_in_p, gq_p, w_q, g_kvn, ctab, stab, tm):
    n, d = h.shape
    tab_blocks = ctab.shape[0] // tm
    row = lambda c: pl.BlockSpec((tm, c), lambda i: (i, 0))
    full = lambda a: pl.BlockSpec(a.shape, lambda i: (0,) * a.ndim)
    tab = pl.BlockSpec((tm, LANES), lambda i: (i % tab_blocks, 0))
    gkv = g_kvn.reshape(1, MLA_KV_RANK)
    out_shape = (
        jax.ShapeDtypeStruct((MLA_HEADS, n, QHEAD), BF16),
        jax.ShapeDtypeStruct((n, KCAT), BF16),
        jax.ShapeDtypeStruct((n, MLA_KV_RANK), F32),
        jax.ShapeDtypeStruct((n, MLA_ROPE_DIM), F32),
        jax.ShapeDtypeStruct((n, RWKV_PROJ), F32),
        jax.ShapeDtypeStruct((n, 3 * CONV_DIM), F32),
    )
    out_specs = (
        pl.BlockSpec((MLA_HEADS, tm, QHEAD), lambda i: (0, i, 0)),
        row(KCAT), row(MLA_KV_RANK), row(MLA_ROPE_DIM), row(RWKV_PROJ), row(3 * CONV_DIM),
    )
    return pl.pallas_call(
        _in_proj_kernel, grid=(n // tm,),
        in_specs=[row(d), full(w_in_p), full(gq_p), full(w_q), full(gkv), tab, tab],
        out_specs=out_specs, out_shape=out_shape, compiler_params=_cparams(("parallel",)),
    )(h, w_in_p, gq_p, w_q, gkv, ctab, stab)


def _mla_prompt_kernel(q_ref, k_ref, o_ref):
    i = pl.program_id(1)
    rows = MLA_HEADS * ATT_TQ
    q = q_ref[...].reshape(rows, QHEAD)

    def chunk(j, carry, masked):
        m, l, acc = carry
        k = k_ref[pl.ds(pl.multiple_of(j * ATT_TK, ATT_TK), ATT_TK), :]
        s = lax.dot_general(q, k, (((1,), (1,)), ((), ())), preferred_element_type=F32)
        if masked:
            qpos = i * ATT_TQ + lax.broadcasted_iota(
                jnp.int32, (MLA_HEADS, ATT_TQ, ATT_TK), 1).reshape(rows, ATT_TK)
            kpos = j * ATT_TK + lax.broadcasted_iota(jnp.int32, (rows, ATT_TK), 1)
            s = jnp.where(kpos <= qpos, s, -jnp.inf)
        m_new = jnp.maximum(m, jnp.max(s, -1, keepdims=True))
        a = jnp.exp(m - m_new)
        pr = jnp.exp(s - m_new)
        l = a * l + jnp.sum(pr, -1, keepdims=True)
        acc = a * acc + jnp.dot(pr.astype(BF16), k[:, :MLA_KV_RANK], preferred_element_type=F32)
        return m_new, l, acc

    init = (jnp.full((rows, 1), -jnp.inf, F32), jnp.zeros((rows, 1), F32),
            jnp.zeros((rows, MLA_KV_RANK), F32))
    n_full = (i * ATT_TQ) // ATT_TK
    carry = lax.fori_loop(0, n_full, lambda j, c: chunk(j, c, False), init)
    m, l, acc = chunk(n_full, carry, True)
    o = acc / l
    for h in range(MLA_HEADS):
        o_ref[:, h * MLA_KV_RANK:(h + 1) * MLA_KV_RANK] = o[h * ATT_TQ:(h + 1) * ATT_TQ].astype(BF16)


def _mla_prompt(q, kcat, batch, seq):
    n = batch * seq
    nq = seq // ATT_TQ
    return pl.pallas_call(
        _mla_prompt_kernel, grid=(batch, nq),
        in_specs=[pl.BlockSpec((MLA_HEADS, ATT_TQ, QHEAD), lambda b, i: (0, b * nq + i, 0)),
                  pl.BlockSpec((seq, KCAT), lambda b, i: (b, 0))],
        out_specs=pl.BlockSpec((ATT_TQ, MLA_HEADS * MLA_KV_RANK), lambda b, i: (b * nq + i, 0)),
        out_shape=jax.ShapeDtypeStruct((n, MLA_HEADS * MLA_KV_RANK), BF16),
        compiler_params=_cparams(("parallel", "arbitrary")),
    )(q, kcat)


def _mla_sample_kernel(pt_ref, q_ref, kn_ref, *rest, n_new):
    npg = PAGES_PER_STEP
    ckv_refs = rest[:npg]
    kr_refs = rest[npg:2 * npg]
    o_ref, m_ref, l_ref, acc_ref = rest[2 * npg:]
    c = pl.program_id(1)

    @pl.when(c == 0)
    def _():
        m_ref[...] = jnp.full_like(m_ref, -jnp.inf)
        l_ref[...] = jnp.zeros_like(l_ref)
        acc_ref[...] = jnp.zeros_like(acc_ref)

    q = q_ref[...]
    q_lat = q[:, :MLA_KV_RANK]
    q_rope = q[:, MLA_KV_RANK:MLA_KV_RANK + MLA_ROPE_DIM]
    nt = (((1,), (1,)), ((), ()))

    def update(s, v):
        m = m_ref[...]
        m_new = jnp.maximum(m, jnp.max(s, -1, keepdims=True))
        a = jnp.exp(m - m_new)
        pr = jnp.exp(s - m_new)
        l_ref[...] = a * l_ref[...] + jnp.sum(pr, -1, keepdims=True)
        acc_ref[...] = a * acc_ref[...] + jnp.dot(pr.astype(BF16), v, preferred_element_type=F32)
        m_ref[...] = m_new

    for j in range(npg):
        ckv = ckv_refs[j][...].astype(BF16)
        kr = kr_refs[j][...].astype(BF16)
        s = (lax.dot_general(q_lat, ckv, nt, preferred_element_type=F32)
             + lax.dot_general(q_rope, kr, nt, preferred_element_type=F32))
        update(s, ckv)

    @pl.when(c == pl.num_programs(1) - 1)
    def _():
        kn = kn_ref[...]
        s = lax.dot_general(q, kn, nt, preferred_element_type=F32)
        t_row = lax.broadcasted_iota(jnp.int32, s.shape, 0) % n_new
        s = jnp.where(lax.broadcasted_iota(jnp.int32, s.shape, 1) <= t_row, s, -jnp.inf)
        update(s, kn[:, :MLA_KV_RANK])
        o_ref[...] = (acc_ref[...] / l_ref[...]).astype(BF16)


def _mla_sample(q, k_new, cache_ckv, cache_krope, page_table, layer):
    bd, rows, _ = q.shape
    n_new = k_new.shape[1]
    k_new = jnp.pad(k_new, ((0, 0), (0, NEW_PAD - n_new), (0, 0)))
    n_pages = page_table.shape[1]
    npg = PAGES_PER_STEP
    assert n_pages % npg == 0
    page = lambda width, j: pl.BlockSpec(
        (None, None, PAGE_SIZE, width), lambda b, c, pt: (layer, pt[b, c * npg + j], 0, 0))
    in_specs = ([pl.BlockSpec((None, rows, QHEAD), lambda b, c, pt: (b, 0, 0)),
                 pl.BlockSpec((None, NEW_PAD, KCAT), lambda b, c, pt: (b, 0, 0))]
                + [page(MLA_KV_RANK, j) for j in range(npg)]
                + [page(MLA_ROPE_DIM, j) for j in range(npg)])
    grid_spec = pltpu.PrefetchScalarGridSpec(
        num_scalar_prefetch=1, grid=(bd, n_pages // npg), in_specs=in_specs,
        out_specs=pl.BlockSpec((None, rows, MLA_KV_RANK), lambda b, c, pt: (b, 0, 0)),
        scratch_shapes=[pltpu.VMEM((rows, 1), F32), pltpu.VMEM((rows, 1), F32),
                        pltpu.VMEM((rows, MLA_KV_RANK), F32)])
    return pl.pallas_call(
        functools.partial(_mla_sample_kernel, n_new=n_new), grid_spec=grid_spec,
        out_shape=jax.ShapeDtypeStruct((bd, rows, MLA_KV_RANK), BF16),
        compiler_params=_cparams(("parallel", "arbitrary")),
    )(page_table, q, k_new, *([cache_ckv] * npg), *([cache_krope] * npg))


def _prev_rows(x, before, shift):
    if shift == 1:
        row = lax.broadcasted_iota(jnp.int32, x.shape, 0)
        return jnp.where(row == 0, before, pltpu.roll(x, 1, 0))
    return jnp.concatenate([before, x[:x.shape[0] - shift]], axis=0)


def _mix_prep_kernel(pr_ref, pc_ref, sh0_ref, cv0_ref, mu_ref, w0_ref, a0_ref, wdec_ref, wiclr_ref,
                     wgate_ref, cw_ref,
                     r_ref, k_ref, v_ref, w_ref, a_ref, g_ref, co_ref, cu_ref,
                     sh_c, cu1_c, cu2_c, *, shift, carry):
    first = pl.program_id(1) == 0

    if carry:
        @pl.when(first)
        def _():
            sh_c[...] = sh0_ref[...]
            cu1_c[...] = cv0_ref[1]
            cu2_c[...] = cv0_ref[0]
        sh_before, cu1_before, cu2_before = sh_c[...], cu1_c[...], cu2_c[...]
    else:
        sh_before, cu1_before, cu2_before = sh0_ref[...], cv0_ref[1], cv0_ref[0]

    p = pr_ref[...]
    xs = p + (_prev_rows(p, sh_before, shift) - p) * mu_ref[...]
    r_ref[...] = xs[:, :RWKV_DIM]
    k_ref[...] = xs[:, RWKV_DIM:2 * RWKV_DIM]
    v_ref[...] = xs[:, 2 * RWKV_DIM:3 * RWKV_DIM]
    lora = xs[:, 3 * RWKV_DIM:3 * RWKV_DIM + LANES]
    z = w0_ref[...] + jnp.dot(jnp.tanh(lora).astype(BF16), wdec_ref[...], preferred_element_type=F32)
    nz = -z
    softplus = jnp.maximum(nz, 0.0) + jnp.log(1.0 + jnp.exp(-jnp.abs(nz)))
    w_ref[...] = jnp.exp(-jnp.exp(-softplus - 0.5))
    a_ref[...] = _sigmoid(a0_ref[...] + jnp.dot(lora.astype(BF16), wiclr_ref[...],
                                                preferred_element_type=F32))
    xg = xs[:, 3 * RWKV_DIM + LANES:]
    g_ref[...] = jnp.dot(_sigmoid(xg).astype(BF16), wgate_ref[...], preferred_element_type=F32)

    pc = pc_ref[...]
    cu = pc[:, CONV_DIM:2 * CONV_DIM] * pc[:, 2 * CONV_DIM:]
    cu1 = _prev_rows(cu, cu1_before, shift)
    cu2 = _prev_rows(cu1, cu2_before, shift)
    cw = cw_ref[...]
    co_ref[...] = (pc[:, :CONV_DIM] * (cw[0:1] * cu2 + cw[1:2] * cu1 + cw[2:3] * cu)).astype(BF16)
    cu_ref[...] = cu

    if carry:
        tm = p.shape[0]
        sh_c[...] = p[tm - 1:tm]
        cu1_c[...] = cu[tm - 1:tm]
        cu2_c[...] = cu1[tm - 1:tm]


def _mix_prep(p_r, p_c, shift0, conv0, mu, w0, a0, wdec_p, wiclr_p, wgate, conv_w, *, n_seq, tm, shift):
    n = p_r.shape[0]
    carry = shift == 1
    tps = n // n_seq // tm if carry else 1
    grid = (n_seq, tps) if carry else (1, 1)
    row = lambda c: pl.BlockSpec((tm, c), lambda b, t: (b * tps + t, 0))
    full = lambda a: pl.BlockSpec(a.shape, lambda b, t: (0,) * a.ndim)
    if carry:
        sh_spec = pl.BlockSpec((None, 1, RWKV_PROJ), lambda b, t: (b, 0, 0))
        cv_spec = pl.BlockSpec((None, 2, 1, CONV_DIM), lambda b, t: (b, 0, 0, 0))
        shift0 = shift0.reshape(n_seq, 1, RWKV_PROJ)
        conv0 = conv0.reshape(n_seq, 2, 1, CONV_DIM)
    else:
        sh_spec = full(shift0)
        conv0 = jnp.swapaxes(conv0, 0, 1)
        cv_spec = full(conv0)
    vec = lambda a: a.reshape(1, -1)
    args = (p_r, p_c, shift0, conv0, vec(mu), vec(w0), vec(a0), wdec_p, wiclr_p, wgate, conv_w)
    in_specs = [row(RWKV_PROJ), row(3 * CONV_DIM), sh_spec, cv_spec] + [full(a) for a in args[4:]]
    f32o = jax.ShapeDtypeStruct((n, RWKV_DIM), F32)
    out_shape = (f32o,) * 6 + (jax.ShapeDtypeStruct((n, CONV_DIM), BF16), f32o)
    rows_before = shift
    return pl.pallas_call(
        functools.partial(_mix_prep_kernel, shift=shift, carry=carry), grid=grid,
        in_specs=in_specs, out_specs=(row(RWKV_DIM),) * 8, out_shape=out_shape,
        scratch_shapes=[pltpu.VMEM((rows_before, RWKV_PROJ), F32),
                        pltpu.VMEM((rows_before, CONV_DIM), F32),
                        pltpu.VMEM((rows_before, CONV_DIM), F32)],
        compiler_params=_cparams(("parallel", "arbitrary")),
    )(*args)


def _wkv_kernel(r_ref, k_ref, v_ref, w_ref, a_ref, kk_ref, ka_ref, rk_ref, gn_ref, bn_ref, s0_ref,
                y_ref, sout_ref, s_ref, al_ref, wr_ref, be_ref, km_ref, wd_ref):
    tb = pl.program_id(1)
    hd = RWKV_HEAD_DIM

    @pl.when(tb == 0)
    def _():
        s_ref[...] = s0_ref[...]

    kk_t, ka_t, rk_t = kk_ref[...], ka_ref[...], rk_ref[...]
    gn_t, bn_t = gn_ref[...], bn_ref[...]

    def step(t, _):
        r, k, v, w, a = r_ref[t], k_ref[t], v_ref[t], w_ref[t], a_ref[t]
        kk = k * kk_t
        kk = kk / jnp.maximum(jnp.sqrt(jnp.sum(kk * kk, 0, keepdims=True)), 1e-12)
        km = k * (1.0 + (a - 1.0) * ka_t)
        be = kk * a
        al_ref[...] = -kk
        wr_ref[...] = w * r
        be_ref[...] = be
        km_ref[...] = km
        wd_ref[...] = w
        sa = jnp.zeros((hd, LANES), F32)
        y0 = jnp.zeros((hd, LANES), F32)
        for i in range(hd):
            s_i = s_ref[i]
            sa = sa + s_i * al_ref[i:i + 1, :]
            y0 = y0 + s_i * wr_ref[i:i + 1, :]
        br = jnp.sum(be * r, 0, keepdims=True)
        kr = jnp.sum(km * r, 0, keepdims=True)
        y = y0 + sa * br + v * kr
        for i in range(hd):
            s_ref[i] = s_ref[i] * wd_ref[i:i + 1, :] + sa * be_ref[i:i + 1, :] + v * km_ref[i:i + 1, :]
        mean = jnp.mean(y, 0, keepdims=True)
        yc = y - mean
        var = jnp.mean(yc * yc, 0, keepdims=True)
        bonus = jnp.sum(r * km * rk_t, 0, keepdims=True) * v
        y_ref[t] = yc * lax.rsqrt(var + GN_EPS) * gn_t + bn_t + bonus
        return 0

    lax.fori_loop(0, r_ref.shape[0], step, 0)

    @pl.when(tb == pl.num_programs(1) - 1)
    def _():
        sout_ref[...] = s_ref[...]


def _wkv(r, k, v, w, a, kk_t, ka_t, rk_t, gn_t, bn_t, s0, tt):
    t_len, hd, lanes = r.shape
    seq = pl.BlockSpec((tt, hd, LANES), lambda g, t: (t, 0, g))
    par = pl.BlockSpec((hd, LANES), lambda g, t: (0, g))
    st = pl.BlockSpec((hd, hd, LANES), lambda g, t: (0, 0, g))
    return pl.pallas_call(
        _wkv_kernel, grid=(lanes // LANES, t_len // tt),
        in_specs=[seq] * 5 + [par] * 5 + [st], out_specs=(seq, st),
        out_shape=(jax.ShapeDtypeStruct((t_len, hd, lanes), F32),
                   jax.ShapeDtypeStruct((hd, hd, lanes), F32)),
        scratch_shapes=[pltpu.VMEM((hd, hd, LANES), F32)] + [pltpu.VMEM((hd, LANES), F32)] * 5,
        compiler_params=_cparams(("parallel", "arbitrary")),
    )(r, k, v, w, a, kk_t, ka_t, rk_t, gn_t, bn_t, s0)


def _mix_out_kernel(h_ref, o_ref, y_ref, g_ref, co_ref, wo_ref, g1_ref, b1_ref, wr_ref, br_ref,
                    x_ref, xb_ref, eid_ref, gate_ref):
    mix = jnp.concatenate(
        [o_ref[...], (y_ref[...] * g_ref[...]).astype(BF16), co_ref[...]], axis=-1)
    pre = ALPHA * h_ref[...] + jnp.dot(mix, wo_ref[...], preferred_element_type=F32)
    x = _ln_rows(pre, g1_ref[...], b1_ref[...])
    x_ref[...] = x
    xb = x.astype(BF16)
    xb_ref[...] = xb

    logits = jnp.dot(xb, wr_ref[...], preferred_element_type=F32) + br_ref[...]
    lane = lax.broadcasted_iota(jnp.int32, logits.shape, 1)
    ninf = -jnp.inf
    lg = jnp.where(lane < N_GROUPS, logits, ninf)
    mg = jnp.max(lg, -1, keepdims=True)
    g_p = 1.0 / jnp.sum(jnp.exp(lg - mg), -1, keepdims=True)
    g_idx = jnp.min(jnp.where(lg == mg, lane, LANES), -1, keepdims=True)
    lo = N_GROUPS + EXPERTS_PER_GROUP * g_idx
    le = jnp.where((lane >= lo) & (lane < lo + EXPERTS_PER_GROUP), logits, ninf)
    m1 = jnp.max(le, -1, keepdims=True)
    i1 = jnp.min(jnp.where(le == m1, lane, LANES), -1, keepdims=True)
    le2 = jnp.where(lane == i1, ninf, le)
    m2 = jnp.max(le2, -1, keepdims=True)
    i2 = jnp.min(jnp.where(le2 == m2, lane, LANES), -1, keepdims=True)
    e2 = jnp.exp(m2 - m1)
    gate1 = g_p / (1.0 + e2)
    gate2 = g_p * e2 / (1.0 + e2)
    eid_ref[...] = jnp.where(lane == 0, i1 - N_GROUPS, jnp.where(lane == 1, i2 - N_GROUPS, 0))
    gate_ref[...] = jnp.where(lane == 0, gate1, jnp.where(lane == 1, gate2, 0.0))


def _mix_out(h, o_lat, y, g, c_out, w_o, ln_g, ln_b, w_router, b_router, tm):
    n, d = h.shape
    row = lambda c: pl.BlockSpec((tm, c), lambda i: (i, 0))
    full = lambda a: pl.BlockSpec(a.shape, lambda i: (0,) * a.ndim)
    vec = lambda a: a.reshape(1, -1)
    args = (h, o_lat, y, g, c_out, w_o, vec(ln_g), vec(ln_b), w_router, b_router)
    in_specs = [row(d), row(o_lat.shape[1]), row(RWKV_DIM), row(RWKV_DIM), row(CONV_DIM)] + [
        full(a) for a in args[5:]]
    out_shape = (jax.ShapeDtypeStruct((n, d), F32), jax.ShapeDtypeStruct((n, d), BF16),
                 jax.ShapeDtypeStruct((n, LANES), jnp.int32), jax.ShapeDtypeStruct((n, LANES), F32))
    return pl.pallas_call(
        _mix_out_kernel, grid=(n // tm,), in_specs=in_specs,
        out_specs=(row(d), row(d), row(LANES), row(LANES)), out_shape=out_shape,
        compiler_params=_cparams(("parallel",)),
    )(*args)


def _moe_kernel(be_ref, x_ref, gw_ref, wgu_ref, wd_ref, o_ref):
    gu = jnp.dot(x_ref[...], wgu_ref[...], preferred_element_type=F32)
    hg = gu[:, :EXPERT_FF]
    hid = hg * _sigmoid(hg) * gu[:, EXPERT_FF:]
    y = jnp.dot(hid.astype(BF16), wd_ref[...], preferred_element_type=F32)
    o_ref[...] = (y * gw_ref[...]).astype(BF16)


def _moe_experts(blk_e, xb, row_w, w_gu, w_d):
    cap, d = xb.shape
    grid_spec = pltpu.PrefetchScalarGridSpec(
        num_scalar_prefetch=1, grid=(cap // MOE_ROWS,),
        in_specs=[pl.BlockSpec((MOE_ROWS, d), lambda i, be: (i, 0)),
                  pl.BlockSpec((MOE_ROWS, 1), lambda i, be: (i, 0)),
                  pl.BlockSpec((None, d, 2 * EXPERT_FF), lambda i, be: (be[i], 0, 0)),
                  pl.BlockSpec((None, EXPERT_FF, d), lambda i, be: (be[i], 0, 0))],
        out_specs=pl.BlockSpec((MOE_ROWS, d), lambda i, be: (i, 0)))
    return pl.pallas_call(
        _moe_kernel, grid_spec=grid_spec, out_shape=jax.ShapeDtypeStruct((cap, d), BF16),
        compiler_params=_cparams(("arbitrary",)),
    )(blk_e, xb, row_w, w_gu, w_d)


def _moe_dispatch(eid, gate):
    n_tok = eid.shape[0]
    n_asg = 2 * n_tok
    e_flat = eid.reshape(n_asg)
    w_flat = gate.reshape(n_asg)
    onehot = (e_flat[:, None] == jnp.arange(N_EXPERTS, dtype=jnp.int32)[None, :]).astype(jnp.int32)
    csum = jnp.cumsum(onehot, axis=0)
    counts = csum[-1]
    rank = jnp.take_along_axis(csum, e_flat[:, None], axis=1)[:, 0] - 1
    starts = jnp.cumsum(counts) - counts
    pcounts = (counts + MOE_ROWS - 1) // MOE_ROWS * MOE_ROWS
    pends = jnp.cumsum(pcounts)
    pstarts = pends - pcounts
    pos = pstarts[e_flat] + rank
    n_blocks = n_asg // MOE_ROWS + N_EXPERTS
    cap = n_blocks * MOE_ROWS
    blk_e = jnp.minimum(jnp.searchsorted(pends, jnp.arange(n_blocks, dtype=jnp.int32) * MOE_ROWS,
                                         side='right'), N_EXPERTS - 1).astype(jnp.int32)
    order = jnp.argsort(e_flat, stable=True).astype(jnp.int32)
    slot = jnp.arange(cap, dtype=jnp.int32)
    e_slot = blk_e[slot // MOE_ROWS]
    off = slot - pstarts[e_slot]
    valid = off < counts[e_slot]
    asg = order[jnp.clip(starts[e_slot] + off, 0, n_asg - 1)]
    slot_tok = jnp.where(valid, asg // 2, 0)
    slot_w = jnp.where(valid, w_flat[asg], 0.0)
    return blk_e, slot_tok, slot_w, pos


def _rope_tables(pos):
    half = MLA_ROPE_DIM // 2
    freqs = ROPE_THETA ** (-jnp.arange(half, dtype=F32) / half)
    ang = pos.astype(F32)[:, None] * freqs
    cos, sin = jnp.cos(ang), jnp.sin(ang)
    pad = jnp.zeros((pos.shape[0], LANES - MLA_ROPE_DIM), F32)
    return (jnp.concatenate([cos, cos, pad], -1), jnp.concatenate([-sin, sin, pad], -1))


def _swap_halves(w):
    half = MLA_ROPE_DIM // 2
    return jnp.concatenate([w[..., half:], w[..., :half]], -1)


def _layer_params(l, w_in, g_qn, w_uq, g_kvn, w_uk, w_uv, mu_shift, w0, w_decay, a0, w_iclr,
                  w_gate_out, k_k, k_a, r_k, lnx_g, lnx_b, conv_w, w_out, ln1_g, ln1_b,
                  w_group_router, b_group_router, w_expert_router, b_expert_router,
                  w_exp_gate, w_exp_up, w_exp_down, ln2_g, ln2_b):
    d = w_in.shape[1]
    hi = lax.Precision.HIGHEST
    wi = w_in[l]
    zc = lambda c: jnp.zeros((d, c), F32)
    c0 = MLA_Q_RANK
    c1 = c0 + MLA_KV_RANK
    c2 = c1 + MLA_ROPE_DIM
    w_kr = wi[:, c1:c2]
    w_in_p = jnp.concatenate([
        wi[:, :c0], zc(QPAD - MLA_Q_RANK), wi[:, c0:c1],
        w_kr, _swap_halves(w_kr), zc(LANES - 2 * MLA_ROPE_DIM), wi[:, c2:]], -1).astype(BF16)
    gq_p = jnp.concatenate([g_qn[l], jnp.zeros((QPAD - MLA_Q_RANK,), F32)]).reshape(1, QPAD)
    uq = w_uq[l]
    q_lat = jnp.einsum('rhn,chn->rhc', uq[..., :MLA_NOPE_DIM], w_uk[l], precision=hi)
    q_rope = uq[..., MLA_NOPE_DIM:]
    wq = jnp.concatenate([q_lat, q_rope, _swap_halves(q_rope),
                          jnp.zeros((MLA_Q_RANK, MLA_HEADS, QHEAD - LANES - 2 * MLA_ROPE_DIM), F32)], -1)
    wq = (wq * ATTN_SCALE).reshape(MLA_Q_RANK, MLA_HEADS * QHEAD)
    wq = jnp.concatenate([wq, jnp.zeros((QPAD - MLA_Q_RANK, MLA_HEADS * QHEAD), F32)], 0).astype(BF16)
    wo = w_out[l]
    mla_dim = MLA_HEADS * MLA_V_DIM
    wo_att = jnp.einsum('chv,hvd->hcd', w_uv[l], wo[:mla_dim].reshape(MLA_HEADS, MLA_V_DIM, -1),
                        precision=hi).reshape(MLA_HEADS * MLA_KV_RANK, -1)
    w_o = jnp.concatenate([wo_att, wo[mla_dim:]], 0).astype(BF16)
    z64 = jnp.zeros((64, RWKV_DIM), F32)
    wdec_p = jnp.concatenate([w_decay[l], z64], 0).astype(BF16)
    wiclr_p = jnp.concatenate([z64, w_iclr[l]], 0).astype(BF16)
    w_router = jnp.concatenate(
        [w_group_router[l], w_expert_router[l],
         jnp.zeros((d, LANES - N_GROUPS - N_EXPERTS), F32)], -1).astype(BF16)
    b_router = jnp.concatenate(
        [b_group_router[l], b_expert_router[l],
         jnp.zeros((LANES - N_GROUPS - N_EXPERTS,), F32)]).reshape(1, LANES)
    w_gu = jnp.concatenate([w_exp_gate[l], w_exp_up[l]], -1).astype(BF16)
    return dict(
        w_in_p=w_in_p, gq_p=gq_p, wq=wq, g_kvn=g_kvn[l], mu=mu_shift[l], w0=w0[l], a0=a0[l],
        wdec_p=wdec_p, wiclr_p=wiclr_p, wgate=w_gate_out[l].astype(BF16), conv_w=conv_w[l],
        k_k=k_k[l], k_a=k_a[l], r_k=r_k[l].reshape(-1), lnx_g=lnx_g[l], lnx_b=lnx_b[l],
        w_o=w_o, ln1_g=ln1_g[l], ln1_b=ln1_b[l], w_router=w_router, b_router=b_router,
        w_gu=w_gu, w_d=w_exp_down[l].astype(BF16), ln2_g=ln2_g[l], ln2_b=ln2_b[l])


def _head_tile(p, n_seq):
    t = p.reshape(RWKV_HEADS, RWKV_HEAD_DIM).T
    return jnp.tile(t, (1, n_seq))


def _layer(h, lp, *, n_seq, t_len, time_major, ctab, stab, attend, shift0, wkv0, conv0):
    n, d = h.shape
    tm = min(512, n)
    q, kcat, ckv, kr, p_r, p_c = _in_proj(h, lp['w_in_p'], lp['gq_p'], lp['wq'], lp['g_kvn'],
                                          ctab, stab, tm)
    o_lat = attend(q, kcat)

    r, k, v, w, a, g, c_out, cu = _mix_prep(
        p_r, p_c, shift0, conv0, lp['mu'], lp['w0'], lp['a0'], lp['wdec_p'], lp['wiclr_p'],
        lp['wgate'], lp['conv_w'], n_seq=n_seq, tm=tm, shift=n_seq if time_major else 1)

    lanes = n_seq * RWKV_HEADS

    def to_scan(x):
        if time_major:
            x = x.reshape(t_len, n_seq, RWKV_HEADS, RWKV_HEAD_DIM).transpose(0, 3, 1, 2)
        else:
            x = x.reshape(n_seq, t_len, RWKV_HEADS, RWKV_HEAD_DIM).transpose(1, 3, 0, 2)
        return x.reshape(t_len, RWKV_HEAD_DIM, lanes)

    tiles = [_head_tile(lp[name], n_seq) for name in ('k_k', 'k_a', 'r_k', 'lnx_g', 'lnx_b')]
    s0 = wkv0.transpose(3, 2, 0, 1).reshape(RWKV_HEAD_DIM, RWKV_HEAD_DIM, lanes)
    y, s_fin = _wkv(*[to_scan(x) for x in (r, k, v, w, a)], *tiles, s0, min(16, t_len))
    y = y.reshape(t_len, RWKV_HEAD_DIM, n_seq, RWKV_HEADS)
    y = (y.transpose(0, 2, 3, 1) if time_major else y.transpose(2, 0, 3, 1)).reshape(n, RWKV_DIM)
    wkv_new = s_fin.reshape(RWKV_HEAD_DIM, RWKV_HEAD_DIM, n_seq, RWKV_HEADS).transpose(2, 3, 1, 0)

    x, xb, eid, gate = _mix_out(h, o_lat, y, g, c_out, lp['w_o'], lp['ln1_g'], lp['ln1_b'],
                                lp['w_router'], lp['b_router'], tm)
    blk_e, slot_tok, slot_w, pos = _moe_dispatch(eid[:, :2], gate[:, :2])
    xs = jnp.take(xb, slot_tok, axis=0)
    yb = _moe_experts(blk_e, xs, slot_w.reshape(-1, 1), lp['w_gu'], lp['w_d'])
    y2 = jnp.take(yb, pos, axis=0).reshape(n, 2 * d)
    h_next = _ln2(x, y2, lp['ln2_g'], lp['ln2_b'], tm)
    return h_next, ckv, kr, p_r, cu, wkv_new


def kernel(x_prompt, x_sample, cache_ckv, cache_krope, state_wkv, state_shift, state_conv, page_table, ln_in_g, ln_in_b, w_in, g_qn, w_uq, g_kvn, w_uk, w_uv, mu_shift, w0, w_decay, a0, w_iclr, w_gate_out, k_k, k_a, r_k, lnx_g, lnx_b, conv_w, w_out, ln1_g, ln1_b, w_group_router, b_group_router, w_expert_router, b_expert_router, w_exp_gate, w_exp_up, w_exp_down, ln2_g, ln2_b):
    bp, sp, d = x_prompt.shape
    bd, td, _ = x_sample.shape
    past_len = page_table.shape[1] * PAGE_SIZE
    depth = w_in.shape[0]
    np_, ns = bp * sp, bd * td

    ctab_p, stab_p = _rope_tables(jnp.arange(sp))
    ctab_s, stab_s = _rope_tables(jnp.repeat(past_len + jnp.arange(td), bd))

    hp = _ln_in(x_prompt.reshape(np_, d), ln_in_g, ln_in_b, 512)
    hs = _ln_in(jnp.swapaxes(x_sample, 0, 1).reshape(ns, d), ln_in_g, ln_in_b, ns)

    zero_shift = jnp.zeros((bp, RWKV_PROJ), F32)
    zero_wkv = jnp.zeros((bp, RWKV_HEADS, RWKV_HEAD_DIM, RWKV_HEAD_DIM), F32)
    zero_conv = jnp.zeros((bp, 2, CONV_DIM), F32)

    outs_p = [[] for _ in range(5)]
    outs_s = [[] for _ in range(5)]
    for l in range(depth):
        lp = _layer_params(l, w_in, g_qn, w_uq, g_kvn, w_uk, w_uv, mu_shift, w0, w_decay, a0,
                           w_iclr, w_gate_out, k_k, k_a, r_k, lnx_g, lnx_b, conv_w, w_out, ln1_g,
                           ln1_b, w_group_router, b_group_router, w_expert_router, b_expert_router,
                           w_exp_gate, w_exp_up, w_exp_down, ln2_g, ln2_b)

        hp, ckv, kr, p_r, cu, wkv = _layer(
            hp, lp, n_seq=bp, t_len=sp, time_major=False, ctab=ctab_p, stab=stab_p,
            attend=lambda q, kcat: _mla_prompt(q, kcat, bp, sp),
            shift0=zero_shift, wkv0=zero_wkv, conv0=zero_conv)
        outs_p[0].append(ckv.reshape(-1, PAGE_SIZE, MLA_KV_RANK))
        outs_p[1].append(kr.reshape(-1, PAGE_SIZE, MLA_ROPE_DIM))
        outs_p[2].append(wkv)
        outs_p[3].append(p_r.reshape(bp, sp, RWKV_PROJ)[:, -1])
        outs_p[4].append(cu.reshape(bp, sp, CONV_DIM)[:, -2:])

        def attend_s(q, kcat, l=l):
            qs = q.reshape(MLA_HEADS, td, bd, QHEAD).transpose(2, 0, 1, 3).reshape(
                bd, MLA_HEADS * td, QHEAD)
            kn = kcat.reshape(td, bd, KCAT).transpose(1, 0, 2)
            o = _mla_sample(qs, kn, cache_ckv, cache_krope, page_table, l)
            return o.reshape(bd, MLA_HEADS, td, MLA_KV_RANK).transpose(2, 0, 1, 3).reshape(
                ns, MLA_HEADS * MLA_KV_RANK)

        hs, ckv, kr, p_r, cu, wkv = _layer(
            hs, lp, n_seq=bd, t_len=td, time_major=True, ctab=ctab_s, stab=stab_s,
            attend=attend_s, shift0=state_shift[l], wkv0=state_wkv[l], conv0=state_conv[l])
        tmaj = lambda x: jnp.swapaxes(x.reshape(td, bd, -1), 0, 1)
        outs_s[0].append(tmaj(ckv))
        outs_s[1].append(tmaj(kr))
        outs_s[2].append(wkv)
        outs_s[3].append(p_r.reshape(td, bd, RWKV_PROJ)[-1])
        outs_s[4].append(tmaj(cu)[:, -2:])

    y_p = hp.reshape(bp, sp, d)
    y_s = jnp.swapaxes(hs.reshape(td, bd, d), 0, 1)
    return (y_p, y_s) + tuple(jnp.stack(o) for o in outs_p) + tuple(jnp.stack(o) for o in outs_s)
```

```python
import functools

import numpy as np
import jax
import jax.numpy as jnp
from jax import lax
from jax.experimental import pallas as pl
from jax.experimental.pallas import tpu as pltpu

F32 = jnp.float32
BF16 = jnp.bfloat16

MLA_HEADS = 8
MLA_NOPE_DIM = 64
MLA_ROPE_DIM = 32
MLA_Q_RANK = 192
MLA_KV_RANK = 128
MLA_V_DIM = 64
ROPE_THETA = 10000.0
ATTN_SCALE = (MLA_NOPE_DIM + MLA_ROPE_DIM) ** -0.5
RWKV_DIM = 256
RWKV_HEADS = 4
RWKV_HEAD_DIM = 64
RWKV_PROJ = 3 * RWKV_DIM + 64 + 64 + 128
CONV_DIM = 256
GN_EPS = 64e-5
N_GROUPS = 4
EXPERTS_PER_GROUP = 8
N_EXPERTS = N_GROUPS * EXPERTS_PER_GROUP
EXPERT_FF = 256
DEPTH = 2
ALPHA = (2 * DEPTH) ** 0.25
LN_EPS = 1e-5
RMS_EPS = 1e-6
PAGE_SIZE = 128

QPAD = 256
QHEAD = 256
KCAT = 256
COL_CKV = QPAD
COL_KR = COL_CKV + MLA_KV_RANK
COL_RWKV = COL_KR + 128
COL_CONV = COL_RWKV + RWKV_PROJ
IN_PAD = COL_CONV + 3 * CONV_DIM

LANES = 128
ATT_TQ = 128
ATT_TK = 512
PAGES_PER_STEP = 32
NEW_PAD = 16
MOE_ROWS = 256
VMEM_LIMIT = 56 * 1024 * 1024


def _cparams(sem):
    return pltpu.CompilerParams(dimension_semantics=sem, vmem_limit_bytes=VMEM_LIMIT)


def _ln_rows(x, g, b):
    mu = jnp.mean(x, -1, keepdims=True)
    xc = x - mu
    var = jnp.mean(xc * xc, -1, keepdims=True)
    return xc * lax.rsqrt(var + LN_EPS) * g + b


def _sigmoid(x):
    return 1.0 / (1.0 + jnp.exp(-x))


def _ln_in_kernel(x_ref, g_ref, b_ref, o_ref):
    o_ref[...] = _ln_rows(x_ref[...], g_ref[...], b_ref[...])


def _ln_in(x, g, b, tm):
    n, d = x.shape
    row = pl.BlockSpec((tm, d), lambda i: (i, 0))
    vec = pl.BlockSpec((1, d), lambda i: (0, 0))
    return pl.pallas_call(
        _ln_in_kernel, grid=(n // tm,), in_specs=[row, vec, vec], out_specs=row,
        out_shape=jax.ShapeDtypeStruct((n, d), F32), compiler_params=_cparams(("parallel",)),
    )(x, g.reshape(1, d), b.reshape(1, d))


def _ln2_kernel(x_ref, y1_ref, y2_ref, gate_ref, g_ref, b_ref, o_ref):
    gate = gate_ref[...]
    y = gate[:, 0:1] * y1_ref[...].astype(F32) + gate[:, 1:2] * y2_ref[...].astype(F32)
    o_ref[...] = _ln_rows(ALPHA * x_ref[...] + y, g_ref[...], b_ref[...])


def _ln2(x, y1, y2, gate, g, b, tm):
    n, d = x.shape
    row = pl.BlockSpec((tm, d), lambda i: (i, 0))
    vec = pl.BlockSpec((1, d), lambda i: (0, 0))
    return pl.pallas_call(
        _ln2_kernel, grid=(n // tm,),
        in_specs=[row, row, row, pl.BlockSpec((tm, LANES), lambda i: (i, 0)), vec, vec],
        out_specs=row,
        out_shape=jax.ShapeDtypeStruct((n, d), F32), compiler_params=_cparams(("parallel",)),
    )(x, y1, y2, gate, g.reshape(1, d), b.reshape(1, d))


def _in_proj_kernel(h_ref, w_ref, gq_ref, wq_ref, gkv_ref, ct_ref, st_ref,
                    q_ref, kcat_ref, ckv_ref, kr_ref, pr_ref, pc_ref):
    p = jnp.dot(h_ref[...].astype(BF16), w_ref[...], preferred_element_type=F32)
    ct = ct_ref[...]
    st = st_ref[...]

    def rope(tile):
        return tile * ct + pltpu.roll(tile, LANES - MLA_ROPE_DIM, 1) * st

    cq = p[:, :QPAD]
    ms = jnp.sum(cq * cq, -1, keepdims=True) * (1.0 / MLA_Q_RANK)
    cqn = (cq * lax.rsqrt(ms + RMS_EPS) * gq_ref[...]).astype(BF16)
    q = jnp.dot(cqn, wq_ref[...], preferred_element_type=F32)
    for h in range(MLA_HEADS):
        base = h * QHEAD
        q_ref[h, :, :LANES] = q[:, base:base + LANES].astype(BF16)
        q_ref[h, :, LANES:] = rope(q[:, base + LANES:base + QHEAD]).astype(BF16)

    c = p[:, COL_CKV:COL_CKV + MLA_KV_RANK]
    ckv = c * lax.rsqrt(jnp.mean(c * c, -1, keepdims=True) + RMS_EPS) * gkv_ref[...]
    ckv_ref[...] = ckv
    kr = rope(p[:, COL_KR:COL_KR + LANES])
    kr_ref[...] = kr[:, :MLA_ROPE_DIM]
    kcat_ref[:, :LANES] = ckv.astype(BF16)
    kcat_ref[:, LANES:] = kr.astype(BF16)
    pr_ref[...] = p[:, COL_RWKV:COL_RWKV + RWKV_PROJ]
    pc_ref[...] = p[:, COL_CONV:COL_CONV + 3 * CONV_DIM]


def _in_proj(h, w_in_p, gq_p, w_q, g_kvn, ctab, stab, tm):
    n, d = h.shape
    tab_blocks = ctab.shape[0] // tm
    row = lambda c: pl.BlockSpec((tm, c), lambda i: (i, 0))
    full = lambda a: pl.BlockSpec(a.shape, lambda i: (0,) * a.ndim)
    tab = pl.BlockSpec((tm, LANES), lambda i: (i % tab_blocks, 0))
    gkv = g_kvn.reshape(1, MLA_KV_RANK)
    out_shape = (
        jax.ShapeDtypeStruct((MLA_HEADS, n, QHEAD), BF16),
        jax.ShapeDtypeStruct((n, KCAT), BF16),
        jax.ShapeDtypeStruct((n, MLA_KV_RANK), F32),
        jax.ShapeDtypeStruct((n, MLA_ROPE_DIM), F32),
        jax.ShapeDtypeStruct((n, RWKV_PROJ), F32),
        jax.ShapeDtypeStruct((n, 3 * CONV_DIM), F32),
    )
    out_specs = (
        pl.BlockSpec((MLA_HEADS, tm, QHEAD), lambda i: (0, i, 0)),
        row(KCAT), row(MLA_KV_RANK), row(MLA_ROPE_DIM), row(RWKV_PROJ), row(3 * CONV_DIM),
    )
    return pl.pallas_call(
        _in_proj_kernel, grid=(n // tm,),
        in_specs=[row(d), full(w_in_p), full(gq_p), full(w_q), full(gkv), tab, tab],
        out_specs=out_specs, out_shape=out_shape, compiler_params=_cparams(("parallel",)),
    )(h, w_in_p, gq_p, w_q, gkv, ctab, stab)


def _mla_prompt_kernel(q_ref, k_ref, o_ref):
    i = pl.program_id(1)
    rows = MLA_HEADS * ATT_TQ
    q = q_ref[...].reshape(rows, QHEAD)

    def chunk(j, carry, masked):
        m, l, acc = carry
        k = k_ref[pl.ds(pl.multiple_of(j * ATT_TK, ATT_TK), ATT_TK), :]
        s = lax.dot_general(q, k, (((1,), (1,)), ((), ())), preferred_element_type=F32)
        if masked:
            qpos = i * ATT_TQ + lax.broadcasted_iota(
                jnp.int32, (MLA_HEADS, ATT_TQ, ATT_TK), 1).reshape(rows, ATT_TK)
            kpos = j * ATT_TK + lax.broadcasted_iota(jnp.int32, (rows, ATT_TK), 1)
            s = jnp.where(kpos <= qpos, s, -jnp.inf)
        m_new = jnp.maximum(m, jnp.max(s, -1, keepdims=True))
        a = jnp.exp(m - m_new)
        pr = jnp.exp(s - m_new)
        l = a * l + jnp.sum(pr, -1, keepdims=True)
        acc = a * acc + jnp.dot(pr.astype(BF16), k[:, :MLA_KV_RANK], preferred_element_type=F32)
        return m_new, l, acc

    init = (jnp.full((rows, 1), -jnp.inf, F32), jnp.zeros((rows, 1), F32),
            jnp.zeros((rows, MLA_KV_RANK), F32))
    n_full = (i * ATT_TQ) // ATT_TK
    carry = lax.fori_loop(0, n_full, lambda j, c: chunk(j, c, False), init)
    m, l, acc = chunk(n_full, carry, True)
    o = acc / l
    for h in range(MLA_HEADS):
        o_ref[:, h * MLA_KV_RANK:(h + 1) * MLA_KV_RANK] = o[h * ATT_TQ:(h + 1) * ATT_TQ].astype(BF16)


def _mla_prompt(q, kcat, batch, seq):
    n = batch * seq
    nq = seq // ATT_TQ
    return pl.pallas_call(
        _mla_prompt_kernel, grid=(batch, nq),
        in_specs=[pl.BlockSpec((MLA_HEADS, ATT_TQ, QHEAD), lambda b, i: (0, b * nq + i, 0)),
                  pl.BlockSpec((seq, KCAT), lambda b, i: (b, 0))],
        out_specs=pl.BlockSpec((ATT_TQ, MLA_HEADS * MLA_KV_RANK), lambda b, i: (b * nq + i, 0)),
        out_shape=jax.ShapeDtypeStruct((n, MLA_HEADS * MLA_KV_RANK), BF16),
        compiler_params=_cparams(("parallel", "arbitrary")),
    )(q, kcat)


def _mla_sample_kernel(pt_ref, q_ref, kn_ref, *rest, n_new):
    npg = PAGES_PER_STEP
    ckv_refs = rest[:npg]
    krt_refs = rest[npg:2 * npg]
    o_ref, kbuf, rbuf, m_ref, l_ref, acc_ref = rest[2 * npg:]
    c = pl.program_id(1)

    @pl.when(c == 0)
    def _():
        m_ref[...] = jnp.full_like(m_ref, -jnp.inf)
        l_ref[...] = jnp.zeros_like(l_ref)
        acc_ref[...] = jnp.zeros_like(acc_ref)

    q = q_ref[...]
    q_lat = q[:, :MLA_KV_RANK]
    q_rope = q[:, MLA_KV_RANK:MLA_KV_RANK + MLA_ROPE_DIM]
    nt = (((1,), (1,)), ((), ()))

    def update(s, v):
        m = m_ref[...]
        m_new = jnp.maximum(m, jnp.max(s, -1, keepdims=True))
        a = jnp.exp(m - m_new)
        pr = jnp.exp(s - m_new)
        l_ref[...] = a * l_ref[...] + jnp.sum(pr, -1, keepdims=True)
        acc_ref[...] = a * acc_ref[...] + jnp.dot(pr.astype(BF16), v, preferred_element_type=F32)
        m_ref[...] = m_new

    for j in range(npg):
        kbuf[j * PAGE_SIZE:(j + 1) * PAGE_SIZE, :] = ckv_refs[j][...].astype(BF16)
        rbuf[:, j * PAGE_SIZE:(j + 1) * PAGE_SIZE] = krt_refs[j][...].astype(BF16)
    k = kbuf[...]
    s = (lax.dot_general(q_lat, k, nt, preferred_element_type=F32)
         + jnp.dot(q_rope, rbuf[...], preferred_element_type=F32))
    update(s, k)

    @pl.when(c == pl.num_programs(1) - 1)
    def _():
        kn = kn_ref[...]
        sn = lax.dot_general(q, kn, nt, preferred_element_type=F32)
        t_row = lax.broadcasted_iota(jnp.int32, sn.shape, 0) % n_new
        sn = jnp.where(lax.broadcasted_iota(jnp.int32, sn.shape, 1) <= t_row, sn, -jnp.inf)
        update(sn, kn[:, :MLA_KV_RANK])
        o_ref[...] = (acc_ref[...] / l_ref[...]).astype(BF16)


def _mla_sample(q, k_new, cache_ckv, cache_krope_t, page_table, layer):
    bd, rows, _ = q.shape
    n_new = k_new.shape[1]
    k_new = jnp.pad(k_new, ((0, 0), (0, NEW_PAD - n_new), (0, 0)))
    n_pages = page_table.shape[1]
    npg = PAGES_PER_STEP
    assert n_pages % npg == 0
    page = lambda shape, j: pl.BlockSpec(
        (None, None) + shape, lambda b, c, pt: (layer, pt[b, c * npg + j], 0, 0))
    in_specs = ([pl.BlockSpec((None, rows, QHEAD), lambda b, c, pt: (b, 0, 0)),
                 pl.BlockSpec((None, NEW_PAD, KCAT), lambda b, c, pt: (b, 0, 0))]
                + [page((PAGE_SIZE, MLA_KV_RANK), j) for j in range(npg)]
                + [page((MLA_ROPE_DIM, PAGE_SIZE), j) for j in range(npg)])
    grid_spec = pltpu.PrefetchScalarGridSpec(
        num_scalar_prefetch=1, grid=(bd, n_pages // npg), in_specs=in_specs,
        out_specs=pl.BlockSpec((None, rows, MLA_KV_RANK), lambda b, c, pt: (b, 0, 0)),
        scratch_shapes=[pltpu.VMEM((npg * PAGE_SIZE, MLA_KV_RANK), BF16),
                        pltpu.VMEM((MLA_ROPE_DIM, npg * PAGE_SIZE), BF16),
                        pltpu.VMEM((rows, 1), F32), pltpu.VMEM((rows, 1), F32),
                        pltpu.VMEM((rows, MLA_KV_RANK), F32)])
    return pl.pallas_call(
        functools.partial(_mla_sample_kernel, n_new=n_new), grid_spec=grid_spec,
        out_shape=jax.ShapeDtypeStruct((bd, rows, MLA_KV_RANK), BF16),
        compiler_params=_cparams(("parallel", "arbitrary")),
    )(page_table, q, k_new, *([cache_ckv] * npg), *([cache_krope_t] * npg))


def _prev_rows(x, before, shift):
    if shift == 1:
        row = lax.broadcasted_iota(jnp.int32, x.shape, 0)
        return jnp.where(row == 0, before, pltpu.roll(x, 1, 0))
    return jnp.concatenate([before, x[:x.shape[0] - shift]], axis=0)


def _mix_prep_kernel(pr_ref, pc_ref, sh0_ref, cv0_ref, mu_ref, w0_ref, a0_ref, wdec_ref, wiclr_ref,
                     wgate_ref, cw_ref,
                     r_ref, k_ref, v_ref, w_ref, a_ref, g_ref, co_ref, cu_ref,
                     sh_c, cu1_c, cu2_c, *, shift, carry):
    first = pl.program_id(1) == 0

    if carry:
        @pl.when(first)
        def _():
            sh_c[...] = sh0_ref[...]
            cu1_c[...] = cv0_ref[1]
            cu2_c[...] = cv0_ref[0]
        sh_before, cu1_before, cu2_before = sh_c[...], cu1_c[...], cu2_c[...]
    else:
        sh_before, cu1_before, cu2_before = sh0_ref[...], cv0_ref[1], cv0_ref[0]

    p = pr_ref[...]
    xs = p + (_prev_rows(p, sh_before, shift) - p) * mu_ref[...]
    r_ref[...] = xs[:, :RWKV_DIM]
    k_ref[...] = xs[:, RWKV_DIM:2 * RWKV_DIM]
    v_ref[...] = xs[:, 2 * RWKV_DIM:3 * RWKV_DIM]
    lora = xs[:, 3 * RWKV_DIM:3 * RWKV_DIM + LANES]
    z = w0_ref[...] + jnp.dot(jnp.tanh(lora).astype(BF16), wdec_ref[...], preferred_element_type=F32)
    nz = -z
    softplus = jnp.maximum(nz, 0.0) + jnp.log(1.0 + jnp.exp(-jnp.abs(nz)))
    w_ref[...] = jnp.exp(-jnp.exp(-softplus - 0.5))
    a_ref[...] = _sigmoid(a0_ref[...] + jnp.dot(lora.astype(BF16), wiclr_ref[...],
                                                preferred_element_type=F32))
    xg = xs[:, 3 * RWKV_DIM + LANES:]
    g_ref[...] = jnp.dot(_sigmoid(xg).astype(BF16), wgate_ref[...], preferred_element_type=F32)

    pc = pc_ref[...]
    cu = pc[:, CONV_DIM:2 * CONV_DIM] * pc[:, 2 * CONV_DIM:]
    cu1 = _prev_rows(cu, cu1_before, shift)
    cu2 = _prev_rows(cu1, cu2_before, shift)
    cw = cw_ref[...]
    co_ref[...] = (pc[:, :CONV_DIM] * (cw[0:1] * cu2 + cw[1:2] * cu1 + cw[2:3] * cu)).astype(BF16)
    cu_ref[...] = cu

    if carry:
        tm = p.shape[0]
        sh_c[...] = p[tm - 1:tm]
        cu1_c[...] = cu[tm - 1:tm]
        cu2_c[...] = cu1[tm - 1:tm]


def _mix_prep(p_r, p_c, shift0, conv0, mu, w0, a0, wdec_p, wiclr_p, wgate, conv_w, *, n_seq, tm, shift):
    n = p_r.shape[0]
    carry = shift == 1
    tps = n // n_seq // tm if carry else 1
    grid = (n_seq, tps) if carry else (1, 1)
    row = lambda c: pl.BlockSpec((tm, c), lambda b, t: (b * tps + t, 0))
    full = lambda a: pl.BlockSpec(a.shape, lambda b, t: (0,) * a.ndim)
    if carry:
        sh_spec = pl.BlockSpec((None, 1, RWKV_PROJ), lambda b, t: (b, 0, 0))
        cv_spec = pl.BlockSpec((None, 2, 1, CONV_DIM), lambda b, t: (b, 0, 0, 0))
        shift0 = shift0.reshape(n_seq, 1, RWKV_PROJ)
        conv0 = conv0.reshape(n_seq, 2, 1, CONV_DIM)
    else:
        sh_spec = full(shift0)
        conv0 = jnp.swapaxes(conv0, 0, 1)
        cv_spec = full(conv0)
    vec = lambda a: a.reshape(1, -1)
    args = (p_r, p_c, shift0, conv0, vec(mu), vec(w0), vec(a0), wdec_p, wiclr_p, wgate, conv_w)
    in_specs = [row(RWKV_PROJ), row(3 * CONV_DIM), sh_spec, cv_spec] + [full(a) for a in args[4:]]
    f32o = jax.ShapeDtypeStruct((n, RWKV_DIM), F32)
    out_shape = (f32o,) * 6 + (jax.ShapeDtypeStruct((n, CONV_DIM), BF16), f32o)
    rows_before = shift
    return pl.pallas_call(
        functools.partial(_mix_prep_kernel, shift=shift, carry=carry), grid=grid,
        in_specs=in_specs, out_specs=(row(RWKV_DIM),) * 8, out_shape=out_shape,
        scratch_shapes=[pltpu.VMEM((rows_before, RWKV_PROJ), F32),
                        pltpu.VMEM((rows_before, CONV_DIM), F32),
                        pltpu.VMEM((rows_before, CONV_DIM), F32)],
        compiler_params=_cparams(("parallel", "arbitrary")),
    )(*args)


def _wkv_kernel(r_ref, k_ref, v_ref, w_ref, a_ref, kk_ref, ka_ref, rk_ref, gn_ref, bn_ref, s0_ref,
                y_ref, sout_ref, s_ref, al_ref, wr_ref, be_ref, km_ref, wd_ref):
    tb = pl.program_id(1)
    hd = RWKV_HEAD_DIM

    @pl.when(tb == 0)
    def _():
        s_ref[...] = s0_ref[...]

    kk_t, ka_t, rk_t = kk_ref[...], ka_ref[...], rk_ref[...]
    gn_t, bn_t = gn_ref[...], bn_ref[...]

    def step(t, _):
        r, k, v, w, a = r_ref[t], k_ref[t], v_ref[t], w_ref[t], a_ref[t]
        kk = k * kk_t
        kk = kk / jnp.maximum(jnp.sqrt(jnp.sum(kk * kk, 0, keepdims=True)), 1e-12)
        km = k * (1.0 + (a - 1.0) * ka_t)
        be = kk * a
        al_ref[...] = -kk
        wr_ref[...] = w * r
        be_ref[...] = be
        km_ref[...] = km
        wd_ref[...] = w
        sa = jnp.zeros((hd, LANES), F32)
        y0 = jnp.zeros((hd, LANES), F32)
        for i in range(hd):
            s_i = s_ref[i]
            sa = sa + s_i * al_ref[i:i + 1, :]
            y0 = y0 + s_i * wr_ref[i:i + 1, :]
        br = jnp.sum(be * r, 0, keepdims=True)
        kr = jnp.sum(km * r, 0, keepdims=True)
        y = y0 + sa * br + v * kr
        for i in range(hd):
            s_ref[i] = s_ref[i] * wd_ref[i:i + 1, :] + sa * be_ref[i:i + 1, :] + v * km_ref[i:i + 1, :]
        mean = jnp.mean(y, 0, keepdims=True)
        yc = y - mean
        var = jnp.mean(yc * yc, 0, keepdims=True)
        bonus = jnp.sum(r * km * rk_t, 0, keepdims=True) * v
        y_ref[t] = yc * lax.rsqrt(var + GN_EPS) * gn_t + bn_t + bonus
        return 0

    lax.fori_loop(0, r_ref.shape[0], step, 0)

    @pl.when(tb == pl.num_programs(1) - 1)
    def _():
        sout_ref[...] = s_ref[...]


def _wkv(r, k, v, w, a, kk_t, ka_t, rk_t, gn_t, bn_t, s0, tt):
    t_len, hd, lanes = r.shape
    seq = pl.BlockSpec((tt, hd, LANES), lambda g, t: (t, 0, g))
    par = pl.BlockSpec((hd, LANES), lambda g, t: (0, g))
    st = pl.BlockSpec((hd, hd, LANES), lambda g, t: (0, 0, g))
    return pl.pallas_call(
        _wkv_kernel, grid=(lanes // LANES, t_len // tt),
        in_specs=[seq] * 5 + [par] * 5 + [st], out_specs=(seq, st),
        out_shape=(jax.ShapeDtypeStruct((t_len, hd, lanes), F32),
                   jax.ShapeDtypeStruct((hd, hd, lanes), F32)),
        scratch_shapes=[pltpu.VMEM((hd, hd, LANES), F32)] + [pltpu.VMEM((hd, LANES), F32)] * 5,
        compiler_params=_cparams(("parallel", "arbitrary")),
    )(r, k, v, w, a, kk_t, ka_t, rk_t, gn_t, bn_t, s0)


def _mix_out_kernel(h_ref, o_ref, y_ref, g_ref, co_ref, wo_ref, g1_ref, b1_ref, wr_ref, br_ref,
                    x_ref, xb_ref, gate_ref, route_ref, cnt_ref):
    mix = jnp.concatenate(
        [o_ref[...], (y_ref[...] * g_ref[...]).astype(BF16), co_ref[...]], axis=-1)
    pre = ALPHA * h_ref[...] + jnp.dot(mix, wo_ref[...], preferred_element_type=F32)
    x = _ln_rows(pre, g1_ref[...], b1_ref[...])
    x_ref[...] = x
    xb = x.astype(BF16)
    xb_ref[...] = xb

    logits = jnp.dot(xb, wr_ref[...], preferred_element_type=F32) + br_ref[...]
    lane = lax.broadcasted_iota(jnp.int32, logits.shape, 1)
    ninf = -jnp.inf
    lg = jnp.where(lane < N_GROUPS, logits, ninf)
    mg = jnp.max(lg, -1, keepdims=True)
    g_p = 1.0 / jnp.sum(jnp.exp(lg - mg), -1, keepdims=True)
    g_idx = jnp.min(jnp.where(lg == mg, lane, LANES), -1, keepdims=True)
    lo = N_GROUPS + EXPERTS_PER_GROUP * g_idx
    le = jnp.where((lane >= lo) & (lane < lo + EXPERTS_PER_GROUP), logits, ninf)
    m1 = jnp.max(le, -1, keepdims=True)
    i1 = jnp.min(jnp.where(le == m1, lane, LANES), -1, keepdims=True)
    le2 = jnp.where(lane == i1, ninf, le)
    m2 = jnp.max(le2, -1, keepdims=True)
    i2 = jnp.min(jnp.where(le2 == m2, lane, LANES), -1, keepdims=True)
    e2 = jnp.exp(m2 - m1)
    gate1 = g_p / (1.0 + e2)
    gate2 = g_p * e2 / (1.0 + e2)
    gate_ref[...] = jnp.where(lane == 0, gate1, jnp.where(lane == 1, gate2, 0.0))

    pick1, pick2 = lane == i1, lane == i2
    picks = jnp.where(pick1 | pick2, 1.0, 0.0)
    tm = logits.shape[0]
    tri = jnp.where(lax.broadcasted_iota(jnp.int32, (tm, tm), 1)
                    < lax.broadcasted_iota(jnp.int32, (tm, tm), 0), 1.0, 0.0).astype(BF16)
    before = jnp.dot(tri, picks.astype(BF16), preferred_element_type=F32)
    rank1 = jnp.sum(jnp.where(pick1, before, 0.0), -1, keepdims=True).astype(jnp.int32)
    rank2 = jnp.sum(jnp.where(pick2, before, 0.0), -1, keepdims=True).astype(jnp.int32)
    route_ref[...] = jnp.where(lane == 0, i1, jnp.where(lane == 1, i2, jnp.where(
        lane == 2, rank1, jnp.where(lane == 3, rank2, 0))))
    cnt_ref[...] = jnp.sum(picks, 0, keepdims=True).astype(jnp.int32)


def _moe_pos_kernel(route_ref, base_ref, pos_ref):
    route = route_ref[...]
    base = base_ref[...]
    lane = lax.broadcasted_iota(jnp.int32, route.shape, 1)
    pos = []
    for c in range(2):
        start = jnp.sum(jnp.where(lane == route[:, c:c + 1], base, 0), -1, keepdims=True)
        pos.append(start + route[:, 2 + c:3 + c])
    pos_ref[...] = jnp.where(lane == 0, pos[0], jnp.where(lane == 1, pos[1], 0))


def _moe_pos(route, base, tm):
    n = route.shape[0]
    row = pl.BlockSpec((tm, LANES), lambda i: (i, 0))
    return pl.pallas_call(
        _moe_pos_kernel, grid=(n // tm,),
        in_specs=[row, pl.BlockSpec((None, 1, LANES), lambda i: (i, 0, 0))], out_specs=row,
        out_shape=jax.ShapeDtypeStruct((n, LANES), jnp.int32),
        compiler_params=_cparams(("parallel",)),
    )(route, base)


def _mix_out(h, o_lat, y, g, c_out, w_o, ln_g, ln_b, w_router, b_router, tm):
    n, d = h.shape
    row = lambda c: pl.BlockSpec((tm, c), lambda i: (i, 0))
    full = lambda a: pl.BlockSpec(a.shape, lambda i: (0,) * a.ndim)
    vec = lambda a: a.reshape(1, -1)
    args = (h, o_lat, y, g, c_out, w_o, vec(ln_g), vec(ln_b), w_router, b_router)
    in_specs = [row(d), row(o_lat.shape[1]), row(RWKV_DIM), row(RWKV_DIM), row(CONV_DIM)] + [
        full(a) for a in args[5:]]
    out_shape = (jax.ShapeDtypeStruct((n, d), F32), jax.ShapeDtypeStruct((n, d), BF16),
                 jax.ShapeDtypeStruct((n, LANES), F32), jax.ShapeDtypeStruct((n, LANES), jnp.int32),
                 jax.ShapeDtypeStruct((n // tm, 1, LANES), jnp.int32))
    return pl.pallas_call(
        _mix_out_kernel, grid=(n // tm,), in_specs=in_specs,
        out_specs=(row(d), row(d), row(LANES), row(LANES),
                   pl.BlockSpec((None, 1, LANES), lambda i: (i, 0, 0))),
        out_shape=out_shape, compiler_params=_cparams(("parallel",)),
    )(*args)


def _moe_kernel(be_ref, x_ref, wgu_ref, wd_ref, o_ref):
    gu = jnp.dot(x_ref[...], wgu_ref[...], preferred_element_type=F32)
    hg = gu[:, :EXPERT_FF]
    hid = hg * _sigmoid(hg) * gu[:, EXPERT_FF:]
    o_ref[...] = jnp.dot(hid.astype(BF16), wd_ref[...], preferred_element_type=F32).astype(BF16)


def _moe_experts(blk_e, xb, w_gu, w_d):
    cap, d = xb.shape
    grid_spec = pltpu.PrefetchScalarGridSpec(
        num_scalar_prefetch=1, grid=(cap // MOE_ROWS,),
        in_specs=[pl.BlockSpec((MOE_ROWS, d), lambda i, be: (i, 0)),
                  pl.BlockSpec((None, d, 2 * EXPERT_FF), lambda i, be: (be[i], 0, 0)),
                  pl.BlockSpec((None, EXPERT_FF, d), lambda i, be: (be[i], 0, 0))],
        out_specs=pl.BlockSpec((MOE_ROWS, d), lambda i, be: (i, 0)))
    return pl.pallas_call(
        _moe_kernel, grid_spec=grid_spec, out_shape=jax.ShapeDtypeStruct((cap, d), BF16),
        compiler_params=_cparams(("arbitrary",)),
    )(blk_e, xb, w_gu, w_d)


def _moe_layout(tile_counts, n_tok):
    cnt = tile_counts[:, 0, N_GROUPS:N_GROUPS + N_EXPERTS]
    tile_off = jnp.cumsum(cnt, axis=0) - cnt
    total = jnp.sum(cnt, axis=0)
    pcounts = (total + MOE_ROWS - 1) // MOE_ROWS * MOE_ROWS
    pends = jnp.cumsum(pcounts)
    base = (pends - pcounts)[None, :] + tile_off
    base = jnp.pad(base, ((0, 0), (N_GROUPS, LANES - N_GROUPS - N_EXPERTS)))[:, None, :]
    n_blocks = 2 * n_tok // MOE_ROWS + N_EXPERTS
    blk_start = jnp.arange(n_blocks, dtype=jnp.int32) * MOE_ROWS
    blk_e = jnp.minimum(jnp.sum((pends[None, :] <= blk_start[:, None]).astype(jnp.int32), axis=1),
                        N_EXPERTS - 1)
    return base.astype(jnp.int32), blk_e.astype(jnp.int32), n_blocks * MOE_ROWS


def _rope_tables(pos):
    half = MLA_ROPE_DIM // 2
    freqs = ROPE_THETA ** (-jnp.arange(half, dtype=F32) / half)
    ang = pos.astype(F32)[:, None] * freqs
    cos, sin = jnp.cos(ang), jnp.sin(ang)
    pad = jnp.zeros((pos.shape[0], LANES - MLA_ROPE_DIM), F32)
    return (jnp.concatenate([cos, cos, pad], -1), jnp.concatenate([-sin, sin, pad], -1))


def _swap_halves(w):
    half = MLA_ROPE_DIM // 2
    return jnp.concatenate([w[..., half:], w[..., :half]], -1)


def _layer_params(l, w_in, g_qn, w_uq, g_kvn, w_uk, w_uv, mu_shift, w0, w_decay, a0, w_iclr,
                  w_gate_out, k_k, k_a, r_k, lnx_g, lnx_b, conv_w, w_out, ln1_g, ln1_b,
                  w_group_router, b_group_router, w_expert_router, b_expert_router,
                  w_exp_gate, w_exp_up, w_exp_down, ln2_g, ln2_b):
    d = w_in.shape[1]
    hi = lax.Precision.HIGHEST
    wi = w_in[l]
    zc = lambda c: jnp.zeros((d, c), F32)
    c0 = MLA_Q_RANK
    c1 = c0 + MLA_KV_RANK
    c2 = c1 + MLA_ROPE_DIM
    w_kr = wi[:, c1:c2]
    w_in_p = jnp.concatenate([
        wi[:, :c0], zc(QPAD - MLA_Q_RANK), wi[:, c0:c1],
        w_kr, _swap_halves(w_kr), zc(LANES - 2 * MLA_ROPE_DIM), wi[:, c2:]], -1).astype(BF16)
    gq_p = jnp.concatenate([g_qn[l], jnp.zeros((QPAD - MLA_Q_RANK,), F32)]).reshape(1, QPAD)
    uq = w_uq[l]
    q_lat = jnp.einsum('rhn,chn->rhc', uq[..., :MLA_NOPE_DIM], w_uk[l], precision=hi)
    q_rope = uq[..., MLA_NOPE_DIM:]
    wq = jnp.concatenate([q_lat, q_rope, _swap_halves(q_rope),
                          jnp.zeros((MLA_Q_RANK, MLA_HEADS, QHEAD - LANES - 2 * MLA_ROPE_DIM), F32)], -1)
    wq = (wq * ATTN_SCALE).reshape(MLA_Q_RANK, MLA_HEADS * QHEAD)
    wq = jnp.concatenate([wq, jnp.zeros((QPAD - MLA_Q_RANK, MLA_HEADS * QHEAD), F32)], 0).astype(BF16)
    wo = w_out[l]
    mla_dim = MLA_HEADS * MLA_V_DIM
    wo_att = jnp.einsum('chv,hvd->hcd', w_uv[l], wo[:mla_dim].reshape(MLA_HEADS, MLA_V_DIM, -1),
                        precision=hi).reshape(MLA_HEADS * MLA_KV_RANK, -1)
    w_o = jnp.concatenate([wo_att, wo[mla_dim:]], 0).astype(BF16)
    z64 = jnp.zeros((64, RWKV_DIM), F32)
    wdec_p = jnp.concatenate([w_decay[l], z64], 0).astype(BF16)
    wiclr_p = jnp.concatenate([z64, w_iclr[l]], 0).astype(BF16)
    w_router = jnp.concatenate(
        [w_group_router[l], w_expert_router[l],
         jnp.zeros((d, LANES - N_GROUPS - N_EXPERTS), F32)], -1).astype(BF16)
    b_router = jnp.concatenate(
        [b_group_router[l], b_expert_router[l],
         jnp.zeros((LANES - N_GROUPS - N_EXPERTS,), F32)]).reshape(1, LANES)
    w_gu = jnp.concatenate([w_exp_gate[l], w_exp_up[l]], -1).astype(BF16)
    return dict(
        w_in_p=w_in_p, gq_p=gq_p, wq=wq, g_kvn=g_kvn[l], mu=mu_shift[l], w0=w0[l], a0=a0[l],
        wdec_p=wdec_p, wiclr_p=wiclr_p, wgate=w_gate_out[l].astype(BF16), conv_w=conv_w[l],
        k_k=k_k[l], k_a=k_a[l], r_k=r_k[l].reshape(-1), lnx_g=lnx_g[l], lnx_b=lnx_b[l],
        w_o=w_o, ln1_g=ln1_g[l], ln1_b=ln1_b[l], w_router=w_router, b_router=b_router,
        w_gu=w_gu, w_d=w_exp_down[l].astype(BF16), ln2_g=ln2_g[l], ln2_b=ln2_b[l])


def _head_tile(p, n_seq):
    t = p.reshape(RWKV_HEADS, RWKV_HEAD_DIM).T
    return jnp.tile(t, (1, n_seq))


def _layer(h, lp, *, n_seq, t_len, time_major, ctab, stab, attend, shift0, wkv0, conv0):
    n, d = h.shape
    tm = min(512, n)
    q, kcat, ckv, kr, p_r, p_c = _in_proj(h, lp['w_in_p'], lp['gq_p'], lp['wq'], lp['g_kvn'],
                                          ctab, stab, tm)
    o_lat = attend(q, kcat)

    r, k, v, w, a, g, c_out, cu = _mix_prep(
        p_r, p_c, shift0, conv0, lp['mu'], lp['w0'], lp['a0'], lp['wdec_p'], lp['wiclr_p'],
        lp['wgate'], lp['conv_w'], n_seq=n_seq, tm=tm, shift=n_seq if time_major else 1)

    lanes = n_seq * RWKV_HEADS

    def to_scan(x):
        if time_major:
            x = x.reshape(t_len, n_seq, RWKV_HEADS, RWKV_HEAD_DIM).transpose(0, 3, 1, 2)
        else:
            x = x.reshape(n_seq, t_len, RWKV_HEADS, RWKV_HEAD_DIM).transpose(1, 3, 0, 2)
        return x.reshape(t_len, RWKV_HEAD_DIM, lanes)

    tiles = [_head_tile(lp[name], n_seq) for name in ('k_k', 'k_a', 'r_k', 'lnx_g', 'lnx_b')]
    s0 = wkv0.transpose(3, 2, 0, 1).reshape(RWKV_HEAD_DIM, RWKV_HEAD_DIM, lanes)
    y, s_fin = _wkv(*[to_scan(x) for x in (r, k, v, w, a)], *tiles, s0, min(16, t_len))
    y = y.reshape(t_len, RWKV_HEAD_DIM, n_seq, RWKV_HEADS)
    y = (y.transpose(0, 2, 3, 1) if time_major else y.transpose(2, 0, 3, 1)).reshape(n, RWKV_DIM)
    wkv_new = s_fin.reshape(RWKV_HEAD_DIM, RWKV_HEAD_DIM, n_seq, RWKV_HEADS).transpose(2, 3, 1, 0)

    x, xb, gate, route, tile_counts = _mix_out(
        h, o_lat, y, g, c_out, lp['w_o'], lp['ln1_g'], lp['ln1_b'], lp['w_router'], lp['b_router'], tm)
    base, blk_e, cap = _moe_layout(tile_counts, n)
    pos = _moe_pos(route, base, tm)
    pos1, pos2 = pos[:, 0], pos[:, 1]
    tok = jnp.arange(n, dtype=jnp.int32)
    slot_tok = jnp.zeros((cap,), jnp.int32).at[jnp.concatenate([pos1, pos2])].set(
        jnp.concatenate([tok, tok]), unique_indices=True, mode='promise_in_bounds')
    xs = xb.at[slot_tok].get(mode='promise_in_bounds')
    yb = _moe_experts(blk_e, xs, lp['w_gu'], lp['w_d'])
    y1 = yb.at[pos1].get(mode='promise_in_bounds')
    y2 = yb.at[pos2].get(mode='promise_in_bounds')
    h_next = _ln2(x, y1, y2, gate, lp['ln2_g'], lp['ln2_b'], tm)
    return h_next, ckv, kr, p_r, cu, wkv_new


def kernel(x_prompt, x_sample, cache_ckv, cache_krope, state_wkv, state_shift, state_conv, page_table, ln_in_g, ln_in_b, w_in, g_qn, w_uq, g_kvn, w_uk, w_uv, mu_shift, w0, w_decay, a0, w_iclr, w_gate_out, k_k, k_a, r_k, lnx_g, lnx_b, conv_w, w_out, ln1_g, ln1_b, w_group_router, b_group_router, w_expert_router, b_expert_router, w_exp_gate, w_exp_up, w_exp_down, ln2_g, ln2_b):
    bp, sp, d = x_prompt.shape
    bd, td, _ = x_sample.shape
    past_len = page_table.shape[1] * PAGE_SIZE
    depth = w_in.shape[0]
    np_, ns = bp * sp, bd * td

    ctab_p, stab_p = _rope_tables(jnp.arange(sp))
    ctab_s, stab_s = _rope_tables(jnp.repeat(past_len + jnp.arange(td), bd))

    hp = _ln_in(x_prompt.reshape(np_, d), ln_in_g, ln_in_b, 512)
    hs = _ln_in(jnp.swapaxes(x_sample, 0, 1).reshape(ns, d), ln_in_g, ln_in_b, ns)

    krope_t = jnp.swapaxes(cache_krope, 2, 3)
    zero_shift = jnp.zeros((bp, RWKV_PROJ), F32)
    zero_wkv = jnp.zeros((bp, RWKV_HEADS, RWKV_HEAD_DIM, RWKV_HEAD_DIM), F32)
    zero_conv = jnp.zeros((bp, 2, CONV_DIM), F32)

    outs_p = [[] for _ in range(5)]
    outs_s = [[] for _ in range(5)]
    for l in range(depth):
        lp = _layer_params(l, w_in, g_qn, w_uq, g_kvn, w_uk, w_uv, mu_shift, w0, w_decay, a0,
                           w_iclr, w_gate_out, k_k, k_a, r_k, lnx_g, lnx_b, conv_w, w_out, ln1_g,
                           ln1_b, w_group_router, b_group_router, w_expert_router, b_expert_router,
                           w_exp_gate, w_exp_up, w_exp_down, ln2_g, ln2_b)

        hp, ckv, kr, p_r, cu, wkv = _layer(
            hp, lp, n_seq=bp, t_len=sp, time_major=False, ctab=ctab_p, stab=stab_p,
            attend=lambda q, kcat: _mla_prompt(q, kcat, bp, sp),
            shift0=zero_shift, wkv0=zero_wkv, conv0=zero_conv)
        outs_p[0].append(ckv.reshape(-1, PAGE_SIZE, MLA_KV_RANK))
        outs_p[1].append(kr.reshape(-1, PAGE_SIZE, MLA_ROPE_DIM))
        outs_p[2].append(wkv)
        outs_p[3].append(p_r.reshape(bp, sp, RWKV_PROJ)[:, -1])
        outs_p[4].append(cu.reshape(bp, sp, CONV_DIM)[:, -2:])

        def attend_s(q, kcat, l=l):
            qs = q.reshape(MLA_HEADS, td, bd, QHEAD).transpose(2, 0, 1, 3).reshape(
                bd, MLA_HEADS * td, QHEAD)
            kn = kcat.reshape(td, bd, KCAT).transpose(1, 0, 2)
            o = _mla_sample(qs, kn, cache_ckv, krope_t, page_table, l)
            return o.reshape(bd, MLA_HEADS, td, MLA_KV_RANK).transpose(2, 0, 1, 3).reshape(
                ns, MLA_HEADS * MLA_KV_RANK)

        hs, ckv, kr, p_r, cu, wkv = _layer(
            hs, lp, n_seq=bd, t_len=td, time_major=True, ctab=ctab_s, stab=stab_s,
            attend=attend_s, shift0=state_shift[l], wkv0=state_wkv[l], conv0=state_conv[l])
        tmaj = lambda x: jnp.swapaxes(x.reshape(td, bd, -1), 0, 1)
        outs_s[0].append(tmaj(ckv))
        outs_s[1].append(tmaj(kr))
        outs_s[2].append(wkv)
        outs_s[3].append(p_r.reshape(td, bd, RWKV_PROJ)[-1])
        outs_s[4].append(tmaj(cu)[:, -2:])

    y_p = hp.reshape(bp, sp, d)
    y_s = jnp.swapaxes(hs.reshape(td, bd, d), 0, 1)
    return (y_p, y_s) + tuple(jnp.stack(o) for o in outs_p) + tuple(jnp.stack(o) for o in outs_s)
```

```python
import functools

import numpy as np
import jax
import jax.numpy as jnp
from jax import lax
from jax.experimental import pallas as pl
from jax.experimental.pallas import tpu as pltpu
from jax.experimental.pallas import tpu_sc as plsc

F32 = jnp.float32
BF16 = jnp.bfloat16

MLA_HEADS = 8
MLA_NOPE_DIM = 64
MLA_ROPE_DIM = 32
MLA_Q_RANK = 192
MLA_KV_RANK = 128
MLA_V_DIM = 64
ROPE_THETA = 10000.0
ATTN_SCALE = (MLA_NOPE_DIM + MLA_ROPE_DIM) ** -0.5
RWKV_DIM = 256
RWKV_HEADS = 4
RWKV_HEAD_DIM = 64
RWKV_PROJ = 3 * RWKV_DIM + 64 + 64 + 128
CONV_DIM = 256
GN_EPS = 64e-5
N_GROUPS = 4
EXPERTS_PER_GROUP = 8
N_EXPERTS = N_GROUPS * EXPERTS_PER_GROUP
EXPERT_FF = 256
DEPTH = 2
ALPHA = (2 * DEPTH) ** 0.25
LN_EPS = 1e-5
RMS_EPS = 1e-6
PAGE_SIZE = 128

QPAD = 256
QHEAD = 256
KCAT = 256
COL_CKV = QPAD
COL_KR = COL_CKV + MLA_KV_RANK
COL_RWKV = COL_KR + 128
COL_CONV = COL_RWKV + RWKV_PROJ
IN_PAD = COL_CONV + 3 * CONV_DIM

LANES = 128
ATT_TQ = 128
ATT_TK = 512
PAGES_PER_STEP = 32
NEW_PAD = 16
MOE_ROWS = 256
VMEM_LIMIT = 56 * 1024 * 1024


def _cparams(sem):
    return pltpu.CompilerParams(dimension_semantics=sem, vmem_limit_bytes=VMEM_LIMIT)


def _ln_rows(x, g, b):
    mu = jnp.mean(x, -1, keepdims=True)
    xc = x - mu
    var = jnp.mean(xc * xc, -1, keepdims=True)
    return xc * lax.rsqrt(var + LN_EPS) * g + b


def _sigmoid(x):
    return 1.0 / (1.0 + jnp.exp(-x))


HI_HALF = -65536


def _pack_bf16_pairs(x):
    w = x.shape[1] // 2
    bits = lax.bitcast_convert_type(x.astype(BF16).astype(F32), jnp.int32)
    return lax.shift_right_logical(bits[:, :w], 16) | (bits[:, w:] & HI_HALF)


def _unpack_bf16_pairs(p):
    lo = lax.bitcast_convert_type(lax.shift_left(p, 16), F32)
    hi = lax.bitcast_convert_type(p & HI_HALF, F32)
    return jnp.concatenate([lo, hi], axis=-1)


def _pack_row_halves(x, ref):
    half = x.shape[1] // 2
    ref[0] = _pack_bf16_pairs(x[:, :half])
    ref[1] = _pack_bf16_pairs(x[:, half:])


def _unpack_row_halves(p0, p1):
    return jnp.concatenate([_unpack_bf16_pairs(p0), _unpack_bf16_pairs(p1)], axis=-1)


def _ln_in_kernel(x_ref, g_ref, b_ref, o_ref):
    o_ref[...] = _ln_rows(x_ref[...], g_ref[...], b_ref[...])


def _ln_in(x, g, b, tm):
    n, d = x.shape
    row = pl.BlockSpec((tm, d), lambda i: (i, 0))
    vec = pl.BlockSpec((1, d), lambda i: (0, 0))
    return pl.pallas_call(
        _ln_in_kernel, grid=(n // tm,), in_specs=[row, vec, vec], out_specs=row,
        out_shape=jax.ShapeDtypeStruct((n, d), F32), compiler_params=_cparams(("parallel",)),
    )(x, g.reshape(1, d), b.reshape(1, d))


def _ln2_kernel(x_ref, y10_ref, y11_ref, y20_ref, y21_ref, gate_ref, g_ref, b_ref, o_ref):
    gate = gate_ref[...]
    y = (gate[:, 0:1] * _unpack_row_halves(y10_ref[...], y11_ref[...])
         + gate[:, 1:2] * _unpack_row_halves(y20_ref[...], y21_ref[...]))
    o_ref[...] = _ln_rows(ALPHA * x_ref[...] + y, g_ref[...], b_ref[...])


def _ln2(x, y12, gate, g, b, tm):
    n, d = x.shape
    nt = n // tm
    row = pl.BlockSpec((tm, d), lambda i: (i, 0))
    vec = pl.BlockSpec((1, d), lambda i: (0, 0))
    part = lambda k: pl.BlockSpec((tm, d // 4), lambda i: (i + k * nt, 0))
    return pl.pallas_call(
        _ln2_kernel, grid=(nt,),
        in_specs=[row, part(0), part(1), part(2), part(3),
                  pl.BlockSpec((tm, LANES), lambda i: (i, 0)), vec, vec],
        out_specs=row,
        out_shape=jax.ShapeDtypeStruct((n, d), F32), compiler_params=_cparams(("parallel",)),
    )(x, y12, y12, y12, y12, gate, g.reshape(1, d), b.reshape(1, d))


def _in_proj_kernel(h_ref, w_ref, gq_ref, wq_ref, gkv_ref, ct_ref, st_ref,
                    q_ref, kcat_ref, ckv_ref, kr_ref, pr_ref, pc_ref):
    p = jnp.dot(h_ref[...].astype(BF16), w_ref[...], preferred_element_type=F32)
    ct = ct_ref[...]
    st = st_ref[...]

    def rope(tile):
        return tile * ct + pltpu.roll(tile, LANES - MLA_ROPE_DIM, 1) * st

    cq = p[:, :QPAD]
    ms = jnp.sum(cq * cq, -1, keepdims=True) * (1.0 / MLA_Q_RANK)
    cqn = (cq * lax.rsqrt(ms + RMS_EPS) * gq_ref[...]).astype(BF16)
    q = jnp.dot(cqn, wq_ref[...], preferred_element_type=F32)
    for h in range(MLA_HEADS):
        base = h * QHEAD
        q_ref[h, :, :LANES] = q[:, base:base + LANES].astype(BF16)
        q_ref[h, :, LANES:] = rope(q[:, base + LANES:base + QHEAD]).astype(BF16)

    c = p[:, COL_CKV:COL_CKV + MLA_KV_RANK]
    ckv = c * lax.rsqrt(jnp.mean(c * c, -1, keepdims=True) + RMS_EPS) * gkv_ref[...]
    ckv_ref[...] = ckv
    kr = rope(p[:, COL_KR:COL_KR + LANES])
    kr_ref[...] = kr[:, :MLA_ROPE_DIM]
    kcat_ref[:, :LANES] = ckv.astype(BF16)
    kcat_ref[:, LANES:] = kr.astype(BF16)
    pr_ref[...] = p[:, COL_RWKV:COL_RWKV + RWKV_PROJ]
    pc_ref[...] = p[:, COL_CONV:COL_CONV + 3 * CONV_DIM]


def _in_proj(h, w_in_p, gq_p, w_q, g_kvn, ctab, stab, tm):
    n, d = h.shape
    tab_blocks = ctab.shape[0] // tm
    row = lambda c: pl.BlockSpec((tm, c), lambda i: (i, 0))
    full = lambda a: pl.BlockSpec(a.shape, lambda i: (0,) * a.ndim)
    tab = pl.BlockSpec((tm, LANES), lambda i: (i % tab_blocks, 0))
    gkv = g_kvn.reshape(1, MLA_KV_RANK)
    out_shape = (
        jax.ShapeDtypeStruct((MLA_HEADS, n, QHEAD), BF16),
        jax.ShapeDtypeStruct((n, KCAT), BF16),
        jax.ShapeDtypeStruct((n, MLA_KV_RANK), F32),
        jax.ShapeDtypeStruct((n, MLA_ROPE_DIM), F32),
        jax.ShapeDtypeStruct((n, RWKV_PROJ), F32),
        jax.ShapeDtypeStruct((n, 3 * CONV_DIM), F32),
    )
    out_specs = (
        pl.BlockSpec((MLA_HEADS, tm, QHEAD), lambda i: (0, i, 0)),
        row(KCAT), row(MLA_KV_RANK), row(MLA_ROPE_DIM), row(RWKV_PROJ), row(3 * CONV_DIM),
    )
    return pl.pallas_call(
        _in_proj_kernel, grid=(n // tm,),
        in_specs=[row(d), full(w_in_p), full(gq_p), full(w_q), full(gkv), tab, tab],
        out_specs=out_specs, out_shape=out_shape, compiler_params=_cparams(("parallel",)),
    )(h, w_in_p, gq_p, w_q, gkv, ctab, stab)


def _mla_prompt_kernel(q_ref, k_ref, o_ref):
    i = pl.program_id(1)
    rows = MLA_HEADS * ATT_TQ
    q = q_ref[...].reshape(rows, QHEAD)

    def chunk(j, carry, masked):
        m, l, acc = carry
        k = k_ref[pl.ds(pl.multiple_of(j * ATT_TK, ATT_TK), ATT_TK), :]
        s = lax.dot_general(q, k, (((1,), (1,)), ((), ())), preferred_element_type=F32)
        if masked:
            qpos = i * ATT_TQ + lax.broadcasted_iota(
                jnp.int32, (MLA_HEADS, ATT_TQ, ATT_TK), 1).reshape(rows, ATT_TK)
            kpos = j * ATT_TK + lax.broadcasted_iota(jnp.int32, (rows, ATT_TK), 1)
            s = jnp.where(kpos <= qpos, s, -jnp.inf)
        m_new = jnp.maximum(m, jnp.max(s, -1, keepdims=True))
        a = jnp.exp(m - m_new)
        pr = jnp.exp(s - m_new)
        l = a * l + jnp.sum(pr, -1, keepdims=True)
        acc = a * acc + jnp.dot(pr.astype(BF16), k[:, :MLA_KV_RANK], preferred_element_type=F32)
        return m_new, l, acc

    init = (jnp.full((rows, 1), -jnp.inf, F32), jnp.zeros((rows, 1), F32),
            jnp.zeros((rows, MLA_KV_RANK), F32))
    n_full = (i * ATT_TQ) // ATT_TK
    carry = lax.fori_loop(0, n_full, lambda j, c: chunk(j, c, False), init)
    m, l, acc = chunk(n_full, carry, True)
    o = acc / l
    for h in range(MLA_HEADS):
        o_ref[:, h * MLA_KV_RANK:(h + 1) * MLA_KV_RANK] = o[h * ATT_TQ:(h + 1) * ATT_TQ].astype(BF16)


def _mla_prompt(q, kcat, batch, seq):
    n = batch * seq
    nq = seq // ATT_TQ
    return pl.pallas_call(
        _mla_prompt_kernel, grid=(batch, nq),
        in_specs=[pl.BlockSpec((MLA_HEADS, ATT_TQ, QHEAD), lambda b, i: (0, b * nq + i, 0)),
                  pl.BlockSpec((seq, KCAT), lambda b, i: (b, 0))],
        out_specs=pl.BlockSpec((ATT_TQ, MLA_HEADS * MLA_KV_RANK), lambda b, i: (b * nq + i, 0)),
        out_shape=jax.ShapeDtypeStruct((n, MLA_HEADS * MLA_KV_RANK), BF16),
        compiler_params=_cparams(("parallel", "arbitrary")),
    )(q, kcat)


def _mla_sample_kernel(pt_ref, q_ref, kn_ref, ckv_hbm, krt_hbm, o_ref,
                       ckv_buf, krt_buf, sem, kbuf, rbuf, m_ref, l_ref, acc_ref,
                       *, n_new, layer, n_seq, n_chunks):
    npg = PAGES_PER_STEP
    b = pl.program_id(0)
    c = pl.program_id(1)
    step = b * n_chunks + c
    slot = step % 2

    def page_copies(bb, cc, sl):
        copies = []
        for j in range(npg):
            pid = 0 if bb is None else pt_ref[bb, cc * npg + j]
            copies.append(pltpu.make_async_copy(
                ckv_hbm.at[layer, pid], ckv_buf.at[sl, pl.ds(j * PAGE_SIZE, PAGE_SIZE), :],
                sem.at[sl, 0]))
            copies.append(pltpu.make_async_copy(
                krt_hbm.at[layer, pid], krt_buf.at[sl, j], sem.at[sl, 1]))
        return copies

    @pl.when(step == 0)
    def _():
        for cp in page_copies(b, c, slot):
            cp.start()

    @pl.when(step + 1 < n_seq * n_chunks)
    def _():
        nxt = step + 1
        for cp in page_copies(nxt // n_chunks, nxt % n_chunks, 1 - slot):
            cp.start()

    for cp in page_copies(None, None, slot):
        cp.wait()

    @pl.when(c == 0)
    def _():
        m_ref[...] = jnp.full_like(m_ref, -jnp.inf)
        l_ref[...] = jnp.zeros_like(l_ref)
        acc_ref[...] = jnp.zeros_like(acc_ref)

    q = q_ref[...]
    q_lat = q[:, :MLA_KV_RANK]
    q_rope = q[:, MLA_KV_RANK:MLA_KV_RANK + MLA_ROPE_DIM]
    nt = (((1,), (1,)), ((), ()))

    def update(s, v):
        m = m_ref[...]
        m_new = jnp.maximum(m, jnp.max(s, -1, keepdims=True))
        a = jnp.exp(m - m_new)
        pr = jnp.exp(s - m_new)
        l_ref[...] = a * l_ref[...] + jnp.sum(pr, -1, keepdims=True)
        acc_ref[...] = a * acc_ref[...] + jnp.dot(pr.astype(BF16), v, preferred_element_type=F32)
        m_ref[...] = m_new

    kbuf[...] = ckv_buf[slot].astype(BF16)
    for j in range(npg):
        rbuf[:, j * PAGE_SIZE:(j + 1) * PAGE_SIZE] = krt_buf[slot, j].astype(BF16)
    k = kbuf[...]
    s = (lax.dot_general(q_lat, k, nt, preferred_element_type=F32)
         + jnp.dot(q_rope, rbuf[...], preferred_element_type=F32))
    update(s, k)

    @pl.when(c == pl.num_programs(1) - 1)
    def _():
        kn = kn_ref[...]
        sn = lax.dot_general(q, kn, nt, preferred_element_type=F32)
        t_row = lax.broadcasted_iota(jnp.int32, sn.shape, 0) % n_new
        sn = jnp.where(lax.broadcasted_iota(jnp.int32, sn.shape, 1) <= t_row, sn, -jnp.inf)
        update(sn, kn[:, :MLA_KV_RANK])
        o_ref[...] = (acc_ref[...] / l_ref[...]).astype(BF16)


def _mla_sample(q, k_new, cache_ckv, cache_krope_t, page_table, layer):
    bd, rows, _ = q.shape
    n_new = k_new.shape[1]
    k_new = jnp.pad(k_new, ((0, 0), (0, NEW_PAD - n_new), (0, 0)))
    n_pages = page_table.shape[1]
    npg = PAGES_PER_STEP
    assert n_pages % npg == 0
    n_chunks = n_pages // npg
    hbm = pl.BlockSpec(memory_space=pl.ANY)
    in_specs = [pl.BlockSpec((None, rows, QHEAD), lambda b, c, pt: (b, 0, 0)),
                pl.BlockSpec((None, NEW_PAD, KCAT), lambda b, c, pt: (b, 0, 0)), hbm, hbm]
    grid_spec = pltpu.PrefetchScalarGridSpec(
        num_scalar_prefetch=1, grid=(bd, n_chunks), in_specs=in_specs,
        out_specs=pl.BlockSpec((None, rows, MLA_KV_RANK), lambda b, c, pt: (b, 0, 0)),
        scratch_shapes=[pltpu.VMEM((2, npg * PAGE_SIZE, MLA_KV_RANK), F32),
                        pltpu.VMEM((2, npg, MLA_ROPE_DIM, PAGE_SIZE), F32),
                        pltpu.SemaphoreType.DMA((2, 2)),
                        pltpu.VMEM((npg * PAGE_SIZE, MLA_KV_RANK), BF16),
                        pltpu.VMEM((MLA_ROPE_DIM, npg * PAGE_SIZE), BF16),
                        pltpu.VMEM((rows, 1), F32), pltpu.VMEM((rows, 1), F32),
                        pltpu.VMEM((rows, MLA_KV_RANK), F32)])
    return pl.pallas_call(
        functools.partial(_mla_sample_kernel, n_new=n_new, layer=layer, n_seq=bd,
                          n_chunks=n_chunks),
        grid_spec=grid_spec, out_shape=jax.ShapeDtypeStruct((bd, rows, MLA_KV_RANK), BF16),
        compiler_params=_cparams(("arbitrary", "arbitrary")),
    )(page_table, q, k_new, cache_ckv, cache_krope_t)


def _prev_rows(x, before, shift):
    if shift == 1:
        row = lax.broadcasted_iota(jnp.int32, x.shape, 0)
        return jnp.where(row == 0, before, pltpu.roll(x, 1, 0))
    return jnp.concatenate([before, x[:x.shape[0] - shift]], axis=0)


def _mix_prep_kernel(pr_ref, pc_ref, sh0_ref, cv0_ref, mu_ref, w0_ref, a0_ref, wdec_ref, wiclr_ref,
                     wgate_ref, cw_ref,
                     r_ref, k_ref, v_ref, w_ref, a_ref, g_ref, co_ref, cu_ref,
                     sh_c, cu1_c, cu2_c, *, shift, carry):
    first = pl.program_id(1) == 0

    if carry:
        @pl.when(first)
        def _():
            sh_c[...] = sh0_ref[...]
            cu1_c[...] = cv0_ref[1]
            cu2_c[...] = cv0_ref[0]
        sh_before, cu1_before, cu2_before = sh_c[...], cu1_c[...], cu2_c[...]
    else:
        sh_before, cu1_before, cu2_before = sh0_ref[...], cv0_ref[1], cv0_ref[0]

    p = pr_ref[...]
    xs = p + (_prev_rows(p, sh_before, shift) - p) * mu_ref[...]
    r_ref[...] = xs[:, :RWKV_DIM]
    k_ref[...] = xs[:, RWKV_DIM:2 * RWKV_DIM]
    v_ref[...] = xs[:, 2 * RWKV_DIM:3 * RWKV_DIM]
    lora = xs[:, 3 * RWKV_DIM:3 * RWKV_DIM + LANES]
    z = w0_ref[...] + jnp.dot(jnp.tanh(lora).astype(BF16), wdec_ref[...], preferred_element_type=F32)
    nz = -z
    softplus = jnp.maximum(nz, 0.0) + jnp.log(1.0 + jnp.exp(-jnp.abs(nz)))
    w_ref[...] = jnp.exp(-jnp.exp(-softplus - 0.5))
    a_ref[...] = _sigmoid(a0_ref[...] + jnp.dot(lora.astype(BF16), wiclr_ref[...],
                                                preferred_element_type=F32))
    xg = xs[:, 3 * RWKV_DIM + LANES:]
    g_ref[...] = jnp.dot(_sigmoid(xg).astype(BF16), wgate_ref[...], preferred_element_type=F32)

    pc = pc_ref[...]
    cu = pc[:, CONV_DIM:2 * CONV_DIM] * pc[:, 2 * CONV_DIM:]
    cu1 = _prev_rows(cu, cu1_before, shift)
    cu2 = _prev_rows(cu1, cu2_before, shift)
    cw = cw_ref[...]
    co_ref[...] = (pc[:, :CONV_DIM] * (cw[0:1] * cu2 + cw[1:2] * cu1 + cw[2:3] * cu)).astype(BF16)
    cu_ref[...] = cu

    if carry:
        tm = p.shape[0]
        sh_c[...] = p[tm - 1:tm]
        cu1_c[...] = cu[tm - 1:tm]
        cu2_c[...] = cu1[tm - 1:tm]


def _mix_prep(p_r, p_c, shift0, conv0, mu, w0, a0, wdec_p, wiclr_p, wgate, conv_w, *, n_seq, tm, shift):
    n = p_r.shape[0]
    carry = shift == 1
    tps = n // n_seq // tm if carry else 1
    grid = (n_seq, tps) if carry else (1, 1)
    row = lambda c: pl.BlockSpec((tm, c), lambda b, t: (b * tps + t, 0))
    full = lambda a: pl.BlockSpec(a.shape, lambda b, t: (0,) * a.ndim)
    if carry:
        sh_spec = pl.BlockSpec((None, 1, RWKV_PROJ), lambda b, t: (b, 0, 0))
        cv_spec = pl.BlockSpec((None, 2, 1, CONV_DIM), lambda b, t: (b, 0, 0, 0))
        shift0 = shift0.reshape(n_seq, 1, RWKV_PROJ)
        conv0 = conv0.reshape(n_seq, 2, 1, CONV_DIM)
    else:
        sh_spec = full(shift0)
        conv0 = jnp.swapaxes(conv0, 0, 1)
        cv_spec = full(conv0)
    vec = lambda a: a.reshape(1, -1)
    args = (p_r, p_c, shift0, conv0, vec(mu), vec(w0), vec(a0), wdec_p, wiclr_p, wgate, conv_w)
    in_specs = [row(RWKV_PROJ), row(3 * CONV_DIM), sh_spec, cv_spec] + [full(a) for a in args[4:]]
    f32o = jax.ShapeDtypeStruct((n, RWKV_DIM), F32)
    out_shape = (f32o,) * 6 + (jax.ShapeDtypeStruct((n, CONV_DIM), BF16), f32o)
    rows_before = shift
    return pl.pallas_call(
        functools.partial(_mix_prep_kernel, shift=shift, carry=carry), grid=grid,
        in_specs=in_specs, out_specs=(row(RWKV_DIM),) * 8, out_shape=out_shape,
        scratch_shapes=[pltpu.VMEM((rows_before, RWKV_PROJ), F32),
                        pltpu.VMEM((rows_before, CONV_DIM), F32),
                        pltpu.VMEM((rows_before, CONV_DIM), F32)],
        compiler_params=_cparams(("parallel", "arbitrary")),
    )(*args)


def _wkv_kernel(r_ref, k_ref, v_ref, w_ref, a_ref, kk_ref, ka_ref, rk_ref, gn_ref, bn_ref, s0_ref,
                y_ref, sout_ref, s_ref, al_ref, wr_ref, be_ref, km_ref, wd_ref):
    tb = pl.program_id(1)
    hd = RWKV_HEAD_DIM

    @pl.when(tb == 0)
    def _():
        s_ref[...] = s0_ref[...]

    kk_t, ka_t, rk_t = kk_ref[...], ka_ref[...], rk_ref[...]
    gn_t, bn_t = gn_ref[...], bn_ref[...]

    def step(t, _):
        r, k, v, w, a = r_ref[t], k_ref[t], v_ref[t], w_ref[t], a_ref[t]
        kk = k * kk_t
        kk = kk / jnp.maximum(jnp.sqrt(jnp.sum(kk * kk, 0, keepdims=True)), 1e-12)
        km = k * (1.0 + (a - 1.0) * ka_t)
        be = kk * a
        al_ref[...] = -kk
        wr_ref[...] = w * r
        be_ref[...] = be
        km_ref[...] = km
        wd_ref[...] = w
        sa = jnp.zeros((hd, LANES), F32)
        y0 = jnp.zeros((hd, LANES), F32)
        for i in range(hd):
            s_i = s_ref[i]
            sa = sa + s_i * al_ref[i:i + 1, :]
            y0 = y0 + s_i * wr_ref[i:i + 1, :]
        br = jnp.sum(be * r, 0, keepdims=True)
        kr = jnp.sum(km * r, 0, keepdims=True)
        y = y0 + sa * br + v * kr
        for i in range(hd):
            s_ref[i] = s_ref[i] * wd_ref[i:i + 1, :] + sa * be_ref[i:i + 1, :] + v * km_ref[i:i + 1, :]
        mean = jnp.mean(y, 0, keepdims=True)
        yc = y - mean
        var = jnp.mean(yc * yc, 0, keepdims=True)
        bonus = jnp.sum(r * km * rk_t, 0, keepdims=True) * v
        y_ref[t] = yc * lax.rsqrt(var + GN_EPS) * gn_t + bn_t + bonus
        return 0

    lax.fori_loop(0, r_ref.shape[0], step, 0)

    @pl.when(tb == pl.num_programs(1) - 1)
    def _():
        sout_ref[...] = s_ref[...]


def _wkv(r, k, v, w, a, kk_t, ka_t, rk_t, gn_t, bn_t, s0, tt):
    t_len, hd, lanes = r.shape
    seq = pl.BlockSpec((tt, hd, LANES), lambda g, t: (t, 0, g))
    par = pl.BlockSpec((hd, LANES), lambda g, t: (0, g))
    st = pl.BlockSpec((hd, hd, LANES), lambda g, t: (0, 0, g))
    return pl.pallas_call(
        _wkv_kernel, grid=(lanes // LANES, t_len // tt),
        in_specs=[seq] * 5 + [par] * 5 + [st], out_specs=(seq, st),
        out_shape=(jax.ShapeDtypeStruct((t_len, hd, lanes), F32),
                   jax.ShapeDtypeStruct((hd, hd, lanes), F32)),
        scratch_shapes=[pltpu.VMEM((hd, hd, LANES), F32)] + [pltpu.VMEM((hd, LANES), F32)] * 5,
        compiler_params=_cparams(("parallel", "arbitrary")),
    )(r, k, v, w, a, kk_t, ka_t, rk_t, gn_t, bn_t, s0)


def _mix_out_kernel(h_ref, o_ref, y_ref, g_ref, co_ref, wo_ref, g1_ref, b1_ref, wr_ref, br_ref,
                    x_ref, xp_ref, gate_ref, route_ref, cnt_ref):
    mix = jnp.concatenate(
        [o_ref[...], (y_ref[...] * g_ref[...]).astype(BF16), co_ref[...]], axis=-1)
    pre = ALPHA * h_ref[...] + jnp.dot(mix, wo_ref[...], preferred_element_type=F32)
    x = _ln_rows(pre, g1_ref[...], b1_ref[...])
    x_ref[...] = x
    xb = x.astype(BF16)
    _pack_row_halves(x, xp_ref)

    logits = jnp.dot(xb, wr_ref[...], preferred_element_type=F32) + br_ref[...]
    lane = lax.broadcasted_iota(jnp.int32, logits.shape, 1)
    ninf = -jnp.inf
    lg = jnp.where(lane < N_GROUPS, logits, ninf)
    mg = jnp.max(lg, -1, keepdims=True)
    g_p = 1.0 / jnp.sum(jnp.exp(lg - mg), -1, keepdims=True)
    g_idx = jnp.min(jnp.where(lg == mg, lane, LANES), -1, keepdims=True)
    lo = N_GROUPS + EXPERTS_PER_GROUP * g_idx
    le = jnp.where((lane >= lo) & (lane < lo + EXPERTS_PER_GROUP), logits, ninf)
    m1 = jnp.max(le, -1, keepdims=True)
    i1 = jnp.min(jnp.where(le == m1, lane, LANES), -1, keepdims=True)
    le2 = jnp.where(lane == i1, ninf, le)
    m2 = jnp.max(le2, -1, keepdims=True)
    i2 = jnp.min(jnp.where(le2 == m2, lane, LANES), -1, keepdims=True)
    e2 = jnp.exp(m2 - m1)
    gate1 = g_p / (1.0 + e2)
    gate2 = g_p * e2 / (1.0 + e2)
    gate_ref[...] = jnp.where(lane == 0, gate1, jnp.where(lane == 1, gate2, 0.0))

    pick1, pick2 = lane == i1, lane == i2
    picks = jnp.where(pick1 | pick2, 1.0, 0.0)
    tm = logits.shape[0]
    tri = jnp.where(lax.broadcasted_iota(jnp.int32, (tm, tm), 1)
                    < lax.broadcasted_iota(jnp.int32, (tm, tm), 0), 1.0, 0.0).astype(BF16)
    before = jnp.dot(tri, picks.astype(BF16), preferred_element_type=F32)
    rank1 = jnp.sum(jnp.where(pick1, before, 0.0), -1, keepdims=True).astype(jnp.int32)
    rank2 = jnp.sum(jnp.where(pick2, before, 0.0), -1, keepdims=True).astype(jnp.int32)
    route_ref[...] = jnp.where(lane == 0, i1, jnp.where(lane == 1, i2, jnp.where(
        lane == 2, rank1, jnp.where(lane == 3, rank2, 0))))
    cnt_ref[...] = jnp.sum(picks, 0, keepdims=True).astype(jnp.int32)


def _moe_pos_kernel(route_ref, base_ref, pos_ref):
    route = route_ref[...]
    base = base_ref[...]
    lane = lax.broadcasted_iota(jnp.int32, route.shape, 1)
    pos = []
    for c in range(2):
        start = jnp.sum(jnp.where(lane == route[:, c:c + 1], base, 0), -1, keepdims=True)
        pos.append(start + route[:, 2 + c:3 + c])
    pos_ref[...] = jnp.where(lane == 0, pos[0], jnp.where(lane == 1, pos[1], 0))


def _moe_pos(route, base, tm):
    n = route.shape[0]
    row = pl.BlockSpec((tm, LANES), lambda i: (i, 0))
    return pl.pallas_call(
        _moe_pos_kernel, grid=(n // tm,),
        in_specs=[row, pl.BlockSpec((None, 1, LANES), lambda i: (i, 0, 0))], out_specs=row,
        out_shape=jax.ShapeDtypeStruct((n, LANES), jnp.int32),
        compiler_params=_cparams(("parallel",)),
    )(route, base)


def _mix_out(h, o_lat, y, g, c_out, w_o, ln_g, ln_b, w_router, b_router, tm):
    n, d = h.shape
    row = lambda c: pl.BlockSpec((tm, c), lambda i: (i, 0))
    full = lambda a: pl.BlockSpec(a.shape, lambda i: (0,) * a.ndim)
    vec = lambda a: a.reshape(1, -1)
    args = (h, o_lat, y, g, c_out, w_o, vec(ln_g), vec(ln_b), w_router, b_router)
    in_specs = [row(d), row(o_lat.shape[1]), row(RWKV_DIM), row(RWKV_DIM), row(CONV_DIM)] + [
        full(a) for a in args[5:]]
    out_shape = (jax.ShapeDtypeStruct((n, d), F32), jax.ShapeDtypeStruct((2, n, d // 4), jnp.int32),
                 jax.ShapeDtypeStruct((n, LANES), F32), jax.ShapeDtypeStruct((n, LANES), jnp.int32),
                 jax.ShapeDtypeStruct((n // tm, 1, LANES), jnp.int32))
    return pl.pallas_call(
        _mix_out_kernel, grid=(n // tm,), in_specs=in_specs,
        out_specs=(row(d), pl.BlockSpec((2, tm, d // 4), lambda i: (0, i, 0)), row(LANES), row(LANES),
                   pl.BlockSpec((None, 1, LANES), lambda i: (i, 0, 0))),
        out_shape=out_shape, compiler_params=_cparams(("parallel",)),
    )(*args)


def _moe_kernel(be_ref, nv_ref, x0_ref, x1_ref, wgu_ref, wd_ref, o_ref):
    valid = nv_ref[pl.program_id(0)]

    @pl.when(valid > 0)
    def _():
        keep = lax.broadcasted_iota(jnp.int32, x0_ref.shape, 0) < valid
        x = _unpack_row_halves(jnp.where(keep, x0_ref[...], 0),
                               jnp.where(keep, x1_ref[...], 0)).astype(BF16)
        gu = jnp.dot(x, wgu_ref[...], preferred_element_type=F32)
        hg = gu[:, :EXPERT_FF]
        hid = hg * _sigmoid(hg) * gu[:, EXPERT_FF:]
        _pack_row_halves(jnp.dot(hid.astype(BF16), wd_ref[...], preferred_element_type=F32), o_ref)

    @pl.when(valid == 0)
    def _():
        o_ref[...] = jnp.zeros_like(o_ref)


def _moe_experts(blk_e, blk_valid, xs, cap, w_gu, w_d):
    quarter = xs.shape[1]
    d = 4 * quarter
    nb = cap // MOE_ROWS
    grid_spec = pltpu.PrefetchScalarGridSpec(
        num_scalar_prefetch=2, grid=(nb,),
        in_specs=[pl.BlockSpec((MOE_ROWS, quarter), lambda i, be, nv: (i, 0)),
                  pl.BlockSpec((MOE_ROWS, quarter), lambda i, be, nv: (i + nb, 0)),
                  pl.BlockSpec((None, d, 2 * EXPERT_FF), lambda i, be, nv: (be[i], 0, 0)),
                  pl.BlockSpec((None, EXPERT_FF, d), lambda i, be, nv: (be[i], 0, 0))],
        out_specs=pl.BlockSpec((2, MOE_ROWS, quarter), lambda i, be, nv: (0, i, 0)))
    return pl.pallas_call(
        _moe_kernel, grid_spec=grid_spec,
        out_shape=jax.ShapeDtypeStruct((2, cap, quarter), jnp.int32),
        compiler_params=_cparams(("arbitrary",)),
    )(blk_e, blk_valid, xs, xs, w_gu, w_d)


def _sc_mesh():
    return plsc.VectorSubcoreMesh(core_axis_name="core", subcore_axis_name="subcore")


SC_WINDOW = 128


def _sc_rows(n):
    mesh = _sc_mesh()
    unit = SC_WINDOW * mesh.num_cores * mesh.num_subcores
    return -(-n // unit) * unit


def _scatter_rows2(x, idx1, idx2, n_out):
    n, w = x.shape
    mesh = _sc_mesh()
    win = SC_WINDOW
    assert n == _sc_rows(n)

    @pl.kernel(out_type=jax.ShapeDtypeStruct((n_out, w), x.dtype), mesh=mesh, scratch_types=[])
    def scatter(x_hbm, i1_hbm, i2_hbm, o_hbm):
        def body(x_vmem, i1_vmem, i2_vmem):
            pltpu.sync_copy(x_vmem, o_hbm.at[i1_vmem.at[0]])
            pltpu.sync_copy(x_vmem, o_hbm.at[i2_vmem.at[0]])

        idx_spec = pl.BlockSpec((1, win), lambda i: (0, i))
        pltpu.emit_pipeline(
            body, grid=(n // win,),
            in_specs=[pl.BlockSpec((win, w), lambda i: (i, 0)), idx_spec, idx_spec],
            out_specs=[], core_axis_name=('core', 'subcore'),
            dimension_semantics=(pltpu.PARALLEL,),
        )(x_hbm, i1_hbm, i2_hbm)

    return scatter(x, idx1.reshape(1, n), idx2.reshape(1, n))


def _gather_rows(x, idx):
    n = idx.shape[0]
    w = x.shape[1]
    mesh = _sc_mesh()
    win = SC_WINDOW
    assert n == _sc_rows(n)

    @pl.kernel(out_type=jax.ShapeDtypeStruct((n, w), x.dtype), mesh=mesh, scratch_types=[])
    def gather(x_hbm, i_hbm, o_hbm):
        def body(i_vmem, o_vmem):
            pltpu.sync_copy(x_hbm.at[i_vmem.at[0]], o_vmem)

        pltpu.emit_pipeline(
            body, grid=(n // win,),
            in_specs=[pl.BlockSpec((1, win), lambda i: (0, i))],
            out_specs=[pl.BlockSpec((win, w), lambda i: (i, 0))],
            core_axis_name=('core', 'subcore'), dimension_semantics=(pltpu.PARALLEL,),
        )(i_hbm, o_hbm)

    return gather(x, idx.reshape(1, n))


def _moe_layout(tile_counts, n_tok):
    cnt = tile_counts[:, 0, N_GROUPS:N_GROUPS + N_EXPERTS]
    tile_off = jnp.cumsum(cnt, axis=0) - cnt
    total = jnp.sum(cnt, axis=0)
    pcounts = (total + MOE_ROWS - 1) // MOE_ROWS * MOE_ROWS
    pends = jnp.cumsum(pcounts)
    base = (pends - pcounts)[None, :] + tile_off
    base = jnp.pad(base, ((0, 0), (N_GROUPS, LANES - N_GROUPS - N_EXPERTS)))[:, None, :]
    n_blocks = 2 * n_tok // MOE_ROWS + N_EXPERTS
    blk_start = jnp.arange(n_blocks, dtype=jnp.int32) * MOE_ROWS
    blk_e = jnp.minimum(jnp.sum((pends[None, :] <= blk_start[:, None]).astype(jnp.int32), axis=1),
                        N_EXPERTS - 1)
    blk_valid = jnp.clip((pends - pcounts + total)[blk_e] - blk_start, 0, MOE_ROWS)
    return (base.astype(jnp.int32), blk_e.astype(jnp.int32), blk_valid.astype(jnp.int32),
            n_blocks * MOE_ROWS)


def _rope_tables(pos):
    half = MLA_ROPE_DIM // 2
    freqs = ROPE_THETA ** (-jnp.arange(half, dtype=F32) / half)
    ang = pos.astype(F32)[:, None] * freqs
    cos, sin = jnp.cos(ang), jnp.sin(ang)
    pad = jnp.zeros((pos.shape[0], LANES - MLA_ROPE_DIM), F32)
    return (jnp.concatenate([cos, cos, pad], -1), jnp.concatenate([-sin, sin, pad], -1))


def _swap_halves(w):
    half = MLA_ROPE_DIM // 2
    return jnp.concatenate([w[..., half:], w[..., :half]], -1)


def _layer_params(l, w_in, g_qn, w_uq, g_kvn, w_uk, w_uv, mu_shift, w0, w_decay, a0, w_iclr,
                  w_gate_out, k_k, k_a, r_k, lnx_g, lnx_b, conv_w, w_out, ln1_g, ln1_b,
                  w_group_router, b_group_router, w_expert_router, b_expert_router,
                  w_exp_gate, w_exp_up, w_exp_down, ln2_g, ln2_b):
    d = w_in.shape[1]
    hi = lax.Precision.HIGHEST
    wi = w_in[l]
    zc = lambda c: jnp.zeros((d, c), F32)
    c0 = MLA_Q_RANK
    c1 = c0 + MLA_KV_RANK
    c2 = c1 + MLA_ROPE_DIM
    w_kr = wi[:, c1:c2]
    w_in_p = jnp.concatenate([
        wi[:, :c0], zc(QPAD - MLA_Q_RANK), wi[:, c0:c1],
        w_kr, _swap_halves(w_kr), zc(LANES - 2 * MLA_ROPE_DIM), wi[:, c2:]], -1).astype(BF16)
    gq_p = jnp.concatenate([g_qn[l], jnp.zeros((QPAD - MLA_Q_RANK,), F32)]).reshape(1, QPAD)
    uq = w_uq[l]
    q_lat = jnp.einsum('rhn,chn->rhc', uq[..., :MLA_NOPE_DIM], w_uk[l], precision=hi)
    q_rope = uq[..., MLA_NOPE_DIM:]
    wq = jnp.concatenate([q_lat, q_rope, _swap_halves(q_rope),
                          jnp.zeros((MLA_Q_RANK, MLA_HEADS, QHEAD - LANES - 2 * MLA_ROPE_DIM), F32)], -1)
    wq = (wq * ATTN_SCALE).reshape(MLA_Q_RANK, MLA_HEADS * QHEAD)
    wq = jnp.concatenate([wq, jnp.zeros((QPAD - MLA_Q_RANK, MLA_HEADS * QHEAD), F32)], 0).astype(BF16)
    wo = w_out[l]
    mla_dim = MLA_HEADS * MLA_V_DIM
    wo_att = jnp.einsum('chv,hvd->hcd', w_uv[l], wo[:mla_dim].reshape(MLA_HEADS, MLA_V_DIM, -1),
                        precision=hi).reshape(MLA_HEADS * MLA_KV_RANK, -1)
    w_o = jnp.concatenate([wo_att, wo[mla_dim:]], 0).astype(BF16)
    z64 = jnp.zeros((64, RWKV_DIM), F32)
    wdec_p = jnp.concatenate([w_decay[l], z64], 0).astype(BF16)
    wiclr_p = jnp.concatenate([z64, w_iclr[l]], 0).astype(BF16)
    w_router = jnp.concatenate(
        [w_group_router[l], w_expert_router[l],
         jnp.zeros((d, LANES - N_GROUPS - N_EXPERTS), F32)], -1).astype(BF16)
    b_router = jnp.concatenate(
        [b_group_router[l], b_expert_router[l],
         jnp.zeros((LANES - N_GROUPS - N_EXPERTS,), F32)]).reshape(1, LANES)
    w_gu = jnp.concatenate([w_exp_gate[l], w_exp_up[l]], -1).astype(BF16)
    return dict(
        w_in_p=w_in_p, gq_p=gq_p, wq=wq, g_kvn=g_kvn[l], mu=mu_shift[l], w0=w0[l], a0=a0[l],
        wdec_p=wdec_p, wiclr_p=wiclr_p, wgate=w_gate_out[l].astype(BF16), conv_w=conv_w[l],
        k_k=k_k[l], k_a=k_a[l], r_k=r_k[l].reshape(-1), lnx_g=lnx_g[l], lnx_b=lnx_b[l],
        w_o=w_o, ln1_g=ln1_g[l], ln1_b=ln1_b[l], w_router=w_router, b_router=b_router,
        w_gu=w_gu, w_d=w_exp_down[l].astype(BF16), ln2_g=ln2_g[l], ln2_b=ln2_b[l])


def _head_tile(p, n_seq):
    t = p.reshape(RWKV_HEADS, RWKV_HEAD_DIM).T
    return jnp.tile(t, (1, n_seq))


def _layer(h, lp, *, n_seq, t_len, time_major, ctab, stab, attend, shift0, wkv0, conv0):
    n, d = h.shape
    tm = min(512, n)
    q, kcat, ckv, kr, p_r, p_c = _in_proj(h, lp['w_in_p'], lp['gq_p'], lp['wq'], lp['g_kvn'],
                                          ctab, stab, tm)
    o_lat = attend(q, kcat)

    r, k, v, w, a, g, c_out, cu = _mix_prep(
        p_r, p_c, shift0, conv0, lp['mu'], lp['w0'], lp['a0'], lp['wdec_p'], lp['wiclr_p'],
        lp['wgate'], lp['conv_w'], n_seq=n_seq, tm=tm, shift=n_seq if time_major else 1)

    lanes = n_seq * RWKV_HEADS

    def to_scan(x):
        if time_major:
            x = x.reshape(t_len, n_seq, RWKV_HEADS, RWKV_HEAD_DIM).transpose(0, 3, 1, 2)
        else:
            x = x.reshape(n_seq, t_len, RWKV_HEADS, RWKV_HEAD_DIM).transpose(1, 3, 0, 2)
        return x.reshape(t_len, RWKV_HEAD_DIM, lanes)

    tiles = [_head_tile(lp[name], n_seq) for name in ('k_k', 'k_a', 'r_k', 'lnx_g', 'lnx_b')]
    s0 = wkv0.transpose(3, 2, 0, 1).reshape(RWKV_HEAD_DIM, RWKV_HEAD_DIM, lanes)
    y, s_fin = _wkv(*[to_scan(x) for x in (r, k, v, w, a)], *tiles, s0, min(16, t_len))
    y = y.reshape(t_len, RWKV_HEAD_DIM, n_seq, RWKV_HEADS)
    y = (y.transpose(0, 2, 3, 1) if time_major else y.transpose(2, 0, 3, 1)).reshape(n, RWKV_DIM)
    wkv_new = s_fin.reshape(RWKV_HEAD_DIM, RWKV_HEAD_DIM, n_seq, RWKV_HEADS).transpose(2, 3, 1, 0)

    x, xp, gate, route, tile_counts = _mix_out(
        h, o_lat, y, g, c_out, lp['w_o'], lp['ln1_g'], lp['ln1_b'], lp['w_router'], lp['b_router'], tm)
    base, blk_e, blk_valid, cap = _moe_layout(tile_counts, n)
    pos = _moe_pos(route, base, tm)
    pos1, pos2 = pos[:, 0], pos[:, 1]
    n_src = _sc_rows(2 * n)
    spare = 2 * cap + jnp.arange(n_src - 2 * n, dtype=jnp.int32)
    xp = jnp.pad(xp.reshape(2 * n, d // 4), ((0, n_src - 2 * n), (0, 0)))
    xs = _scatter_rows2(xp, jnp.concatenate([pos1, pos1 + cap, spare]),
                        jnp.concatenate([pos2, pos2 + cap, spare]), 2 * cap + n_src - 2 * n)
    yb = _moe_experts(blk_e, blk_valid, xs, cap, lp['w_gu'], lp['w_d'])
    n_dst = _sc_rows(4 * n)
    y12 = _gather_rows(yb.reshape(2 * cap, d // 4), jnp.concatenate(
        [pos1, pos1 + cap, pos2, pos2 + cap, jnp.zeros((n_dst - 4 * n,), jnp.int32)]))
    h_next = _ln2(x, y12, gate, lp['ln2_g'], lp['ln2_b'], tm)
    return h_next, ckv, kr, p_r, cu, wkv_new


def kernel(x_prompt, x_sample, cache_ckv, cache_krope, state_wkv, state_shift, state_conv, page_table, ln_in_g, ln_in_b, w_in, g_qn, w_uq, g_kvn, w_uk, w_uv, mu_shift, w0, w_decay, a0, w_iclr, w_gate_out, k_k, k_a, r_k, lnx_g, lnx_b, conv_w, w_out, ln1_g, ln1_b, w_group_router, b_group_router, w_expert_router, b_expert_router, w_exp_gate, w_exp_up, w_exp_down, ln2_g, ln2_b):
    bp, sp, d = x_prompt.shape
    bd, td, _ = x_sample.shape
    past_len = page_table.shape[1] * PAGE_SIZE
    depth = w_in.shape[0]
    np_, ns = bp * sp, bd * td

    ctab_p, stab_p = _rope_tables(jnp.arange(sp))
    ctab_s, stab_s = _rope_tables(jnp.repeat(past_len + jnp.arange(td), bd))

    hp = _ln_in(x_prompt.reshape(np_, d), ln_in_g, ln_in_b, 512)
    hs = _ln_in(jnp.swapaxes(x_sample, 0, 1).reshape(ns, d), ln_in_g, ln_in_b, ns)

    krope_t = jnp.swapaxes(cache_krope, 2, 3)
    zero_shift = jnp.zeros((bp, RWKV_PROJ), F32)
    zero_wkv = jnp.zeros((bp, RWKV_HEADS, RWKV_HEAD_DIM, RWKV_HEAD_DIM), F32)
    zero_conv = jnp.zeros((bp, 2, CONV_DIM), F32)

    outs_p = [[] for _ in range(5)]
    outs_s = [[] for _ in range(5)]
    for l in range(depth):
        lp = _layer_params(l, w_in, g_qn, w_uq, g_kvn, w_uk, w_uv, mu_shift, w0, w_decay, a0,
                           w_iclr, w_gate_out, k_k, k_a, r_k, lnx_g, lnx_b, conv_w, w_out, ln1_g,
                           ln1_b, w_group_router, b_group_router, w_expert_router, b_expert_router,
                           w_exp_gate, w_exp_up, w_exp_down, ln2_g, ln2_b)

        def attend_s(q, kcat, l=l):
            qs = q.reshape(MLA_HEADS, td, bd, QHEAD).transpose(2, 0, 1, 3).reshape(
                bd, MLA_HEADS * td, QHEAD)
            kn = kcat.reshape(td, bd, KCAT).transpose(1, 0, 2)
            o = _mla_sample(qs, kn, cache_ckv, krope_t, page_table, l)
            return o.reshape(bd, MLA_HEADS, td, MLA_KV_RANK).transpose(2, 0, 1, 3).reshape(
                ns, MLA_HEADS * MLA_KV_RANK)

        hs, ckv, kr, p_r, cu, wkv = _layer(
            hs, lp, n_seq=bd, t_len=td, time_major=True, ctab=ctab_s, stab=stab_s,
            attend=attend_s, shift0=state_shift[l], wkv0=state_wkv[l], conv0=state_conv[l])
        tmaj = lambda x: jnp.swapaxes(x.reshape(td, bd, -1), 0, 1)
        outs_s[0].append(tmaj(ckv))
        outs_s[1].append(tmaj(kr))
        outs_s[2].append(wkv)
        outs_s[3].append(p_r.reshape(td, bd, RWKV_PROJ)[-1])
        outs_s[4].append(tmaj(cu)[:, -2:])

        hp, ckv, kr, p_r, cu, wkv = _layer(
            hp, lp, n_seq=bp, t_len=sp, time_major=False, ctab=ctab_p, stab=stab_p,
            attend=lambda q, kcat: _mla_prompt(q, kcat, bp, sp),
            shift0=zero_shift, wkv0=zero_wkv, conv0=zero_conv)
        outs_p[0].append(ckv.reshape(-1, PAGE_SIZE, MLA_KV_RANK))
        outs_p[1].append(kr.reshape(-1, PAGE_SIZE, MLA_ROPE_DIM))
        outs_p[2].append(wkv)
        outs_p[3].append(p_r.reshape(bp, sp, RWKV_PROJ)[:, -1])
        outs_p[4].append(cu.reshape(bp, sp, CONV_DIM)[:, -2:])

    y_p = hp.reshape(bp, sp, d)
    y_s = jnp.swapaxes(hs.reshape(td, bd, d), 0, 1)
    return (y_p, y_s) + tuple(jnp.stack(o) for o in outs_p) + tuple(jnp.stack(o) for o in outs_s)
```

```python
import functools

import numpy as np
import jax
import jax.numpy as jnp
from jax import lax
from jax.experimental import pallas as pl
from jax.experimental.pallas import tpu as pltpu
from jax.experimental.pallas import tpu_sc as plsc

F32 = jnp.float32
BF16 = jnp.bfloat16

MLA_HEADS = 8
MLA_NOPE_DIM = 64
MLA_ROPE_DIM = 32
MLA_Q_RANK = 192
MLA_KV_RANK = 128
MLA_V_DIM = 64
ROPE_THETA = 10000.0
ATTN_SCALE = (MLA_NOPE_DIM + MLA_ROPE_DIM) ** -0.5
RWKV_DIM = 256
RWKV_HEADS = 4
RWKV_HEAD_DIM = 64
RWKV_PROJ = 3 * RWKV_DIM + 64 + 64 + 128
CONV_DIM = 256
GN_EPS = 64e-5
N_GROUPS = 4
EXPERTS_PER_GROUP = 8
N_EXPERTS = N_GROUPS * EXPERTS_PER_GROUP
EXPERT_FF = 256
DEPTH = 2
ALPHA = (2 * DEPTH) ** 0.25
LN_EPS = 1e-5
RMS_EPS = 1e-6
PAGE_SIZE = 128

QPAD = 256
QHEAD = 256
KCAT = 256
COL_CKV = QPAD
COL_KR = COL_CKV + MLA_KV_RANK
COL_RWKV = COL_KR + 128
COL_CONV = COL_RWKV + RWKV_PROJ
IN_PAD = COL_CONV + 3 * CONV_DIM

LANES = 128
ATT_TQ = 128
ATT_TK = 512
PAGES_PER_STEP = 32
NEW_PAD = 16
MOE_ROWS = 256
VMEM_LIMIT = 56 * 1024 * 1024


def _cparams(sem):
    return pltpu.CompilerParams(dimension_semantics=sem, vmem_limit_bytes=VMEM_LIMIT)


def _ln_rows(x, g, b):
    mu = jnp.mean(x, -1, keepdims=True)
    xc = x - mu
    var = jnp.mean(xc * xc, -1, keepdims=True)
    return xc * lax.rsqrt(var + LN_EPS) * g + b


def _sigmoid(x):
    return 1.0 / (1.0 + jnp.exp(-x))


HI_HALF = -65536


def _pack_bf16_pairs(x):
    w = x.shape[1] // 2
    bits = lax.bitcast_convert_type(x.astype(BF16).astype(F32), jnp.int32)
    return lax.shift_right_logical(bits[:, :w], 16) | (bits[:, w:] & HI_HALF)


def _unpack_bf16_pairs(p):
    lo = lax.bitcast_convert_type(lax.shift_left(p, 16), F32)
    hi = lax.bitcast_convert_type(p & HI_HALF, F32)
    return jnp.concatenate([lo, hi], axis=-1)


def _pack_row_halves(x, ref):
    half = x.shape[1] // 2
    ref[0] = _pack_bf16_pairs(x[:, :half])
    ref[1] = _pack_bf16_pairs(x[:, half:])


def _unpack_row_halves(p0, p1):
    return jnp.concatenate([_unpack_bf16_pairs(p0), _unpack_bf16_pairs(p1)], axis=-1)


def _ln_in_kernel(x_ref, g_ref, b_ref, o_ref):
    o_ref[...] = _ln_rows(x_ref[...], g_ref[...], b_ref[...])


def _ln_in(x, g, b, tm):
    n, d = x.shape
    row = pl.BlockSpec((tm, d), lambda i: (i, 0))
    vec = pl.BlockSpec((1, d), lambda i: (0, 0))
    return pl.pallas_call(
        _ln_in_kernel, grid=(n // tm,), in_specs=[row, vec, vec], out_specs=row,
        out_shape=jax.ShapeDtypeStruct((n, d), F32), compiler_params=_cparams(("parallel",)),
    )(x, g.reshape(1, d), b.reshape(1, d))


def _ln2_kernel(x_ref, y10_ref, y11_ref, y20_ref, y21_ref, gate_ref, g_ref, b_ref, o_ref):
    gate = gate_ref[...]
    y = (gate[:, 0:1] * _unpack_row_halves(y10_ref[...], y11_ref[...])
         + gate[:, 1:2] * _unpack_row_halves(y20_ref[...], y21_ref[...]))
    o_ref[...] = _ln_rows(ALPHA * x_ref[...] + y, g_ref[...], b_ref[...])


def _ln2(x, y12, gate, g, b, tm):
    n, d = x.shape
    nt = n // tm
    row = pl.BlockSpec((tm, d), lambda i: (i, 0))
    vec = pl.BlockSpec((1, d), lambda i: (0, 0))
    part = lambda k: pl.BlockSpec((tm, d // 4), lambda i: (i + k * nt, 0))
    return pl.pallas_call(
        _ln2_kernel, grid=(nt,),
        in_specs=[row, part(0), part(1), part(2), part(3),
                  pl.BlockSpec((tm, LANES), lambda i: (i, 0)), vec, vec],
        out_specs=row,
        out_shape=jax.ShapeDtypeStruct((n, d), F32), compiler_params=_cparams(("parallel",)),
    )(x, y12, y12, y12, y12, gate, g.reshape(1, d), b.reshape(1, d))


def _in_proj_kernel(h_ref, w_ref, gq_ref, wq_ref, gkv_ref, ct_ref, st_ref,
                    q_ref, kcat_ref, ckv_ref, kr_ref, pr_ref, pc_ref):
    p = jnp.dot(h_ref[...].astype(BF16), w_ref[...], preferred_element_type=F32)
    ct = ct_ref[...]
    st = st_ref[...]

    def rope(tile):
        return tile * ct + pltpu.roll(tile, LANES - MLA_ROPE_DIM, 1) * st

    cq = p[:, :QPAD]
    ms = jnp.sum(cq * cq, -1, keepdims=True) * (1.0 / MLA_Q_RANK)
    cqn = (cq * lax.rsqrt(ms + RMS_EPS) * gq_ref[...]).astype(BF16)
    q = jnp.dot(cqn, wq_ref[...], preferred_element_type=F32)
    for h in range(MLA_HEADS):
        base = h * QHEAD
        q_ref[h, :, :LANES] = q[:, base:base + LANES].astype(BF16)
        q_ref[h, :, LANES:] = rope(q[:, base + LANES:base + QHEAD]).astype(BF16)

    c = p[:, COL_CKV:COL_CKV + MLA_KV_RANK]
    ckv = c * lax.rsqrt(jnp.mean(c * c, -1, keepdims=True) + RMS_EPS) * gkv_ref[...]
    ckv_ref[...] = ckv
    kr = rope(p[:, COL_KR:COL_KR + LANES])
    kr_ref[...] = kr[:, :MLA_ROPE_DIM]
    kcat_ref[:, :LANES] = ckv.astype(BF16)
    kcat_ref[:, LANES:] = kr.astype(BF16)
    pr_ref[...] = p[:, COL_RWKV:COL_RWKV + RWKV_PROJ]
    pc_ref[...] = p[:, COL_CONV:COL_CONV + 3 * CONV_DIM]


def _in_proj(h, w_in_p, gq_p, w_q, g_kvn, ctab, stab, tm):
    n, d = h.shape
    tab_blocks = ctab.shape[0] // tm
    row = lambda c: pl.BlockSpec((tm, c), lambda i: (i, 0))
    full = lambda a: pl.BlockSpec(a.shape, lambda i: (0,) * a.ndim)
    tab = pl.BlockSpec((tm, LANES), lambda i: (i % tab_blocks, 0))
    gkv = g_kvn.reshape(1, MLA_KV_RANK)
    out_shape = (
        jax.ShapeDtypeStruct((MLA_HEADS, n, QHEAD), BF16),
        jax.ShapeDtypeStruct((n, KCAT), BF16),
        jax.ShapeDtypeStruct((n, MLA_KV_RANK), F32),
        jax.ShapeDtypeStruct((n, MLA_ROPE_DIM), F32),
        jax.ShapeDtypeStruct((n, RWKV_PROJ), F32),
        jax.ShapeDtypeStruct((n, 3 * CONV_DIM), F32),
    )
    out_specs = (
        pl.BlockSpec((MLA_HEADS, tm, QHEAD), lambda i: (0, i, 0)),
        row(KCAT), row(MLA_KV_RANK), row(MLA_ROPE_DIM), row(RWKV_PROJ), row(3 * CONV_DIM),
    )
    return pl.pallas_call(
        _in_proj_kernel, grid=(n // tm,),
        in_specs=[row(d), full(w_in_p), full(gq_p), full(w_q), full(gkv), tab, tab],
        out_specs=out_specs, out_shape=out_shape, compiler_params=_cparams(("parallel",)),
    )(h, w_in_p, gq_p, w_q, gkv, ctab, stab)


def _mla_prompt_kernel(q_ref, k_ref, o_ref):
    i = pl.program_id(1)
    rows = MLA_HEADS * ATT_TQ
    q = q_ref[...].reshape(rows, QHEAD)

    def chunk(j, carry, masked):
        m, l, acc = carry
        k = k_ref[pl.ds(pl.multiple_of(j * ATT_TK, ATT_TK), ATT_TK), :]
        s = lax.dot_general(q, k, (((1,), (1,)), ((), ())), preferred_element_type=F32)
        if masked:
            qpos = i * ATT_TQ + lax.broadcasted_iota(
                jnp.int32, (MLA_HEADS, ATT_TQ, ATT_TK), 1).reshape(rows, ATT_TK)
            kpos = j * ATT_TK + lax.broadcasted_iota(jnp.int32, (rows, ATT_TK), 1)
            s = jnp.where(kpos <= qpos, s, -jnp.inf)
        m_new = jnp.maximum(m, jnp.max(s, -1, keepdims=True))
        a = jnp.exp(m - m_new)
        pr = jnp.exp(s - m_new)
        l = a * l + jnp.sum(pr, -1, keepdims=True)
        acc = a * acc + jnp.dot(pr.astype(BF16), k[:, :MLA_KV_RANK], preferred_element_type=F32)
        return m_new, l, acc

    init = (jnp.full((rows, 1), -jnp.inf, F32), jnp.zeros((rows, 1), F32),
            jnp.zeros((rows, MLA_KV_RANK), F32))
    n_full = (i * ATT_TQ) // ATT_TK
    carry = lax.fori_loop(0, n_full, lambda j, c: chunk(j, c, False), init)
    m, l, acc = chunk(n_full, carry, True)
    o = acc / l
    for h in range(MLA_HEADS):
        o_ref[:, h * MLA_KV_RANK:(h + 1) * MLA_KV_RANK] = o[h * ATT_TQ:(h + 1) * ATT_TQ].astype(BF16)


def _mla_prompt(q, kcat, batch, seq):
    n = batch * seq
    nq = seq // ATT_TQ
    return pl.pallas_call(
        _mla_prompt_kernel, grid=(batch, nq),
        in_specs=[pl.BlockSpec((MLA_HEADS, ATT_TQ, QHEAD), lambda b, i: (0, b * nq + i, 0)),
                  pl.BlockSpec((seq, KCAT), lambda b, i: (b, 0))],
        out_specs=pl.BlockSpec((ATT_TQ, MLA_HEADS * MLA_KV_RANK), lambda b, i: (b * nq + i, 0)),
        out_shape=jax.ShapeDtypeStruct((n, MLA_HEADS * MLA_KV_RANK), BF16),
        compiler_params=_cparams(("parallel", "arbitrary")),
    )(q, kcat)


def _mla_sample_kernel(pt_ref, q_ref, kn_ref, ckv_hbm, krt_hbm, o_ref,
                       ckv_buf, krt_buf, sem, kbuf, rbuf, m_ref, l_ref, acc_ref,
                       *, n_new, layer, n_seq, n_chunks):
    npg = PAGES_PER_STEP
    b = pl.program_id(0)
    c = pl.program_id(1)
    step = b * n_chunks + c
    slot = step % 2

    def page_copies(bb, cc, sl):
        copies = []
        for j in range(npg):
            pid = 0 if bb is None else pt_ref[bb, cc * npg + j]
            copies.append(pltpu.make_async_copy(
                ckv_hbm.at[layer, pid], ckv_buf.at[sl, pl.ds(j * PAGE_SIZE, PAGE_SIZE), :],
                sem.at[sl, 0]))
            copies.append(pltpu.make_async_copy(
                krt_hbm.at[layer, pid], krt_buf.at[sl, j], sem.at[sl, 1]))
        return copies

    @pl.when(step == 0)
    def _():
        for cp in page_copies(b, c, slot):
            cp.start()

    @pl.when(step + 1 < n_seq * n_chunks)
    def _():
        nxt = step + 1
        for cp in page_copies(nxt // n_chunks, nxt % n_chunks, 1 - slot):
            cp.start()

    for cp in page_copies(None, None, slot):
        cp.wait()

    @pl.when(c == 0)
    def _():
        m_ref[...] = jnp.full_like(m_ref, -jnp.inf)
        l_ref[...] = jnp.zeros_like(l_ref)
        acc_ref[...] = jnp.zeros_like(acc_ref)

    q = q_ref[...]
    q_lat = q[:, :MLA_KV_RANK]
    q_rope = q[:, MLA_KV_RANK:MLA_KV_RANK + MLA_ROPE_DIM]
    nt = (((1,), (1,)), ((), ()))

    def update(s, v):
        m = m_ref[...]
        m_new = jnp.maximum(m, jnp.max(s, -1, keepdims=True))
        a = jnp.exp(m - m_new)
        pr = jnp.exp(s - m_new)
        l_ref[...] = a * l_ref[...] + jnp.sum(pr, -1, keepdims=True)
        acc_ref[...] = a * acc_ref[...] + jnp.dot(pr.astype(BF16), v, preferred_element_type=F32)
        m_ref[...] = m_new

    kbuf[...] = ckv_buf[slot].astype(BF16)
    for j in range(npg):
        rbuf[:, j * PAGE_SIZE:(j + 1) * PAGE_SIZE] = krt_buf[slot, j].astype(BF16)
    k = kbuf[...]
    s = (lax.dot_general(q_lat, k, nt, preferred_element_type=F32)
         + jnp.dot(q_rope, rbuf[...], preferred_element_type=F32))
    update(s, k)

    @pl.when(c == pl.num_programs(1) - 1)
    def _():
        kn = kn_ref[...]
        sn = lax.dot_general(q, kn, nt, preferred_element_type=F32)
        t_row = lax.broadcasted_iota(jnp.int32, sn.shape, 0) % n_new
        sn = jnp.where(lax.broadcasted_iota(jnp.int32, sn.shape, 1) <= t_row, sn, -jnp.inf)
        update(sn, kn[:, :MLA_KV_RANK])
        o_ref[...] = (acc_ref[...] / l_ref[...]).astype(BF16)


def _mla_sample(q, k_new, cache_ckv, cache_krope_t, page_table, layer):
    bd, rows, _ = q.shape
    n_new = k_new.shape[1]
    k_new = jnp.pad(k_new, ((0, 0), (0, NEW_PAD - n_new), (0, 0)))
    n_pages = page_table.shape[1]
    npg = PAGES_PER_STEP
    assert n_pages % npg == 0
    n_chunks = n_pages // npg
    hbm = pl.BlockSpec(memory_space=pl.ANY)
    in_specs = [pl.BlockSpec((None, rows, QHEAD), lambda b, c, pt: (b, 0, 0)),
                pl.BlockSpec((None, NEW_PAD, KCAT), lambda b, c, pt: (b, 0, 0)), hbm, hbm]
    grid_spec = pltpu.PrefetchScalarGridSpec(
        num_scalar_prefetch=1, grid=(bd, n_chunks), in_specs=in_specs,
        out_specs=pl.BlockSpec((None, rows, MLA_KV_RANK), lambda b, c, pt: (b, 0, 0)),
        scratch_shapes=[pltpu.VMEM((2, npg * PAGE_SIZE, MLA_KV_RANK), F32),
                        pltpu.VMEM((2, npg, MLA_ROPE_DIM, PAGE_SIZE), F32),
                        pltpu.SemaphoreType.DMA((2, 2)),
                        pltpu.VMEM((npg * PAGE_SIZE, MLA_KV_RANK), BF16),
                        pltpu.VMEM((MLA_ROPE_DIM, npg * PAGE_SIZE), BF16),
                        pltpu.VMEM((rows, 1), F32), pltpu.VMEM((rows, 1), F32),
                        pltpu.VMEM((rows, MLA_KV_RANK), F32)])
    return pl.pallas_call(
        functools.partial(_mla_sample_kernel, n_new=n_new, layer=layer, n_seq=bd,
                          n_chunks=n_chunks),
        grid_spec=grid_spec, out_shape=jax.ShapeDtypeStruct((bd, rows, MLA_KV_RANK), BF16),
        compiler_params=_cparams(("arbitrary", "arbitrary")),
    )(page_table, q, k_new, cache_ckv, cache_krope_t)


def _prev_rows(x, before, shift):
    if shift == 1:
        row = lax.broadcasted_iota(jnp.int32, x.shape, 0)
        return jnp.where(row == 0, before, pltpu.roll(x, 1, 0))
    return jnp.concatenate([before, x[:x.shape[0] - shift]], axis=0)


def _mix_prep_kernel(pr_ref, pc_ref, sh0_ref, cv0_ref, mu_ref, w0_ref, a0_ref, wdec_ref, wiclr_ref,
                     wgate_ref, cw_ref,
                     r_ref, k_ref, v_ref, w_ref, a_ref, g_ref, co_ref, cu_ref,
                     sh_c, cu1_c, cu2_c, *, shift, carry):
    first = pl.program_id(1) == 0

    if carry:
        @pl.when(first)
        def _():
            sh_c[...] = sh0_ref[...]
            cu1_c[...] = cv0_ref[1]
            cu2_c[...] = cv0_ref[0]
        sh_before, cu1_before, cu2_before = sh_c[...], cu1_c[...], cu2_c[...]
    else:
        sh_before, cu1_before, cu2_before = sh0_ref[...], cv0_ref[1], cv0_ref[0]

    p = pr_ref[...]
    xs = p + (_prev_rows(p, sh_before, shift) - p) * mu_ref[...]
    r_ref[...] = xs[:, :RWKV_DIM]
    k_ref[...] = xs[:, RWKV_DIM:2 * RWKV_DIM]
    v_ref[...] = xs[:, 2 * RWKV_DIM:3 * RWKV_DIM]
    lora = xs[:, 3 * RWKV_DIM:3 * RWKV_DIM + LANES]
    z = w0_ref[...] + jnp.dot(jnp.tanh(lora).astype(BF16), wdec_ref[...], preferred_element_type=F32)
    nz = -z
    softplus = jnp.maximum(nz, 0.0) + jnp.log(1.0 + jnp.exp(-jnp.abs(nz)))
    w_ref[...] = jnp.exp(-jnp.exp(-softplus - 0.5))
    a_ref[...] = _sigmoid(a0_ref[...] + jnp.dot(lora.astype(BF16), wiclr_ref[...],
                                                preferred_element_type=F32))
    xg = xs[:, 3 * RWKV_DIM + LANES:]
    g_ref[...] = jnp.dot(_sigmoid(xg).astype(BF16), wgate_ref[...], preferred_element_type=F32)

    pc = pc_ref[...]
    cu = pc[:, CONV_DIM:2 * CONV_DIM] * pc[:, 2 * CONV_DIM:]
    cu1 = _prev_rows(cu, cu1_before, shift)
    cu2 = _prev_rows(cu1, cu2_before, shift)
    cw = cw_ref[...]
    co_ref[...] = (pc[:, :CONV_DIM] * (cw[0:1] * cu2 + cw[1:2] * cu1 + cw[2:3] * cu)).astype(BF16)
    cu_ref[...] = cu

    if carry:
        tm = p.shape[0]
        sh_c[...] = p[tm - 1:tm]
        cu1_c[...] = cu[tm - 1:tm]
        cu2_c[...] = cu1[tm - 1:tm]


def _mix_prep(p_r, p_c, shift0, conv0, mu, w0, a0, wdec_p, wiclr_p, wgate, conv_w, *, n_seq, tm, shift):
    n = p_r.shape[0]
    carry = shift == 1
    tps = n // n_seq // tm if carry else 1
    grid = (n_seq, tps) if carry else (1, 1)
    row = lambda c: pl.BlockSpec((tm, c), lambda b, t: (b * tps + t, 0))
    full = lambda a: pl.BlockSpec(a.shape, lambda b, t: (0,) * a.ndim)
    if carry:
        sh_spec = pl.BlockSpec((None, 1, RWKV_PROJ), lambda b, t: (b, 0, 0))
        cv_spec = pl.BlockSpec((None, 2, 1, CONV_DIM), lambda b, t: (b, 0, 0, 0))
        shift0 = shift0.reshape(n_seq, 1, RWKV_PROJ)
        conv0 = conv0.reshape(n_seq, 2, 1, CONV_DIM)
    else:
        sh_spec = full(shift0)
        conv0 = jnp.swapaxes(conv0, 0, 1)
        cv_spec = full(conv0)
    vec = lambda a: a.reshape(1, -1)
    args = (p_r, p_c, shift0, conv0, vec(mu), vec(w0), vec(a0), wdec_p, wiclr_p, wgate, conv_w)
    in_specs = [row(RWKV_PROJ), row(3 * CONV_DIM), sh_spec, cv_spec] + [full(a) for a in args[4:]]
    f32o = jax.ShapeDtypeStruct((n, RWKV_DIM), F32)
    out_shape = (f32o,) * 6 + (jax.ShapeDtypeStruct((n, CONV_DIM), BF16), f32o)
    rows_before = shift
    return pl.pallas_call(
        functools.partial(_mix_prep_kernel, shift=shift, carry=carry), grid=grid,
        in_specs=in_specs, out_specs=(row(RWKV_DIM),) * 8, out_shape=out_shape,
        scratch_shapes=[pltpu.VMEM((rows_before, RWKV_PROJ), F32),
                        pltpu.VMEM((rows_before, CONV_DIM), F32),
                        pltpu.VMEM((rows_before, CONV_DIM), F32)],
        compiler_params=_cparams(("parallel", "arbitrary")),
    )(*args)


def _wkv_kernel(r_ref, k_ref, v_ref, w_ref, a_ref, kk_ref, ka_ref, rk_ref, gn_ref, bn_ref, s0_ref,
                y_ref, sout_ref, s_ref, al_ref, wr_ref, be_ref, km_ref, wd_ref):
    tb = pl.program_id(1)
    hd = RWKV_HEAD_DIM

    @pl.when(tb == 0)
    def _():
        s_ref[...] = s0_ref[...]

    kk_t, ka_t, rk_t = kk_ref[...], ka_ref[...], rk_ref[...]
    gn_t, bn_t = gn_ref[...], bn_ref[...]

    def step(t, _):
        r, k, v, w, a = r_ref[t], k_ref[t], v_ref[t], w_ref[t], a_ref[t]
        kk = k * kk_t
        kk = kk / jnp.maximum(jnp.sqrt(jnp.sum(kk * kk, 0, keepdims=True)), 1e-12)
        km = k * (1.0 + (a - 1.0) * ka_t)
        be = kk * a
        al_ref[...] = -kk
        wr_ref[...] = w * r
        be_ref[...] = be
        km_ref[...] = km
        wd_ref[...] = w
        sa = jnp.zeros((hd, LANES), F32)
        y0 = jnp.zeros((hd, LANES), F32)
        for i in range(hd):
            s_i = s_ref[i]
            sa = sa + s_i * al_ref[i:i + 1, :]
            y0 = y0 + s_i * wr_ref[i:i + 1, :]
        br = jnp.sum(be * r, 0, keepdims=True)
        kr = jnp.sum(km * r, 0, keepdims=True)
        y = y0 + sa * br + v * kr
        for i in range(hd):
            s_ref[i] = s_ref[i] * wd_ref[i:i + 1, :] + sa * be_ref[i:i + 1, :] + v * km_ref[i:i + 1, :]
        mean = jnp.mean(y, 0, keepdims=True)
        yc = y - mean
        var = jnp.mean(yc * yc, 0, keepdims=True)
        bonus = jnp.sum(r * km * rk_t, 0, keepdims=True) * v
        y_ref[t] = yc * lax.rsqrt(var + GN_EPS) * gn_t + bn_t + bonus
        return 0

    lax.fori_loop(0, r_ref.shape[0], step, 0)

    @pl.when(tb == pl.num_programs(1) - 1)
    def _():
        sout_ref[...] = s_ref[...]


def _wkv(r, k, v, w, a, kk_t, ka_t, rk_t, gn_t, bn_t, s0, tt):
    t_len, hd, lanes = r.shape
    seq = pl.BlockSpec((tt, hd, LANES), lambda g, t: (t, 0, g))
    par = pl.BlockSpec((hd, LANES), lambda g, t: (0, g))
    st = pl.BlockSpec((hd, hd, LANES), lambda g, t: (0, 0, g))
    return pl.pallas_call(
        _wkv_kernel, grid=(lanes // LANES, t_len // tt),
        in_specs=[seq] * 5 + [par] * 5 + [st], out_specs=(seq, st),
        out_shape=(jax.ShapeDtypeStruct((t_len, hd, lanes), F32),
                   jax.ShapeDtypeStruct((hd, hd, lanes), F32)),
        scratch_shapes=[pltpu.VMEM((hd, hd, LANES), F32)] + [pltpu.VMEM((hd, LANES), F32)] * 5,
        compiler_params=_cparams(("parallel", "arbitrary")),
    )(r, k, v, w, a, kk_t, ka_t, rk_t, gn_t, bn_t, s0)


WKV_CHUNK = 64
WKV_BLOCK = 256


def _bf16_parts(x, n):
    parts = []
    for _ in range(n):
        p = x.astype(BF16)
        parts.append(p)
        x = x - p.astype(F32)
    return parts


def _dot(a, b):
    return jnp.dot(a.astype(BF16), b.astype(BF16), preferred_element_type=F32)


def _dot_nt(a, b):
    return lax.dot_general(a.astype(BF16), b.astype(BF16), (((1,), (1,)), ((), ())),
                           preferred_element_type=F32)


def _wkv_chunk_kernel(r_ref, k_ref, v_ref, w_ref, a_ref, par_ref, s0_ref, y_ref, sout_ref, st_ref):
    c_len = WKV_CHUNK
    width = RWKV_DIM
    shift = RWKV_HEAD_DIM.bit_length() - 1

    @pl.when(pl.program_id(1) == 0)
    def _():
        st_ref[...] = s0_ref[...]

    row = lax.broadcasted_iota(jnp.int32, (width, width), 0)
    col = lax.broadcasted_iota(jnp.int32, (width, width), 1)
    same_head = (row >> shift) == (col >> shift)
    eye = row == col
    head_ones = jnp.where(same_head, 1.0, 0.0).astype(BF16)
    t_idx = lax.broadcasted_iota(jnp.int32, (c_len, width), 0)
    s_idx = lax.broadcasted_iota(jnp.int32, (c_len, width), 1) & (c_len - 1)
    strict, incl = s_idx < t_idx, s_idx <= t_idx

    def head_sum(x):
        return jnp.dot(x.astype(BF16), head_ones, preferred_element_type=F32)

    def blockdiag(z):
        return jnp.where(same_head, jnp.concatenate([z] * RWKV_HEADS, axis=0), 0.0)

    par = par_ref[...]
    k_k, k_a, r_k, gn, bn = (par[i:i + 1] for i in range(5))
    r_all, k_all, v_all, w_all, a_all = r_ref[...], k_ref[...], v_ref[...], w_ref[...], a_ref[...]
    kk_all = k_all * k_k
    kk_all = kk_all / jnp.maximum(jnp.sqrt(head_sum(kk_all * kk_all)), 1e-12)
    km_all = k_all * (1.0 + (a_all - 1.0) * k_a)
    logw_all = jnp.log(w_all)
    bonus_all = head_sum(r_all * km_all * r_k) * v_all

    n_chunks = r_all.shape[0] // c_len
    assert n_chunks == RWKV_HEADS
    tri = jnp.where(same_head & (col <= row), 1.0, 0.0).astype(BF16)
    logp = sum(jnp.dot(tri, p, preferred_element_type=F32) for p in _bf16_parts(logw_all, 3))
    p_inc, p_inv, p_prev = jnp.exp(logp), jnp.exp(-logp), jnp.exp(logp - logw_all)
    at_all, rt_all = -kk_all * p_prev, r_all * p_inc
    bt_all, kt_all = kk_all * a_all * p_inv, km_all * p_inv

    chunks = range(n_chunks)
    rows = [slice(c * c_len, (c + 1) * c_len) for c in chunks]
    p_end = [p_inc[(c + 1) * c_len - 1:(c + 1) * c_len] for c in chunks]
    at, rt, bt, kt = ([x[sl] for sl in rows] for x in (at_all, rt_all, bt_all, kt_all))
    bt_bd = [blockdiag(x) for x in bt]
    kt_bd = [blockdiag(x) for x in kt]
    v_bd = [blockdiag(v_all[sl]) for sl in rows]
    at_bd = [blockdiag(x) for x in at]
    a_ab = [jnp.where(strict, _dot_nt(at[c], bt_bd[c]), 0.0) for c in chunks]
    a_ak = [jnp.where(strict, _dot_nt(at[c], kt_bd[c]), 0.0) for c in chunks]
    a_rb = [jnp.where(incl, _dot_nt(rt[c], bt_bd[c]), 0.0) for c in chunks]
    a_rk = [jnp.where(incl, _dot_nt(rt[c], kt_bd[c]), 0.0) for c in chunks]
    lp = [blockdiag(x) for x in a_ab]
    ident = jnp.where(eye, 1.0, 0.0)
    tinv = [ident + x for x in lp]
    for _ in range(c_len.bit_length() - 2):
        lp = [_dot(x, x) for x in lp]
        tinv = [tinv[c] + _dot(tinv[c], lp[c]) for c in chunks]
    g_bd = [_dot(blockdiag(a_ak[c]), v_bd[c]) for c in chunks]
    w_bd = [_dot(tinv[c], at_bd[c]) for c in chunks]
    u0_bd = [_dot(tinv[c], g_bd[c]) for c in chunks]
    q = [rt[c] + _dot(a_rb[c], w_bd[c]) for c in chunks]
    y0 = [_dot(a_rb[c], u0_bd[c]) + _dot(a_rk[c], v_bd[c]) for c in chunks]
    bs_t = [blockdiag(bt[c] * p_end[c]).T for c in chunks]
    ks_t = [blockdiag(kt[c] * p_end[c]).T for c in chunks]
    m = [jnp.where(eye, p_end[c], 0.0) + _dot(bs_t[c], w_bd[c]) for c in chunks]
    n0 = [_dot(bs_t[c], u0_bd[c]) + _dot(ks_t[c], v_bd[c]) for c in chunks]

    st = st_ref[...]
    ys = []
    for c in chunks:
        ys.append(_dot(q[c], st) + y0[c])
        st = _dot(m[c], st) + n0[c]
    st_ref[...] = st

    y = jnp.concatenate(ys, axis=0)
    mean = head_sum(y) * (1.0 / RWKV_HEAD_DIM)
    yc = y - mean
    var = head_sum(yc * yc) * (1.0 / RWKV_HEAD_DIM)
    y_ref[...] = yc * lax.rsqrt(var + GN_EPS) * gn + bn + bonus_all

    @pl.when(pl.program_id(1) == pl.num_programs(1) - 1)
    def _():
        sout_ref[...] = st_ref[...]


def _wkv_chunked(r, k, v, w, a, par, s0_bd, n_seq):
    n, width = r.shape
    nblk = n // n_seq // WKV_BLOCK
    seq = pl.BlockSpec((WKV_BLOCK, width), lambda b, j: (b * nblk + j, 0))
    st = pl.BlockSpec((None, width, width), lambda b, j: (b, 0, 0))
    return pl.pallas_call(
        _wkv_chunk_kernel, grid=(n_seq, nblk),
        in_specs=[seq] * 5 + [pl.BlockSpec(par.shape, lambda b, j: (0, 0)), st],
        out_specs=(seq, st),
        out_shape=(jax.ShapeDtypeStruct((n, width), F32),
                   jax.ShapeDtypeStruct((n_seq, width, width), F32)),
        scratch_shapes=[pltpu.VMEM((width, width), F32)],
        compiler_params=_cparams(("parallel", "arbitrary")),
    )(r, k, v, w, a, par, s0_bd)


def _mix_out_kernel(h_ref, o_ref, y_ref, g_ref, co_ref, wo_ref, g1_ref, b1_ref, wr_ref, br_ref,
                    x_ref, xp_ref, gate_ref, route_ref, cnt_ref):
    mix = jnp.concatenate(
        [o_ref[...], (y_ref[...] * g_ref[...]).astype(BF16), co_ref[...]], axis=-1)
    pre = ALPHA * h_ref[...] + jnp.dot(mix, wo_ref[...], preferred_element_type=F32)
    x = _ln_rows(pre, g1_ref[...], b1_ref[...])
    x_ref[...] = x
    xb = x.astype(BF16)
    _pack_row_halves(x, xp_ref)

    logits = jnp.dot(xb, wr_ref[...], preferred_element_type=F32) + br_ref[...]
    lane = lax.broadcasted_iota(jnp.int32, logits.shape, 1)
    ninf = -jnp.inf
    lg = jnp.where(lane < N_GROUPS, logits, ninf)
    mg = jnp.max(lg, -1, keepdims=True)
    g_p = 1.0 / jnp.sum(jnp.exp(lg - mg), -1, keepdims=True)
    g_idx = jnp.min(jnp.where(lg == mg, lane, LANES), -1, keepdims=True)
    lo = N_GROUPS + EXPERTS_PER_GROUP * g_idx
    le = jnp.where((lane >= lo) & (lane < lo + EXPERTS_PER_GROUP), logits, ninf)
    m1 = jnp.max(le, -1, keepdims=True)
    i1 = jnp.min(jnp.where(le == m1, lane, LANES), -1, keepdims=True)
    le2 = jnp.where(lane == i1, ninf, le)
    m2 = jnp.max(le2, -1, keepdims=True)
    i2 = jnp.min(jnp.where(le2 == m2, lane, LANES), -1, keepdims=True)
    e2 = jnp.exp(m2 - m1)
    gate1 = g_p / (1.0 + e2)
    gate2 = g_p * e2 / (1.0 + e2)
    gate_ref[...] = jnp.where(lane == 0, gate1, jnp.where(lane == 1, gate2, 0.0))

    pick1, pick2 = lane == i1, lane == i2
    picks = jnp.where(pick1 | pick2, 1.0, 0.0)
    tm = logits.shape[0]
    tri = jnp.where(lax.broadcasted_iota(jnp.int32, (tm, tm), 1)
                    < lax.broadcasted_iota(jnp.int32, (tm, tm), 0), 1.0, 0.0).astype(BF16)
    before = jnp.dot(tri, picks.astype(BF16), preferred_element_type=F32)
    rank1 = jnp.sum(jnp.where(pick1, before, 0.0), -1, keepdims=True).astype(jnp.int32)
    rank2 = jnp.sum(jnp.where(pick2, before, 0.0), -1, keepdims=True).astype(jnp.int32)
    route_ref[...] = jnp.where(lane == 0, i1, jnp.where(lane == 1, i2, jnp.where(
        lane == 2, rank1, jnp.where(lane == 3, rank2, 0))))
    cnt_ref[...] = jnp.sum(picks, 0, keepdims=True).astype(jnp.int32)


def _moe_pos_kernel(route_ref, base_ref, pos_ref):
    route = route_ref[...]
    base = base_ref[...]
    lane = lax.broadcasted_iota(jnp.int32, route.shape, 1)
    pos = []
    for c in range(2):
        start = jnp.sum(jnp.where(lane == route[:, c:c + 1], base, 0), -1, keepdims=True)
        pos.append(start + route[:, 2 + c:3 + c])
    pos_ref[...] = jnp.where(lane == 0, pos[0], jnp.where(lane == 1, pos[1], 0))


def _moe_pos(route, base, tm):
    n = route.shape[0]
    row = pl.BlockSpec((tm, LANES), lambda i: (i, 0))
    return pl.pallas_call(
        _moe_pos_kernel, grid=(n // tm,),
        in_specs=[row, pl.BlockSpec((None, 1, LANES), lambda i: (i, 0, 0))], out_specs=row,
        out_shape=jax.ShapeDtypeStruct((n, LANES), jnp.int32),
        compiler_params=_cparams(("parallel",)),
    )(route, base)


def _mix_out(h, o_lat, y, g, c_out, w_o, ln_g, ln_b, w_router, b_router, tm):
    n, d = h.shape
    row = lambda c: pl.BlockSpec((tm, c), lambda i: (i, 0))
    full = lambda a: pl.BlockSpec(a.shape, lambda i: (0,) * a.ndim)
    vec = lambda a: a.reshape(1, -1)
    args = (h, o_lat, y, g, c_out, w_o, vec(ln_g), vec(ln_b), w_router, b_router)
    in_specs = [row(d), row(o_lat.shape[1]), row(RWKV_DIM), row(RWKV_DIM), row(CONV_DIM)] + [
        full(a) for a in args[5:]]
    out_shape = (jax.ShapeDtypeStruct((n, d), F32), jax.ShapeDtypeStruct((2, n, d // 4), jnp.int32),
                 jax.ShapeDtypeStruct((n, LANES), F32), jax.ShapeDtypeStruct((n, LANES), jnp.int32),
                 jax.ShapeDtypeStruct((n // tm, 1, LANES), jnp.int32))
    return pl.pallas_call(
        _mix_out_kernel, grid=(n // tm,), in_specs=in_specs,
        out_specs=(row(d), pl.BlockSpec((2, tm, d // 4), lambda i: (0, i, 0)), row(LANES), row(LANES),
                   pl.BlockSpec((None, 1, LANES), lambda i: (i, 0, 0))),
        out_shape=out_shape, compiler_params=_cparams(("parallel",)),
    )(*args)


def _moe_kernel(be_ref, nv_ref, x0_ref, x1_ref, wgu_ref, wd_ref, o_ref):
    valid = nv_ref[pl.program_id(0)]

    @pl.when(valid > 0)
    def _():
        keep = lax.broadcasted_iota(jnp.int32, x0_ref.shape, 0) < valid
        x = _unpack_row_halves(jnp.where(keep, x0_ref[...], 0),
                               jnp.where(keep, x1_ref[...], 0)).astype(BF16)
        gu = jnp.dot(x, wgu_ref[...], preferred_element_type=F32)
        hg = gu[:, :EXPERT_FF]
        hid = hg * _sigmoid(hg) * gu[:, EXPERT_FF:]
        _pack_row_halves(jnp.dot(hid.astype(BF16), wd_ref[...], preferred_element_type=F32), o_ref)

    @pl.when(valid == 0)
    def _():
        o_ref[...] = jnp.zeros_like(o_ref)


def _moe_experts(blk_e, blk_valid, xs, cap, w_gu, w_d):
    quarter = xs.shape[1]
    d = 4 * quarter
    nb = cap // MOE_ROWS
    grid_spec = pltpu.PrefetchScalarGridSpec(
        num_scalar_prefetch=2, grid=(nb,),
        in_specs=[pl.BlockSpec((MOE_ROWS, quarter), lambda i, be, nv: (i, 0)),
                  pl.BlockSpec((MOE_ROWS, quarter), lambda i, be, nv: (i + nb, 0)),
                  pl.BlockSpec((None, d, 2 * EXPERT_FF), lambda i, be, nv: (be[i], 0, 0)),
                  pl.BlockSpec((None, EXPERT_FF, d), lambda i, be, nv: (be[i], 0, 0))],
        out_specs=pl.BlockSpec((2, MOE_ROWS, quarter), lambda i, be, nv: (0, i, 0)))
    return pl.pallas_call(
        _moe_kernel, grid_spec=grid_spec,
        out_shape=jax.ShapeDtypeStruct((2, cap, quarter), jnp.int32),
        compiler_params=_cparams(("arbitrary",)),
    )(blk_e, blk_valid, xs, xs, w_gu, w_d)


def _sc_mesh():
    return plsc.VectorSubcoreMesh(core_axis_name="core", subcore_axis_name="subcore")


SC_WINDOW = 128


def _sc_rows(n):
    mesh = _sc_mesh()
    unit = SC_WINDOW * mesh.num_cores * mesh.num_subcores
    return -(-n // unit) * unit


def _scatter_rows2(x, idx1, idx2, n_out):
    n, w = x.shape
    mesh = _sc_mesh()
    win = SC_WINDOW
    assert n == _sc_rows(n)

    @pl.kernel(out_type=jax.ShapeDtypeStruct((n_out, w), x.dtype), mesh=mesh, scratch_types=[])
    def scatter(x_hbm, i1_hbm, i2_hbm, o_hbm):
        def body(x_vmem, i1_vmem, i2_vmem):
            pltpu.sync_copy(x_vmem, o_hbm.at[i1_vmem.at[0]])
            pltpu.sync_copy(x_vmem, o_hbm.at[i2_vmem.at[0]])

        idx_spec = pl.BlockSpec((1, win), lambda i: (0, i))
        pltpu.emit_pipeline(
            body, grid=(n // win,),
            in_specs=[pl.BlockSpec((win, w), lambda i: (i, 0)), idx_spec, idx_spec],
            out_specs=[], core_axis_name=('core', 'subcore'),
            dimension_semantics=(pltpu.PARALLEL,),
        )(x_hbm, i1_hbm, i2_hbm)

    return scatter(x, idx1.reshape(1, n), idx2.reshape(1, n))


def _gather_rows(x, idx):
    n = idx.shape[0]
    w = x.shape[1]
    mesh = _sc_mesh()
    win = SC_WINDOW
    assert n == _sc_rows(n)

    @pl.kernel(out_type=jax.ShapeDtypeStruct((n, w), x.dtype), mesh=mesh, scratch_types=[])
    def gather(x_hbm, i_hbm, o_hbm):
        def body(i_vmem, o_vmem):
            pltpu.sync_copy(x_hbm.at[i_vmem.at[0]], o_vmem)

        pltpu.emit_pipeline(
            body, grid=(n // win,),
            in_specs=[pl.BlockSpec((1, win), lambda i: (0, i))],
            out_specs=[pl.BlockSpec((win, w), lambda i: (i, 0))],
            core_axis_name=('core', 'subcore'), dimension_semantics=(pltpu.PARALLEL,),
        )(i_hbm, o_hbm)

    return gather(x, idx.reshape(1, n))


def _moe_layout(tile_counts, n_tok):
    cnt = tile_counts[:, 0, N_GROUPS:N_GROUPS + N_EXPERTS]
    tile_off = jnp.cumsum(cnt, axis=0) - cnt
    total = jnp.sum(cnt, axis=0)
    pcounts = (total + MOE_ROWS - 1) // MOE_ROWS * MOE_ROWS
    pends = jnp.cumsum(pcounts)
    base = (pends - pcounts)[None, :] + tile_off
    base = jnp.pad(base, ((0, 0), (N_GROUPS, LANES - N_GROUPS - N_EXPERTS)))[:, None, :]
    n_blocks = 2 * n_tok // MOE_ROWS + N_EXPERTS
    blk_start = jnp.arange(n_blocks, dtype=jnp.int32) * MOE_ROWS
    blk_e = jnp.minimum(jnp.sum((pends[None, :] <= blk_start[:, None]).astype(jnp.int32), axis=1),
                        N_EXPERTS - 1)
    blk_valid = jnp.clip((pends - pcounts + total)[blk_e] - blk_start, 0, MOE_ROWS)
    return (base.astype(jnp.int32), blk_e.astype(jnp.int32), blk_valid.astype(jnp.int32),
            n_blocks * MOE_ROWS)


def _rope_tables(pos):
    half = MLA_ROPE_DIM // 2
    freqs = ROPE_THETA ** (-jnp.arange(half, dtype=F32) / half)
    ang = pos.astype(F32)[:, None] * freqs
    cos, sin = jnp.cos(ang), jnp.sin(ang)
    pad = jnp.zeros((pos.shape[0], LANES - MLA_ROPE_DIM), F32)
    return (jnp.concatenate([cos, cos, pad], -1), jnp.concatenate([-sin, sin, pad], -1))


def _swap_halves(w):
    half = MLA_ROPE_DIM // 2
    return jnp.concatenate([w[..., half:], w[..., :half]], -1)


def _layer_params(l, w_in, g_qn, w_uq, g_kvn, w_uk, w_uv, mu_shift, w0, w_decay, a0, w_iclr,
                  w_gate_out, k_k, k_a, r_k, lnx_g, lnx_b, conv_w, w_out, ln1_g, ln1_b,
                  w_group_router, b_group_router, w_expert_router, b_expert_router,
                  w_exp_gate, w_exp_up, w_exp_down, ln2_g, ln2_b):
    d = w_in.shape[1]
    hi = lax.Precision.HIGHEST
    wi = w_in[l]
    zc = lambda c: jnp.zeros((d, c), F32)
    c0 = MLA_Q_RANK
    c1 = c0 + MLA_KV_RANK
    c2 = c1 + MLA_ROPE_DIM
    w_kr = wi[:, c1:c2]
    w_in_p = jnp.concatenate([
        wi[:, :c0], zc(QPAD - MLA_Q_RANK), wi[:, c0:c1],
        w_kr, _swap_halves(w_kr), zc(LANES - 2 * MLA_ROPE_DIM), wi[:, c2:]], -1).astype(BF16)
    gq_p = jnp.concatenate([g_qn[l], jnp.zeros((QPAD - MLA_Q_RANK,), F32)]).reshape(1, QPAD)
    uq = w_uq[l]
    q_lat = jnp.einsum('rhn,chn->rhc', uq[..., :MLA_NOPE_DIM], w_uk[l], precision=hi)
    q_rope = uq[..., MLA_NOPE_DIM:]
    wq = jnp.concatenate([q_lat, q_rope, _swap_halves(q_rope),
                          jnp.zeros((MLA_Q_RANK, MLA_HEADS, QHEAD - LANES - 2 * MLA_ROPE_DIM), F32)], -1)
    wq = (wq * ATTN_SCALE).reshape(MLA_Q_RANK, MLA_HEADS * QHEAD)
    wq = jnp.concatenate([wq, jnp.zeros((QPAD - MLA_Q_RANK, MLA_HEADS * QHEAD), F32)], 0).astype(BF16)
    wo = w_out[l]
    mla_dim = MLA_HEADS * MLA_V_DIM
    wo_att = jnp.einsum('chv,hvd->hcd', w_uv[l], wo[:mla_dim].reshape(MLA_HEADS, MLA_V_DIM, -1),
                        precision=hi).reshape(MLA_HEADS * MLA_KV_RANK, -1)
    w_o = jnp.concatenate([wo_att, wo[mla_dim:]], 0).astype(BF16)
    z64 = jnp.zeros((64, RWKV_DIM), F32)
    wdec_p = jnp.concatenate([w_decay[l], z64], 0).astype(BF16)
    wiclr_p = jnp.concatenate([z64, w_iclr[l]], 0).astype(BF16)
    w_router = jnp.concatenate(
        [w_group_router[l], w_expert_router[l],
         jnp.zeros((d, LANES - N_GROUPS - N_EXPERTS), F32)], -1).astype(BF16)
    b_router = jnp.concatenate(
        [b_group_router[l], b_expert_router[l],
         jnp.zeros((LANES - N_GROUPS - N_EXPERTS,), F32)]).reshape(1, LANES)
    w_gu = jnp.concatenate([w_exp_gate[l], w_exp_up[l]], -1).astype(BF16)
    return dict(
        w_in_p=w_in_p, gq_p=gq_p, wq=wq, g_kvn=g_kvn[l], mu=mu_shift[l], w0=w0[l], a0=a0[l],
        wdec_p=wdec_p, wiclr_p=wiclr_p, wgate=w_gate_out[l].astype(BF16), conv_w=conv_w[l],
        k_k=k_k[l], k_a=k_a[l], r_k=r_k[l].reshape(-1), lnx_g=lnx_g[l], lnx_b=lnx_b[l],
        w_o=w_o, ln1_g=ln1_g[l], ln1_b=ln1_b[l], w_router=w_router, b_router=b_router,
        w_gu=w_gu, w_d=w_exp_down[l].astype(BF16), ln2_g=ln2_g[l], ln2_b=ln2_b[l])


def _head_tile(p, n_seq):
    t = p.reshape(RWKV_HEADS, RWKV_HEAD_DIM).T
    return jnp.tile(t, (1, n_seq))


def _layer(h, lp, *, n_seq, t_len, time_major, ctab, stab, attend, shift0, wkv0, conv0):
    n, d = h.shape
    tm = min(512, n)
    q, kcat, ckv, kr, p_r, p_c = _in_proj(h, lp['w_in_p'], lp['gq_p'], lp['wq'], lp['g_kvn'],
                                          ctab, stab, tm)
    o_lat = attend(q, kcat)

    r, k, v, w, a, g, c_out, cu = _mix_prep(
        p_r, p_c, shift0, conv0, lp['mu'], lp['w0'], lp['a0'], lp['wdec_p'], lp['wiclr_p'],
        lp['wgate'], lp['conv_w'], n_seq=n_seq, tm=tm, shift=n_seq if time_major else 1)

    names = ('k_k', 'k_a', 'r_k', 'lnx_g', 'lnx_b')
    if not time_major and t_len % WKV_BLOCK == 0:
        par = jnp.stack([lp[nm] for nm in names] + [jnp.zeros_like(lp['k_k'])] * 3)
        head_eye = jnp.eye(RWKV_HEADS, dtype=F32)
        s0_bd = jnp.einsum('bhvk,hg->bhkgv', wkv0, head_eye).reshape(n_seq, RWKV_DIM, RWKV_DIM)
        y, s_fin = _wkv_chunked(r, k, v, w, a, par, s0_bd, n_seq)
        wkv_new = jnp.einsum(
            'bhkgv,hg->bhvk',
            s_fin.reshape(n_seq, RWKV_HEADS, RWKV_HEAD_DIM, RWKV_HEADS, RWKV_HEAD_DIM), head_eye)
    else:
        lanes = n_seq * RWKV_HEADS

        def to_scan(x):
            if time_major:
                x = x.reshape(t_len, n_seq, RWKV_HEADS, RWKV_HEAD_DIM).transpose(0, 3, 1, 2)
            else:
                x = x.reshape(n_seq, t_len, RWKV_HEADS, RWKV_HEAD_DIM).transpose(1, 3, 0, 2)
            return x.reshape(t_len, RWKV_HEAD_DIM, lanes)

        tiles = [_head_tile(lp[nm], n_seq) for nm in names]
        s0 = wkv0.transpose(3, 2, 0, 1).reshape(RWKV_HEAD_DIM, RWKV_HEAD_DIM, lanes)
        y, s_fin = _wkv(*[to_scan(x) for x in (r, k, v, w, a)], *tiles, s0, min(16, t_len))
        y = y.reshape(t_len, RWKV_HEAD_DIM, n_seq, RWKV_HEADS)
        y = (y.transpose(0, 2, 3, 1) if time_major else y.transpose(2, 0, 3, 1)).reshape(n, RWKV_DIM)
        wkv_new = s_fin.reshape(RWKV_HEAD_DIM, RWKV_HEAD_DIM, n_seq, RWKV_HEADS).transpose(2, 3, 1, 0)

    x, xp, gate, route, tile_counts = _mix_out(
        h, o_lat, y, g, c_out, lp['w_o'], lp['ln1_g'], lp['ln1_b'], lp['w_router'], lp['b_router'], tm)
    base, blk_e, blk_valid, cap = _moe_layout(tile_counts, n)
    pos = _moe_pos(route, base, tm)
    pos1, pos2 = pos[:, 0], pos[:, 1]
    n_src = _sc_rows(2 * n)
    spare = 2 * cap + jnp.arange(n_src - 2 * n, dtype=jnp.int32)
    xp = jnp.pad(xp.reshape(2 * n, d // 4), ((0, n_src - 2 * n), (0, 0)))
    xs = _scatter_rows2(xp, jnp.concatenate([pos1, pos1 + cap, spare]),
                        jnp.concatenate([pos2, pos2 + cap, spare]), 2 * cap + n_src - 2 * n)
    yb = _moe_experts(blk_e, blk_valid, xs, cap, lp['w_gu'], lp['w_d'])
    n_dst = _sc_rows(4 * n)
    y12 = _gather_rows(yb.reshape(2 * cap, d // 4), jnp.concatenate(
        [pos1, pos1 + cap, pos2, pos2 + cap, jnp.zeros((n_dst - 4 * n,), jnp.int32)]))
    h_next = _ln2(x, y12, gate, lp['ln2_g'], lp['ln2_b'], tm)
    return h_next, ckv, kr, p_r, cu, wkv_new


def kernel(x_prompt, x_sample, cache_ckv, cache_krope, state_wkv, state_shift, state_conv, page_table, ln_in_g, ln_in_b, w_in, g_qn, w_uq, g_kvn, w_uk, w_uv, mu_shift, w0, w_decay, a0, w_iclr, w_gate_out, k_k, k_a, r_k, lnx_g, lnx_b, conv_w, w_out, ln1_g, ln1_b, w_group_router, b_group_router, w_expert_router, b_expert_router, w_exp_gate, w_exp_up, w_exp_down, ln2_g, ln2_b):
    bp, sp, d = x_prompt.shape
    bd, td, _ = x_sample.shape
    past_len = page_table.shape[1] * PAGE_SIZE
    depth = w_in.shape[0]
    np_, ns = bp * sp, bd * td

    ctab_p, stab_p = _rope_tables(jnp.arange(sp))
    ctab_s, stab_s = _rope_tables(jnp.repeat(past_len + jnp.arange(td), bd))

    hp = _ln_in(x_prompt.reshape(np_, d), ln_in_g, ln_in_b, 512)
    hs = _ln_in(jnp.swapaxes(x_sample, 0, 1).reshape(ns, d), ln_in_g, ln_in_b, ns)

    krope_t = jnp.swapaxes(cache_krope, 2, 3)
    zero_shift = jnp.zeros((bp, RWKV_PROJ), F32)
    zero_wkv = jnp.zeros((bp, RWKV_HEADS, RWKV_HEAD_DIM, RWKV_HEAD_DIM), F32)
    zero_conv = jnp.zeros((bp, 2, CONV_DIM), F32)

    outs_p = [[] for _ in range(5)]
    outs_s = [[] for _ in range(5)]
    for l in range(depth):
        lp = _layer_params(l, w_in, g_qn, w_uq, g_kvn, w_uk, w_uv, mu_shift, w0, w_decay, a0,
                           w_iclr, w_gate_out, k_k, k_a, r_k, lnx_g, lnx_b, conv_w, w_out, ln1_g,
                           ln1_b, w_group_router, b_group_router, w_expert_router, b_expert_router,
                           w_exp_gate, w_exp_up, w_exp_down, ln2_g, ln2_b)

        def attend_s(q, kcat, l=l):
            qs = q.reshape(MLA_HEADS, td, bd, QHEAD).transpose(2, 0, 1, 3).reshape(
                bd, MLA_HEADS * td, QHEAD)
            kn = kcat.reshape(td, bd, KCAT).transpose(1, 0, 2)
            o = _mla_sample(qs, kn, cache_ckv, krope_t, page_table, l)
            return o.reshape(bd, MLA_HEADS, td, MLA_KV_RANK).transpose(2, 0, 1, 3).reshape(
                ns, MLA_HEADS * MLA_KV_RANK)

        hs, ckv, kr, p_r, cu, wkv = _layer(
            hs, lp, n_seq=bd, t_len=td, time_major=True, ctab=ctab_s, stab=stab_s,
            attend=attend_s, shift0=state_shift[l], wkv0=state_wkv[l], conv0=state_conv[l])
        tmaj = lambda x: jnp.swapaxes(x.reshape(td, bd, -1), 0, 1)
        outs_s[0].append(tmaj(ckv))
        outs_s[1].append(tmaj(kr))
        outs_s[2].append(wkv)
        outs_s[3].append(p_r.reshape(td, bd, RWKV_PROJ)[-1])
        outs_s[4].append(tmaj(cu)[:, -2:])

        hp, ckv, kr, p_r, cu, wkv = _layer(
            hp, lp, n_seq=bp, t_len=sp, time_major=False, ctab=ctab_p, stab=stab_p,
            attend=lambda q, kcat: _mla_prompt(q, kcat, bp, sp),
            shift0=zero_shift, wkv0=zero_wkv, conv0=zero_conv)
        outs_p[0].append(ckv.reshape(-1, PAGE_SIZE, MLA_KV_RANK))
        outs_p[1].append(kr.reshape(-1, PAGE_SIZE, MLA_ROPE_DIM))
        outs_p[2].append(wkv)
        outs_p[3].append(p_r.reshape(bp, sp, RWKV_PROJ)[:, -1])
        outs_p[4].append(cu.reshape(bp, sp, CONV_DIM)[:, -2:])

    y_p = hp.reshape(bp, sp, d)
    y_s = jnp.swapaxes(hs.reshape(td, bd, d), 0, 1)
    return (y_p, y_s) + tuple(jnp.stack(o) for o in outs_p) + tuple(jnp.stack(o) for o in outs_s)
```

```python
import functools

import numpy as np
import jax
import jax.numpy as jnp
from jax import lax
from jax.experimental import pallas as pl
from jax.experimental.pallas import tpu as pltpu
from jax.experimental.pallas import tpu_sc as plsc

F32 = jnp.float32
BF16 = jnp.bfloat16

MLA_HEADS = 8
MLA_NOPE_DIM = 64
MLA_ROPE_DIM = 32
MLA_Q_RANK = 192
MLA_KV_RANK = 128
MLA_V_DIM = 64
ROPE_THETA = 10000.0
ATTN_SCALE = (MLA_NOPE_DIM + MLA_ROPE_DIM) ** -0.5
RWKV_DIM = 256
RWKV_HEADS = 4
RWKV_HEAD_DIM = 64
RWKV_PROJ = 3 * RWKV_DIM + 64 + 64 + 128
CONV_DIM = 256
GN_EPS = 64e-5
N_GROUPS = 4
EXPERTS_PER_GROUP = 8
N_EXPERTS = N_GROUPS * EXPERTS_PER_GROUP
EXPERT_FF = 256
DEPTH = 2
ALPHA = (2 * DEPTH) ** 0.25
LN_EPS = 1e-5
RMS_EPS = 1e-6
PAGE_SIZE = 128

QPAD = 256
QHEAD = 256
KCAT = 256
ONE_LANE = KCAT - 1
LOG2E = 1.4426950408889634
COL_CKV = QPAD
COL_KR = COL_CKV + MLA_KV_RANK
COL_RWKV = COL_KR + 128
COL_CONV = COL_RWKV + RWKV_PROJ
IN_PAD = COL_CONV + 3 * CONV_DIM

LANES = 128
ATT_TQ = 128
ATT_TK = 512
PAGES_PER_STEP = 32
SAMPLE_GROUPS = 4
NEW_PAD = 16
MOE_ROWS = 256
VMEM_LIMIT = 56 * 1024 * 1024


def _cparams(sem):
    return pltpu.CompilerParams(dimension_semantics=sem, vmem_limit_bytes=VMEM_LIMIT)


def _ln_rows(x, g, b):
    mu = jnp.mean(x, -1, keepdims=True)
    xc = x - mu
    var = jnp.mean(xc * xc, -1, keepdims=True)
    return xc * lax.rsqrt(var + LN_EPS) * g + b


def _sigmoid(x):
    return 1.0 / (1.0 + jnp.exp(-x))


HI_HALF = -65536


def _pack_bf16_pairs(x):
    w = x.shape[1] // 2
    bits = lax.bitcast_convert_type(x.astype(BF16).astype(F32), jnp.int32)
    return lax.shift_right_logical(bits[:, :w], 16) | (bits[:, w:] & HI_HALF)


def _unpack_bf16_pairs(p):
    lo = lax.bitcast_convert_type(lax.shift_left(p, 16), F32)
    hi = lax.bitcast_convert_type(p & HI_HALF, F32)
    return jnp.concatenate([lo, hi], axis=-1)


def _pack_row_halves(x, ref):
    half = x.shape[1] // 2
    ref[0] = _pack_bf16_pairs(x[:, :half])
    ref[1] = _pack_bf16_pairs(x[:, half:])


def _unpack_row_halves(p0, p1):
    return jnp.concatenate([_unpack_bf16_pairs(p0), _unpack_bf16_pairs(p1)], axis=-1)


def _moe_mix(y10_ref, y11_ref, y20_ref, y21_ref, gate_ref):
    gate = gate_ref[...]
    return (gate[:, 0:1] * _unpack_row_halves(y10_ref[...], y11_ref[...])
            + gate[:, 1:2] * _unpack_row_halves(y20_ref[...], y21_ref[...]))


def _ln2_kernel(x_ref, y10_ref, y11_ref, y20_ref, y21_ref, gate_ref, g_ref, b_ref, o_ref):
    y = _moe_mix(y10_ref, y11_ref, y20_ref, y21_ref, gate_ref)
    o_ref[...] = _ln_rows(ALPHA * x_ref[...] + y, g_ref[...], b_ref[...])


def _ln2(x, y12, gate, g, b, tm):
    n, d = x.shape
    nt = n // tm
    row = pl.BlockSpec((tm, d), lambda i: (i, 0))
    vec = pl.BlockSpec((1, d), lambda i: (0, 0))
    part = lambda k: pl.BlockSpec((tm, d // 4), lambda i: (i + k * nt, 0))
    return pl.pallas_call(
        _ln2_kernel, grid=(nt,),
        in_specs=[row, part(0), part(1), part(2), part(3),
                  pl.BlockSpec((tm, LANES), lambda i: (i, 0)), vec, vec],
        out_specs=row,
        out_shape=jax.ShapeDtypeStruct((n, d), F32), compiler_params=_cparams(("parallel",)),
    )(x, y12, y12, y12, y12, gate, g.reshape(1, d), b.reshape(1, d))


IN_PROJ_SOURCES = {'plain': 1, 'ln_in': 3, 'ln2': 8}


def _in_proj_kernel(*refs, source):
    n_src = IN_PROJ_SOURCES[source]
    src = refs[:n_src]
    w_ref, gq_ref, wq_ref, gkv_ref, ct_ref, st_ref = refs[n_src:n_src + 6]
    outs = refs[n_src + 6:]
    if source == 'plain':
        h = src[0][...]
    else:
        if source == 'ln_in':
            h = _ln_rows(src[0][...], src[1][...], src[2][...])
        else:
            h = _ln_rows(ALPHA * src[0][...] + _moe_mix(*src[1:6]), src[6][...], src[7][...])
        outs[0][...] = h
        outs = outs[1:]
    q_ref, kcat_ref, ckv_ref, kr_ref, pr_ref, pc_ref = outs
    p = jnp.dot(h.astype(BF16), w_ref[...], preferred_element_type=F32)
    ct = ct_ref[...]
    st = st_ref[...]

    def rope(tile):
        return tile * ct + pltpu.roll(tile, LANES - MLA_ROPE_DIM, 1) * st

    cq = p[:, :QPAD]
    ms = jnp.sum(cq * cq, -1, keepdims=True) * (1.0 / MLA_Q_RANK)
    cqn = (cq * lax.rsqrt(ms + RMS_EPS) * gq_ref[...]).astype(BF16)
    q = jnp.dot(cqn, wq_ref[...], preferred_element_type=F32)
    for h in range(MLA_HEADS):
        base = h * QHEAD
        q_ref[h, :, :LANES] = q[:, base:base + LANES].astype(BF16)
        q_ref[h, :, LANES:] = rope(q[:, base + LANES:base + QHEAD]).astype(BF16)

    c = p[:, COL_CKV:COL_CKV + MLA_KV_RANK]
    ckv = c * lax.rsqrt(jnp.mean(c * c, -1, keepdims=True) + RMS_EPS) * gkv_ref[...]
    ckv_ref[...] = ckv
    kr = rope(p[:, COL_KR:COL_KR + LANES])
    kr_ref[...] = kr[:, :MLA_ROPE_DIM]
    kcat_ref[:, :LANES] = ckv.astype(BF16)
    one = lax.broadcasted_iota(jnp.int32, kr.shape, 1) == ONE_LANE - LANES
    kcat_ref[:, LANES:] = jnp.where(one, 1.0, kr).astype(BF16)
    pr_ref[...] = p[:, COL_RWKV:COL_RWKV + RWKV_PROJ]
    pc_ref[...] = p[:, COL_CONV:COL_CONV + 3 * CONV_DIM]


def _in_proj(source, src, w_in_p, gq_p, w_q, g_kvn, ctab, stab, tm):
    n, d = src[0].shape
    nt = n // tm
    tab_blocks = ctab.shape[0] // tm
    row = lambda c: pl.BlockSpec((tm, c), lambda i: (i, 0))
    full = lambda a: pl.BlockSpec(a.shape, lambda i: (0,) * a.ndim)
    tab = pl.BlockSpec((tm, LANES), lambda i: (i % tab_blocks, 0))
    vec = pl.BlockSpec((1, d), lambda i: (0, 0))
    gkv = g_kvn.reshape(1, MLA_KV_RANK)
    if source == 'plain':
        src_args, src_specs = [src[0]], [row(d)]
    elif source == 'ln_in':
        src_args = [src[0], src[1].reshape(1, d), src[2].reshape(1, d)]
        src_specs = [row(d), vec, vec]
    else:
        x, y12, gate, ln_g, ln_b = src
        part = lambda k: pl.BlockSpec((tm, d // 4), lambda i: (i + k * nt, 0))
        src_args = [x, y12, y12, y12, y12, gate, ln_g.reshape(1, d), ln_b.reshape(1, d)]
        src_specs = [row(d), part(0), part(1), part(2), part(3), row(LANES), vec, vec]
    out_shape = (
        jax.ShapeDtypeStruct((MLA_HEADS, n, QHEAD), BF16),
        jax.ShapeDtypeStruct((n, KCAT), BF16),
        jax.ShapeDtypeStruct((n, MLA_KV_RANK), F32),
        jax.ShapeDtypeStruct((n, MLA_ROPE_DIM), F32),
        jax.ShapeDtypeStruct((n, RWKV_PROJ), F32),
        jax.ShapeDtypeStruct((n, 3 * CONV_DIM), F32),
    )
    out_specs = (
        pl.BlockSpec((MLA_HEADS, tm, QHEAD), lambda i: (0, i, 0)),
        row(KCAT), row(MLA_KV_RANK), row(MLA_ROPE_DIM), row(RWKV_PROJ), row(3 * CONV_DIM),
    )
    if source != 'plain':
        out_shape = (jax.ShapeDtypeStruct((n, d), F32),) + out_shape
        out_specs = (row(d),) + out_specs
    outs = pl.pallas_call(
        functools.partial(_in_proj_kernel, source=source), grid=(nt,),
        in_specs=src_specs + [full(w_in_p), full(gq_p), full(w_q), full(gkv), tab, tab],
        out_specs=out_specs, out_shape=out_shape, compiler_params=_cparams(("parallel",)),
    )(*src_args, w_in_p, gq_p, w_q, gkv, ctab, stab)
    return outs if source != 'plain' else (src[0],) + tuple(outs)


def _mla_prompt_kernel(q_ref, k_ref, o_ref, diag_ref, *, n_variants):
    i = pl.program_id(1)
    rows = MLA_HEADS * ATT_TQ

    @pl.when((pl.program_id(0) == 0) & (i == 0))
    def _():
        qoff = lax.broadcasted_iota(jnp.int32, (MLA_HEADS, ATT_TQ, ATT_TK), 1).reshape(rows, ATT_TK)
        diag_ref[...] = lax.broadcasted_iota(jnp.int32, (rows, ATT_TK), 1) - qoff

    q = q_ref[...].reshape(rows, QHEAD)
    n_full = (i * ATT_TQ) // ATT_TK
    lead = i * ATT_TQ - n_full * ATT_TK

    def chunk(j, carry, masked):
        m, acc = carry
        k = k_ref[j * ATT_TK:(j + 1) * ATT_TK, :]
        s = lax.dot_general(q, k, (((1,), (1,)), ((), ())), preferred_element_type=F32)
        if masked:
            s = jnp.where(diag_ref[...] <= lead, s, -jnp.inf)
        m_new = jnp.maximum(m, jnp.max(s, -1, keepdims=True))
        pr = jnp.exp2(s - m_new).astype(BF16)
        acc = jnp.exp2(m - m_new) * acc + jnp.dot(pr, k, preferred_element_type=F32)
        return m_new, acc

    def variant(v):
        carry = (jnp.full((rows, 1), -jnp.inf, F32), jnp.zeros((rows, KCAT), F32))
        for j in range(v):
            carry = chunk(j, carry, False)
        _, acc = chunk(v, carry, True)
        o = acc[:, :MLA_KV_RANK] / acc[:, ONE_LANE:ONE_LANE + 1]
        for h in range(MLA_HEADS):
            o_ref[:, h * MLA_KV_RANK:(h + 1) * MLA_KV_RANK] = (
                o[h * ATT_TQ:(h + 1) * ATT_TQ].astype(BF16))

    for v in range(n_variants):
        pl.when(n_full == v)(functools.partial(variant, v))


def _mla_prompt(q, kcat, batch, seq):
    n = batch * seq
    nq = seq // ATT_TQ
    return pl.pallas_call(
        functools.partial(_mla_prompt_kernel, n_variants=seq // ATT_TK), grid=(batch, nq),
        in_specs=[pl.BlockSpec((MLA_HEADS, ATT_TQ, QHEAD), lambda b, i: (0, b * nq + i, 0)),
                  pl.BlockSpec((seq, KCAT), lambda b, i: (b, 0))],
        out_specs=pl.BlockSpec((ATT_TQ, MLA_HEADS * MLA_KV_RANK), lambda b, i: (b * nq + i, 0)),
        out_shape=jax.ShapeDtypeStruct((n, MLA_HEADS * MLA_KV_RANK), BF16),
        scratch_shapes=[pltpu.VMEM((MLA_HEADS * ATT_TQ, ATT_TK), jnp.int32)],
        compiler_params=_cparams(("arbitrary", "arbitrary")),
    )(q, kcat)


def _mla_sample_kernel(pt_ref, q_ref, kn_ref, ckv_hbm, krt_hbm, o_ref,
                       ckv_buf, krt_buf, sem, m_ref, l_ref, acc_ref,
                       *, n_new, layer, n_seq, n_chunks):
    npg = PAGES_PER_STEP
    b = pl.program_id(0)
    c = pl.program_id(1)
    step = b * n_chunks + c
    slot = step % 2

    def page_copies(bb, cc, sl):
        copies = []
        for j in range(npg):
            pid = 0 if bb is None else pt_ref[bb, cc * npg + j]
            copies.append(pltpu.make_async_copy(
                ckv_hbm.at[layer, pid], ckv_buf.at[sl, pl.ds(j * PAGE_SIZE, PAGE_SIZE), :],
                sem.at[sl, 0]))
            copies.append(pltpu.make_async_copy(
                krt_hbm.at[layer, pid], krt_buf.at[sl, j], sem.at[sl, 1]))
        return copies

    @pl.when(step == 0)
    def _():
        for cp in page_copies(b, c, slot):
            cp.start()

    @pl.when(step + 1 < n_seq * n_chunks)
    def _():
        nxt = step + 1
        for cp in page_copies(nxt // n_chunks, nxt % n_chunks, 1 - slot):
            cp.start()

    for cp in page_copies(None, None, slot):
        cp.wait()

    @pl.when(c == 0)
    def _():
        m_ref[...] = jnp.full_like(m_ref, -jnp.inf)
        l_ref[...] = jnp.zeros_like(l_ref)
        acc_ref[...] = jnp.zeros_like(acc_ref)

    q = q_ref[...]
    q_lat = q[:, :MLA_KV_RANK]
    q_rope = q[:, MLA_KV_RANK:MLA_KV_RANK + MLA_ROPE_DIM]
    nt = (((1,), (1,)), ((), ()))

    def update(stats, s, v):
        m, l, acc = stats
        m_new = jnp.maximum(m, jnp.max(s, -1, keepdims=True))
        a = jnp.exp2(m - m_new)
        pr = jnp.exp2(s - m_new)
        return (m_new, a * l + jnp.sum(pr, -1, keepdims=True),
                a * acc + jnp.dot(pr.astype(BF16), v, preferred_element_type=F32))

    groups = range(SAMPLE_GROUPS)
    stats = [(m_ref[g], l_ref[g], acc_ref[g]) for g in groups]
    ppg = npg // SAMPLE_GROUPS
    ks = [ckv_buf[slot, pl.ds(g * ppg * PAGE_SIZE, ppg * PAGE_SIZE), :].astype(BF16) for g in groups]
    krs = [jnp.concatenate([krt_buf[slot, g * ppg + j] for j in range(ppg)], axis=1).astype(BF16)
           for g in groups]
    ss = [lax.dot_general(q_lat, ks[g], nt, preferred_element_type=F32)
          + jnp.dot(q_rope, krs[g], preferred_element_type=F32) for g in groups]
    m_new = [jnp.maximum(stats[g][0], jnp.max(ss[g], -1, keepdims=True)) for g in groups]
    prs = [jnp.exp2(ss[g] - m_new[g]) for g in groups]
    pvs = [jnp.dot(prs[g].astype(BF16), ks[g], preferred_element_type=F32) for g in groups]
    for g in groups:
        a = jnp.exp2(stats[g][0] - m_new[g])
        stats[g] = (m_new[g], a * stats[g][1] + jnp.sum(prs[g], -1, keepdims=True),
                    a * stats[g][2] + pvs[g])
        m_ref[g], l_ref[g], acc_ref[g] = stats[g]

    @pl.when(c == pl.num_programs(1) - 1)
    def _():
        kn = kn_ref[...]
        sn = lax.dot_general(q, kn, nt, preferred_element_type=F32)
        t_row = lax.broadcasted_iota(jnp.int32, sn.shape, 0) % n_new
        sn = jnp.where(lax.broadcasted_iota(jnp.int32, sn.shape, 1) <= t_row, sn, -jnp.inf)
        final = list(stats)
        final[0] = update(final[0], sn, kn[:, :MLA_KV_RANK])
        m_all = functools.reduce(jnp.maximum, [st[0] for st in final])
        l = jnp.zeros_like(m_all)
        acc = jnp.zeros(acc_ref.shape[1:], F32)
        for m_g, l_g, acc_g in final:
            wg = jnp.exp2(m_g - m_all)
            l = l + wg * l_g
            acc = acc + wg * acc_g
        o_ref[...] = (acc / l).astype(BF16)


def _mla_sample(q, k_new, cache_ckv, cache_krope_t, page_table, layer):
    bd, rows, _ = q.shape
    n_new = k_new.shape[1]
    k_new = jnp.pad(k_new, ((0, 0), (0, NEW_PAD - n_new), (0, 0)))
    n_pages = page_table.shape[1]
    npg = PAGES_PER_STEP
    assert n_pages % npg == 0
    n_chunks = n_pages // npg
    hbm = pl.BlockSpec(memory_space=pl.ANY)
    in_specs = [pl.BlockSpec((None, rows, QHEAD), lambda b, c, pt: (b, 0, 0)),
                pl.BlockSpec((None, NEW_PAD, KCAT), lambda b, c, pt: (b, 0, 0)), hbm, hbm]
    grid_spec = pltpu.PrefetchScalarGridSpec(
        num_scalar_prefetch=1, grid=(bd, n_chunks), in_specs=in_specs,
        out_specs=pl.BlockSpec((None, rows, MLA_KV_RANK), lambda b, c, pt: (b, 0, 0)),
        scratch_shapes=[pltpu.VMEM((2, npg * PAGE_SIZE, MLA_KV_RANK), F32),
                        pltpu.VMEM((2, npg, MLA_ROPE_DIM, PAGE_SIZE), F32),
                        pltpu.SemaphoreType.DMA((2, 2)),
                        pltpu.VMEM((SAMPLE_GROUPS, rows, 1), F32),
                        pltpu.VMEM((SAMPLE_GROUPS, rows, 1), F32),
                        pltpu.VMEM((SAMPLE_GROUPS, rows, MLA_KV_RANK), F32)])
    return pl.pallas_call(
        functools.partial(_mla_sample_kernel, n_new=n_new, layer=layer, n_seq=bd,
                          n_chunks=n_chunks),
        grid_spec=grid_spec, out_shape=jax.ShapeDtypeStruct((bd, rows, MLA_KV_RANK), BF16),
        compiler_params=_cparams(("arbitrary", "arbitrary")),
    )(page_table, q, k_new, cache_ckv, cache_krope_t)


def _prev_rows(x, before, shift):
    if shift == 1:
        row = lax.broadcasted_iota(jnp.int32, x.shape, 0)
        return jnp.where(row == 0, before, pltpu.roll(x, 1, 0))
    return jnp.concatenate([before, x[:x.shape[0] - shift]], axis=0)


def _mix_prep_kernel(pr_ref, pc_ref, sh0_ref, cv0_ref, mu_ref, w0_ref, a0_ref, wdec_ref, wiclr_ref,
                     wgate_ref, cw_ref,
                     r_ref, k_ref, v_ref, w_ref, a_ref, g_ref, co_ref, cu_ref,
                     sh_c, cu1_c, cu2_c, *, shift, carry):
    first = pl.program_id(1) == 0

    if carry:
        @pl.when(first)
        def _():
            sh_c[...] = sh0_ref[...]
            cu1_c[...] = cv0_ref[1]
            cu2_c[...] = cv0_ref[0]
        sh_before, cu1_before, cu2_before = sh_c[...], cu1_c[...], cu2_c[...]
    else:
        sh_before, cu1_before, cu2_before = sh0_ref[...], cv0_ref[1], cv0_ref[0]

    p = pr_ref[...]
    xs = p + (_prev_rows(p, sh_before, shift) - p) * mu_ref[...]
    r_ref[...] = xs[:, :RWKV_DIM]
    k_ref[...] = xs[:, RWKV_DIM:2 * RWKV_DIM]
    v_ref[...] = xs[:, 2 * RWKV_DIM:3 * RWKV_DIM]
    lora = xs[:, 3 * RWKV_DIM:3 * RWKV_DIM + LANES]
    z = w0_ref[...] + jnp.dot(jnp.tanh(lora).astype(BF16), wdec_ref[...], preferred_element_type=F32)
    nz = -z
    softplus = jnp.maximum(nz, 0.0) + jnp.log(1.0 + jnp.exp(-jnp.abs(nz)))
    w_ref[...] = jnp.exp(-jnp.exp(-softplus - 0.5))
    a_ref[...] = _sigmoid(a0_ref[...] + jnp.dot(lora.astype(BF16), wiclr_ref[...],
                                                preferred_element_type=F32))
    xg = xs[:, 3 * RWKV_DIM + LANES:]
    g_ref[...] = jnp.dot(_sigmoid(xg).astype(BF16), wgate_ref[...], preferred_element_type=F32)

    pc = pc_ref[...]
    cu = pc[:, CONV_DIM:2 * CONV_DIM] * pc[:, 2 * CONV_DIM:]
    cu1 = _prev_rows(cu, cu1_before, shift)
    cu2 = _prev_rows(cu1, cu2_before, shift)
    cw = cw_ref[...]
    co_ref[...] = (pc[:, :CONV_DIM] * (cw[0:1] * cu2 + cw[1:2] * cu1 + cw[2:3] * cu)).astype(BF16)
    cu_ref[...] = cu

    if carry:
        tm = p.shape[0]
        sh_c[...] = p[tm - 1:tm]
        cu1_c[...] = cu[tm - 1:tm]
        cu2_c[...] = cu1[tm - 1:tm]


def _mix_prep(p_r, p_c, shift0, conv0, mu, w0, a0, wdec_p, wiclr_p, wgate, conv_w, *, n_seq, tm, shift):
    n = p_r.shape[0]
    carry = shift == 1
    tps = n // n_seq // tm if carry else 1
    grid = (n_seq, tps) if carry else (1, 1)
    row = lambda c: pl.BlockSpec((tm, c), lambda b, t: (b * tps + t, 0))
    full = lambda a: pl.BlockSpec(a.shape, lambda b, t: (0,) * a.ndim)
    if carry:
        sh_spec = pl.BlockSpec((None, 1, RWKV_PROJ), lambda b, t: (b, 0, 0))
        cv_spec = pl.BlockSpec((None, 2, 1, CONV_DIM), lambda b, t: (b, 0, 0, 0))
        shift0 = shift0.reshape(n_seq, 1, RWKV_PROJ)
        conv0 = conv0.reshape(n_seq, 2, 1, CONV_DIM)
    else:
        sh_spec = full(shift0)
        conv0 = jnp.swapaxes(conv0, 0, 1)
        cv_spec = full(conv0)
    vec = lambda a: a.reshape(1, -1)
    args = (p_r, p_c, shift0, conv0, vec(mu), vec(w0), vec(a0), wdec_p, wiclr_p, wgate, conv_w)
    in_specs = [row(RWKV_PROJ), row(3 * CONV_DIM), sh_spec, cv_spec] + [full(a) for a in args[4:]]
    f32o = jax.ShapeDtypeStruct((n, RWKV_DIM), F32)
    out_shape = (f32o,) * 6 + (jax.ShapeDtypeStruct((n, CONV_DIM), BF16), f32o)
    rows_before = shift
    return pl.pallas_call(
        functools.partial(_mix_prep_kernel, shift=shift, carry=carry), grid=grid,
        in_specs=in_specs, out_specs=(row(RWKV_DIM),) * 8, out_shape=out_shape,
        scratch_shapes=[pltpu.VMEM((rows_before, RWKV_PROJ), F32),
                        pltpu.VMEM((rows_before, CONV_DIM), F32),
                        pltpu.VMEM((rows_before, CONV_DIM), F32)],
        compiler_params=_cparams(("parallel", "arbitrary")),
    )(*args)


def _wkv_kernel(r_ref, k_ref, v_ref, w_ref, a_ref, kk_ref, ka_ref, rk_ref, gn_ref, bn_ref, s0_ref,
                y_ref, sout_ref, s_ref, al_ref, wr_ref, be_ref, km_ref, wd_ref):
    tb = pl.program_id(1)
    hd = RWKV_HEAD_DIM

    @pl.when(tb == 0)
    def _():
        s_ref[...] = s0_ref[...]

    kk_t, ka_t, rk_t = kk_ref[...], ka_ref[...], rk_ref[...]
    gn_t, bn_t = gn_ref[...], bn_ref[...]

    def step(t, _):
        r, k, v, w, a = r_ref[t], k_ref[t], v_ref[t], w_ref[t], a_ref[t]
        kk = k * kk_t
        kk = kk / jnp.maximum(jnp.sqrt(jnp.sum(kk * kk, 0, keepdims=True)), 1e-12)
        km = k * (1.0 + (a - 1.0) * ka_t)
        be = kk * a
        al_ref[...] = -kk
        wr_ref[...] = w * r
        be_ref[...] = be
        km_ref[...] = km
        wd_ref[...] = w
        sa = jnp.zeros((hd, LANES), F32)
        y0 = jnp.zeros((hd, LANES), F32)
        for i in range(hd):
            s_i = s_ref[i]
            sa = sa + s_i * al_ref[i:i + 1, :]
            y0 = y0 + s_i * wr_ref[i:i + 1, :]
        br = jnp.sum(be * r, 0, keepdims=True)
        kr = jnp.sum(km * r, 0, keepdims=True)
        y = y0 + sa * br + v * kr
        for i in range(hd):
            s_ref[i] = s_ref[i] * wd_ref[i:i + 1, :] + sa * be_ref[i:i + 1, :] + v * km_ref[i:i + 1, :]
        mean = jnp.mean(y, 0, keepdims=True)
        yc = y - mean
        var = jnp.mean(yc * yc, 0, keepdims=True)
        bonus = jnp.sum(r * km * rk_t, 0, keepdims=True) * v
        y_ref[t] = yc * lax.rsqrt(var + GN_EPS) * gn_t + bn_t + bonus
        return 0

    lax.fori_loop(0, r_ref.shape[0], step, 0)

    @pl.when(tb == pl.num_programs(1) - 1)
    def _():
        sout_ref[...] = s_ref[...]


def _wkv(r, k, v, w, a, kk_t, ka_t, rk_t, gn_t, bn_t, s0, tt):
    t_len, hd, lanes = r.shape
    seq = pl.BlockSpec((tt, hd, LANES), lambda g, t: (t, 0, g))
    par = pl.BlockSpec((hd, LANES), lambda g, t: (0, g))
    st = pl.BlockSpec((hd, hd, LANES), lambda g, t: (0, 0, g))
    return pl.pallas_call(
        _wkv_kernel, grid=(lanes // LANES, t_len // tt),
        in_specs=[seq] * 5 + [par] * 5 + [st], out_specs=(seq, st),
        out_shape=(jax.ShapeDtypeStruct((t_len, hd, lanes), F32),
                   jax.ShapeDtypeStruct((hd, hd, lanes), F32)),
        scratch_shapes=[pltpu.VMEM((hd, hd, LANES), F32)] + [pltpu.VMEM((hd, LANES), F32)] * 5,
        compiler_params=_cparams(("parallel", "arbitrary")),
    )(r, k, v, w, a, kk_t, ka_t, rk_t, gn_t, bn_t, s0)


WKV_CHUNK = 64
WKV_BLOCK = 256


def _bf16_parts(x, n):
    parts = []
    for _ in range(n):
        p = x.astype(BF16)
        parts.append(p)
        x = x - p.astype(F32)
    return parts


def _dot(a, b):
    return jnp.dot(a.astype(BF16), b.astype(BF16), preferred_element_type=F32)


def _dot_nt(a, b):
    return lax.dot_general(a.astype(BF16), b.astype(BF16), (((1,), (1,)), ((), ())),
                           preferred_element_type=F32)


def _wkv_chunk_kernel(r_ref, k_ref, v_ref, w_ref, a_ref, par_ref, s0_ref, y_ref, sout_ref, st_ref):
    c_len = WKV_CHUNK
    width = RWKV_DIM
    shift = RWKV_HEAD_DIM.bit_length() - 1

    @pl.when(pl.program_id(1) == 0)
    def _():
        st_ref[...] = s0_ref[...]

    row = lax.broadcasted_iota(jnp.int32, (width, width), 0)
    col = lax.broadcasted_iota(jnp.int32, (width, width), 1)
    same_head = (row >> shift) == (col >> shift)
    eye = row == col
    head_ones = jnp.where(same_head, 1.0, 0.0).astype(BF16)
    t_idx = lax.broadcasted_iota(jnp.int32, (c_len, width), 0)
    s_idx = lax.broadcasted_iota(jnp.int32, (c_len, width), 1) & (c_len - 1)
    strict, incl = s_idx < t_idx, s_idx <= t_idx

    def head_sum(x):
        return jnp.dot(x.astype(BF16), head_ones, preferred_element_type=F32)

    def blockdiag(z):
        return jnp.where(same_head, jnp.concatenate([z] * RWKV_HEADS, axis=0), 0.0)

    par = par_ref[...]
    k_k, k_a, r_k, gn, bn = (par[i:i + 1] for i in range(5))
    r_all, k_all, v_all, w_all, a_all = r_ref[...], k_ref[...], v_ref[...], w_ref[...], a_ref[...]
    kk_all = k_all * k_k
    kk_all = kk_all / jnp.maximum(jnp.sqrt(head_sum(kk_all * kk_all)), 1e-12)
    km_all = k_all * (1.0 + (a_all - 1.0) * k_a)
    logw_all = jnp.log(w_all)
    bonus_all = head_sum(r_all * km_all * r_k) * v_all

    n_chunks = r_all.shape[0] // c_len
    assert n_chunks == RWKV_HEADS
    tri = jnp.where(same_head & (col <= row), 1.0, 0.0).astype(BF16)
    logp = sum(jnp.dot(tri, p, preferred_element_type=F32) for p in _bf16_parts(logw_all, 3))
    p_inc, p_inv, p_prev = jnp.exp(logp), jnp.exp(-logp), jnp.exp(logp - logw_all)
    at_all, rt_all = -kk_all * p_prev, r_all * p_inc
    bt_all, kt_all = kk_all * a_all * p_inv, km_all * p_inv

    chunks = range(n_chunks)
    rows = [slice(c * c_len, (c + 1) * c_len) for c in chunks]
    p_end = [p_inc[(c + 1) * c_len - 1:(c + 1) * c_len] for c in chunks]
    at, rt, bt, kt = ([x[sl] for sl in rows] for x in (at_all, rt_all, bt_all, kt_all))
    bt_bd = [blockdiag(x) for x in bt]
    kt_bd = [blockdiag(x) for x in kt]
    v_bd = [blockdiag(v_all[sl]) for sl in rows]
    at_bd = [blockdiag(x) for x in at]
    a_ab = [jnp.where(strict, _dot_nt(at[c], bt_bd[c]), 0.0) for c in chunks]
    a_ak = [jnp.where(strict, _dot_nt(at[c], kt_bd[c]), 0.0) for c in chunks]
    a_rb = [jnp.where(incl, _dot_nt(rt[c], bt_bd[c]), 0.0) for c in chunks]
    a_rk = [jnp.where(incl, _dot_nt(rt[c], kt_bd[c]), 0.0) for c in chunks]
    lp = [blockdiag(x) for x in a_ab]
    ident = jnp.where(eye, 1.0, 0.0)
    tinv = [ident + x for x in lp]
    for _ in range(c_len.bit_length() - 2):
        lp = [_dot(x, x) for x in lp]
        tinv = [tinv[c] + _dot(tinv[c], lp[c]) for c in chunks]
    g_bd = [_dot(blockdiag(a_ak[c]), v_bd[c]) for c in chunks]
    w_bd = [_dot(tinv[c], at_bd[c]) for c in chunks]
    u0_bd = [_dot(tinv[c], g_bd[c]) for c in chunks]
    q = [rt[c] + _dot(a_rb[c], w_bd[c]) for c in chunks]
    y0 = [_dot(a_rb[c], u0_bd[c]) + _dot(a_rk[c], v_bd[c]) for c in chunks]
    bs_t = [blockdiag(bt[c] * p_end[c]).T for c in chunks]
    ks_t = [blockdiag(kt[c] * p_end[c]).T for c in chunks]
    m = [jnp.where(eye, p_end[c], 0.0) + _dot(bs_t[c], w_bd[c]) for c in chunks]
    n0 = [_dot(bs_t[c], u0_bd[c]) + _dot(ks_t[c], v_bd[c]) for c in chunks]

    st = st_ref[...]
    ys = []
    for c in chunks:
        ys.append(_dot(q[c], st) + y0[c])
        st = _dot(m[c], st) + n0[c]
    st_ref[...] = st

    y = jnp.concatenate(ys, axis=0)
    mean = head_sum(y) * (1.0 / RWKV_HEAD_DIM)
    yc = y - mean
    var = head_sum(yc * yc) * (1.0 / RWKV_HEAD_DIM)
    y_ref[...] = yc * lax.rsqrt(var + GN_EPS) * gn + bn + bonus_all

    @pl.when(pl.program_id(1) == pl.num_programs(1) - 1)
    def _():
        sout_ref[...] = st_ref[...]


def _wkv_chunked(r, k, v, w, a, par, s0_bd, n_seq):
    n, width = r.shape
    nblk = n // n_seq // WKV_BLOCK
    seq = pl.BlockSpec((WKV_BLOCK, width), lambda b, j: (b * nblk + j, 0))
    st = pl.BlockSpec((None, width, width), lambda b, j: (b, 0, 0))
    return pl.pallas_call(
        _wkv_chunk_kernel, grid=(n_seq, nblk),
        in_specs=[seq] * 5 + [pl.BlockSpec(par.shape, lambda b, j: (0, 0)), st],
        out_specs=(seq, st),
        out_shape=(jax.ShapeDtypeStruct((n, width), F32),
                   jax.ShapeDtypeStruct((n_seq, width, width), F32)),
        scratch_shapes=[pltpu.VMEM((width, width), F32)],
        compiler_params=_cparams(("parallel", "arbitrary")),
    )(r, k, v, w, a, par, s0_bd)


def _mix_out_kernel(h_ref, o_ref, y_ref, g_ref, co_ref, wo_ref, g1_ref, b1_ref, wr_ref, br_ref,
                    x_ref, xp_ref, gate_ref, route_ref, cnt_ref):
    mix = jnp.concatenate(
        [o_ref[...], (y_ref[...] * g_ref[...]).astype(BF16), co_ref[...]], axis=-1)
    pre = ALPHA * h_ref[...] + jnp.dot(mix, wo_ref[...], preferred_element_type=F32)
    x = _ln_rows(pre, g1_ref[...], b1_ref[...])
    x_ref[...] = x
    xb = x.astype(BF16)
    _pack_row_halves(x, xp_ref)

    logits = jnp.dot(xb, wr_ref[...], preferred_element_type=F32) + br_ref[...]
    lane = lax.broadcasted_iota(jnp.int32, logits.shape, 1)
    ninf = -jnp.inf
    lg = jnp.where(lane < N_GROUPS, logits, ninf)
    mg = jnp.max(lg, -1, keepdims=True)
    g_p = 1.0 / jnp.sum(jnp.exp(lg - mg), -1, keepdims=True)
    g_idx = jnp.min(jnp.where(lg == mg, lane, LANES), -1, keepdims=True)
    lo = N_GROUPS + EXPERTS_PER_GROUP * g_idx
    le = jnp.where((lane >= lo) & (lane < lo + EXPERTS_PER_GROUP), logits, ninf)
    m1 = jnp.max(le, -1, keepdims=True)
    i1 = jnp.min(jnp.where(le == m1, lane, LANES), -1, keepdims=True)
    le2 = jnp.where(lane == i1, ninf, le)
    m2 = jnp.max(le2, -1, keepdims=True)
    i2 = jnp.min(jnp.where(le2 == m2, lane, LANES), -1, keepdims=True)
    e2 = jnp.exp(m2 - m1)
    gate1 = g_p / (1.0 + e2)
    gate2 = g_p * e2 / (1.0 + e2)
    gate_ref[...] = jnp.where(lane == 0, gate1, jnp.where(lane == 1, gate2, 0.0))

    pick1, pick2 = lane == i1, lane == i2
    picks = jnp.where(pick1 | pick2, 1.0, 0.0)
    tm = logits.shape[0]
    tri = jnp.where(lax.broadcasted_iota(jnp.int32, (tm, tm), 1)
                    < lax.broadcasted_iota(jnp.int32, (tm, tm), 0), 1.0, 0.0).astype(BF16)
    before = jnp.dot(tri, picks.astype(BF16), preferred_element_type=F32)
    rank1 = jnp.sum(jnp.where(pick1, before, 0.0), -1, keepdims=True).astype(jnp.int32)
    rank2 = jnp.sum(jnp.where(pick2, before, 0.0), -1, keepdims=True).astype(jnp.int32)
    route_ref[...] = jnp.where(lane == 0, i1, jnp.where(lane == 1, i2, jnp.where(
        lane == 2, rank1, jnp.where(lane == 3, rank2, 0))))
    cnt_ref[...] = jnp.sum(picks, 0, keepdims=True).astype(jnp.int32)


def _moe_pos_kernel(route_ref, base_ref, pos_ref):
    route = route_ref[...]
    base = base_ref[...]
    lane = lax.broadcasted_iota(jnp.int32, route.shape, 1)
    pos = []
    for c in range(2):
        start = jnp.sum(jnp.where(lane == route[:, c:c + 1], base, 0), -1, keepdims=True)
        pos.append(start + route[:, 2 + c:3 + c])
    pos_ref[...] = jnp.where(lane == 0, pos[0], jnp.where(lane == 1, pos[1], 0))


def _moe_pos(route, base, tm):
    n = route.shape[0]
    row = pl.BlockSpec((tm, LANES), lambda i: (i, 0))
    return pl.pallas_call(
        _moe_pos_kernel, grid=(n // tm,),
        in_specs=[row, pl.BlockSpec((None, 1, LANES), lambda i: (i, 0, 0))], out_specs=row,
        out_shape=jax.ShapeDtypeStruct((n, LANES), jnp.int32),
        compiler_params=_cparams(("parallel",)),
    )(route, base)


def _mix_out(h, o_lat, y, g, c_out, w_o, ln_g, ln_b, w_router, b_router, tm):
    n, d = h.shape
    row = lambda c: pl.BlockSpec((tm, c), lambda i: (i, 0))
    full = lambda a: pl.BlockSpec(a.shape, lambda i: (0,) * a.ndim)
    vec = lambda a: a.reshape(1, -1)
    args = (h, o_lat, y, g, c_out, w_o, vec(ln_g), vec(ln_b), w_router, b_router)
    in_specs = [row(d), row(o_lat.shape[1]), row(RWKV_DIM), row(RWKV_DIM), row(CONV_DIM)] + [
        full(a) for a in args[5:]]
    out_shape = (jax.ShapeDtypeStruct((n, d), F32), jax.ShapeDtypeStruct((2, n, d // 4), jnp.int32),
                 jax.ShapeDtypeStruct((n, LANES), F32), jax.ShapeDtypeStruct((n, LANES), jnp.int32),
                 jax.ShapeDtypeStruct((n // tm, 1, LANES), jnp.int32))
    return pl.pallas_call(
        _mix_out_kernel, grid=(n // tm,), in_specs=in_specs,
        out_specs=(row(d), pl.BlockSpec((2, tm, d // 4), lambda i: (0, i, 0)), row(LANES), row(LANES),
                   pl.BlockSpec((None, 1, LANES), lambda i: (i, 0, 0))),
        out_shape=out_shape, compiler_params=_cparams(("parallel",)),
    )(*args)


def _moe_kernel(be_ref, nv_ref, x0_ref, x1_ref, wgu_ref, wd_ref, o_ref):
    valid = nv_ref[pl.program_id(0)]

    @pl.when(valid > 0)
    def _():
        keep = lax.broadcasted_iota(jnp.int32, x0_ref.shape, 0) < valid
        x = _unpack_row_halves(jnp.where(keep, x0_ref[...], 0),
                               jnp.where(keep, x1_ref[...], 0)).astype(BF16)
        gu = jnp.dot(x, wgu_ref[...], preferred_element_type=F32)
        hg = gu[:, :EXPERT_FF]
        hid = hg * _sigmoid(hg) * gu[:, EXPERT_FF:]
        _pack_row_halves(jnp.dot(hid.astype(BF16), wd_ref[...], preferred_element_type=F32), o_ref)

    @pl.when(valid == 0)
    def _():
        o_ref[...] = jnp.zeros_like(o_ref)


def _moe_experts(blk_e, blk_valid, xs, cap, w_gu, w_d):
    quarter = xs.shape[1]
    d = 4 * quarter
    nb = cap // MOE_ROWS
    grid_spec = pltpu.PrefetchScalarGridSpec(
        num_scalar_prefetch=2, grid=(nb,),
        in_specs=[pl.BlockSpec((MOE_ROWS, quarter), lambda i, be, nv: (i, 0)),
                  pl.BlockSpec((MOE_ROWS, quarter), lambda i, be, nv: (i + nb, 0)),
                  pl.BlockSpec((None, d, 2 * EXPERT_FF), lambda i, be, nv: (be[i], 0, 0)),
                  pl.BlockSpec((None, EXPERT_FF, d), lambda i, be, nv: (be[i], 0, 0))],
        out_specs=pl.BlockSpec((2, MOE_ROWS, quarter), lambda i, be, nv: (0, i, 0)))
    return pl.pallas_call(
        _moe_kernel, grid_spec=grid_spec,
        out_shape=jax.ShapeDtypeStruct((2, cap, quarter), jnp.int32),
        compiler_params=_cparams(("arbitrary",)),
    )(blk_e, blk_valid, xs, xs, w_gu, w_d)


def _sc_mesh():
    return plsc.VectorSubcoreMesh(core_axis_name="core", subcore_axis_name="subcore")


SC_WINDOW = 128


def _sc_rows(n):
    mesh = _sc_mesh()
    unit = SC_WINDOW * mesh.num_cores * mesh.num_subcores
    return -(-n // unit) * unit


def _scatter_rows2(x, idx1, idx2, n_out):
    n, w = x.shape
    mesh = _sc_mesh()
    win = SC_WINDOW
    assert n == _sc_rows(n)

    @pl.kernel(out_type=jax.ShapeDtypeStruct((n_out, w), x.dtype), mesh=mesh, scratch_types=[])
    def scatter(x_hbm, i1_hbm, i2_hbm, o_hbm):
        def body(x_vmem, i1_vmem, i2_vmem):
            pltpu.sync_copy(x_vmem, o_hbm.at[i1_vmem.at[0]])
            pltpu.sync_copy(x_vmem, o_hbm.at[i2_vmem.at[0]])

        idx_spec = pl.BlockSpec((1, win), lambda i: (0, i))
        pltpu.emit_pipeline(
            body, grid=(n // win,),
            in_specs=[pl.BlockSpec((win, w), lambda i: (i, 0)), idx_spec, idx_spec],
            out_specs=[], core_axis_name=('core', 'subcore'),
            dimension_semantics=(pltpu.PARALLEL,),
        )(x_hbm, i1_hbm, i2_hbm)

    return scatter(x, idx1.reshape(1, n), idx2.reshape(1, n))


def _gather_rows(x, idx):
    n = idx.shape[0]
    w = x.shape[1]
    mesh = _sc_mesh()
    win = SC_WINDOW
    assert n == _sc_rows(n)

    @pl.kernel(out_type=jax.ShapeDtypeStruct((n, w), x.dtype), mesh=mesh, scratch_types=[])
    def gather(x_hbm, i_hbm, o_hbm):
        def body(i_vmem, o_vmem):
            pltpu.sync_copy(x_hbm.at[i_vmem.at[0]], o_vmem)

        pltpu.emit_pipeline(
            body, grid=(n // win,),
            in_specs=[pl.BlockSpec((1, win), lambda i: (0, i))],
            out_specs=[pl.BlockSpec((win, w), lambda i: (i, 0))],
            core_axis_name=('core', 'subcore'), dimension_semantics=(pltpu.PARALLEL,),
        )(i_hbm, o_hbm)

    return gather(x, idx.reshape(1, n))


def _moe_layout(tile_counts, n_tok):
    cnt = tile_counts[:, 0, N_GROUPS:N_GROUPS + N_EXPERTS]
    tile_off = jnp.cumsum(cnt, axis=0) - cnt
    total = jnp.sum(cnt, axis=0)
    pcounts = (total + MOE_ROWS - 1) // MOE_ROWS * MOE_ROWS
    pends = jnp.cumsum(pcounts)
    base = (pends - pcounts)[None, :] + tile_off
    base = jnp.pad(base, ((0, 0), (N_GROUPS, LANES - N_GROUPS - N_EXPERTS)))[:, None, :]
    n_blocks = 2 * n_tok // MOE_ROWS + N_EXPERTS
    blk_start = jnp.arange(n_blocks, dtype=jnp.int32) * MOE_ROWS
    blk_e = jnp.minimum(jnp.sum((pends[None, :] <= blk_start[:, None]).astype(jnp.int32), axis=1),
                        N_EXPERTS - 1)
    blk_valid = jnp.clip((pends - pcounts + total)[blk_e] - blk_start, 0, MOE_ROWS)
    return (base.astype(jnp.int32), blk_e.astype(jnp.int32), blk_valid.astype(jnp.int32),
            n_blocks * MOE_ROWS)


def _rope_tables(pos):
    half = MLA_ROPE_DIM // 2
    freqs = ROPE_THETA ** (-jnp.arange(half, dtype=F32) / half)
    ang = pos.astype(F32)[:, None] * freqs
    cos, sin = jnp.cos(ang), jnp.sin(ang)
    pad = jnp.zeros((pos.shape[0], LANES - MLA_ROPE_DIM), F32)
    return (jnp.concatenate([cos, cos, pad], -1), jnp.concatenate([-sin, sin, pad], -1))


def _swap_halves(w):
    half = MLA_ROPE_DIM // 2
    return jnp.concatenate([w[..., half:], w[..., :half]], -1)


def _layer_params(l, w_in, g_qn, w_uq, g_kvn, w_uk, w_uv, mu_shift, w0, w_decay, a0, w_iclr,
                  w_gate_out, k_k, k_a, r_k, lnx_g, lnx_b, conv_w, w_out, ln1_g, ln1_b,
                  w_group_router, b_group_router, w_expert_router, b_expert_router,
                  w_exp_gate, w_exp_up, w_exp_down, ln2_g, ln2_b):
    d = w_in.shape[1]
    hi = lax.Precision.HIGHEST
    wi = w_in[l]
    zc = lambda c: jnp.zeros((d, c), F32)
    c0 = MLA_Q_RANK
    c1 = c0 + MLA_KV_RANK
    c2 = c1 + MLA_ROPE_DIM
    w_kr = wi[:, c1:c2]
    w_in_p = jnp.concatenate([
        wi[:, :c0], zc(QPAD - MLA_Q_RANK), wi[:, c0:c1],
        w_kr, _swap_halves(w_kr), zc(LANES - 2 * MLA_ROPE_DIM), wi[:, c2:]], -1).astype(BF16)
    gq_p = jnp.concatenate([g_qn[l], jnp.zeros((QPAD - MLA_Q_RANK,), F32)]).reshape(1, QPAD)
    uq = w_uq[l]
    q_lat = jnp.einsum('rhn,chn->rhc', uq[..., :MLA_NOPE_DIM], w_uk[l], precision=hi)
    q_rope = uq[..., MLA_NOPE_DIM:]
    wq = jnp.concatenate([q_lat, q_rope, _swap_halves(q_rope),
                          jnp.zeros((MLA_Q_RANK, MLA_HEADS, QHEAD - LANES - 2 * MLA_ROPE_DIM), F32)], -1)
    wq = (wq * (ATTN_SCALE * LOG2E)).reshape(MLA_Q_RANK, MLA_HEADS * QHEAD)
    wq = jnp.concatenate([wq, jnp.zeros((QPAD - MLA_Q_RANK, MLA_HEADS * QHEAD), F32)], 0).astype(BF16)
    wo = w_out[l]
    mla_dim = MLA_HEADS * MLA_V_DIM
    wo_att = jnp.einsum('chv,hvd->hcd', w_uv[l], wo[:mla_dim].reshape(MLA_HEADS, MLA_V_DIM, -1),
                        precision=hi).reshape(MLA_HEADS * MLA_KV_RANK, -1)
    w_o = jnp.concatenate([wo_att, wo[mla_dim:]], 0).astype(BF16)
    z64 = jnp.zeros((64, RWKV_DIM), F32)
    wdec_p = jnp.concatenate([w_decay[l], z64], 0).astype(BF16)
    wiclr_p = jnp.concatenate([z64, w_iclr[l]], 0).astype(BF16)
    w_router = jnp.concatenate(
        [w_group_router[l], w_expert_router[l],
         jnp.zeros((d, LANES - N_GROUPS - N_EXPERTS), F32)], -1).astype(BF16)
    b_router = jnp.concatenate(
        [b_group_router[l], b_expert_router[l],
         jnp.zeros((LANES - N_GROUPS - N_EXPERTS,), F32)]).reshape(1, LANES)
    w_gu = jnp.concatenate([w_exp_gate[l], w_exp_up[l]], -1).astype(BF16)
    return dict(
        w_in_p=w_in_p, gq_p=gq_p, wq=wq, g_kvn=g_kvn[l], mu=mu_shift[l], w0=w0[l], a0=a0[l],
        wdec_p=wdec_p, wiclr_p=wiclr_p, wgate=w_gate_out[l].astype(BF16), conv_w=conv_w[l],
        k_k=k_k[l], k_a=k_a[l], r_k=r_k[l].reshape(-1), lnx_g=lnx_g[l], lnx_b=lnx_b[l],
        w_o=w_o, ln1_g=ln1_g[l], ln1_b=ln1_b[l], w_router=w_router, b_router=b_router,
        w_gu=w_gu, w_d=w_exp_down[l].astype(BF16), ln2_g=ln2_g[l], ln2_b=ln2_b[l])


def _head_tile(p, n_seq):
    t = p.reshape(RWKV_HEADS, RWKV_HEAD_DIM).T
    return jnp.tile(t, (1, n_seq))


def _layer(source, src, lp, *, n_seq, t_len, time_major, ctab, stab, attend, shift0, wkv0, conv0):
    n, d = src[0].shape
    tm = min(512, n)
    h, q, kcat, ckv, kr, p_r, p_c = _in_proj(source, src, lp['w_in_p'], lp['gq_p'], lp['wq'],
                                             lp['g_kvn'], ctab, stab, tm)
    o_lat = attend(q, kcat)

    r, k, v, w, a, g, c_out, cu = _mix_prep(
        p_r, p_c, shift0, conv0, lp['mu'], lp['w0'], lp['a0'], lp['wdec_p'], lp['wiclr_p'],
        lp['wgate'], lp['conv_w'], n_seq=n_seq, tm=tm, shift=n_seq if time_major else 1)

    names = ('k_k', 'k_a', 'r_k', 'lnx_g', 'lnx_b')
    if not time_major and t_len % WKV_BLOCK == 0:
        par = jnp.stack([lp[nm] for nm in names] + [jnp.zeros_like(lp['k_k'])] * 3)
        head_eye = jnp.eye(RWKV_HEADS, dtype=F32)
        s0_bd = jnp.einsum('bhvk,hg->bhkgv', wkv0, head_eye).reshape(n_seq, RWKV_DIM, RWKV_DIM)
        y, s_fin = _wkv_chunked(r, k, v, w, a, par, s0_bd, n_seq)
        wkv_new = jnp.einsum(
            'bhkgv,hg->bhvk',
            s_fin.reshape(n_seq, RWKV_HEADS, RWKV_HEAD_DIM, RWKV_HEADS, RWKV_HEAD_DIM), head_eye)
    else:
        lanes = n_seq * RWKV_HEADS

        def to_scan(x):
            if time_major:
                x = x.reshape(t_len, n_seq, RWKV_HEADS, RWKV_HEAD_DIM).transpose(0, 3, 1, 2)
            else:
                x = x.reshape(n_seq, t_len, RWKV_HEADS, RWKV_HEAD_DIM).transpose(1, 3, 0, 2)
            return x.reshape(t_len, RWKV_HEAD_DIM, lanes)

        tiles = [_head_tile(lp[nm], n_seq) for nm in names]
        s0 = wkv0.transpose(3, 2, 0, 1).reshape(RWKV_HEAD_DIM, RWKV_HEAD_DIM, lanes)
        y, s_fin = _wkv(*[to_scan(x) for x in (r, k, v, w, a)], *tiles, s0, min(16, t_len))
        y = y.reshape(t_len, RWKV_HEAD_DIM, n_seq, RWKV_HEADS)
        y = (y.transpose(0, 2, 3, 1) if time_major else y.transpose(2, 0, 3, 1)).reshape(n, RWKV_DIM)
        wkv_new = s_fin.reshape(RWKV_HEAD_DIM, RWKV_HEAD_DIM, n_seq, RWKV_HEADS).transpose(2, 3, 1, 0)

    x, xp, gate, route, tile_counts = _mix_out(
        h, o_lat, y, g, c_out, lp['w_o'], lp['ln1_g'], lp['ln1_b'], lp['w_router'], lp['b_router'], tm)
    base, blk_e, blk_valid, cap = _moe_layout(tile_counts, n)
    pos = _moe_pos(route, base, tm)
    pos1, pos2 = pos[:, 0], pos[:, 1]
    n_src = _sc_rows(2 * n)
    spare = 2 * cap + jnp.arange(n_src - 2 * n, dtype=jnp.int32)
    xp = jnp.pad(xp.reshape(2 * n, d // 4), ((0, n_src - 2 * n), (0, 0)))
    xs = _scatter_rows2(xp, jnp.concatenate([pos1, pos1 + cap, spare]),
                        jnp.concatenate([pos2, pos2 + cap, spare]), 2 * cap + n_src - 2 * n)
    yb = _moe_experts(blk_e, blk_valid, xs, cap, lp['w_gu'], lp['w_d'])
    n_dst = _sc_rows(4 * n)
    y12 = _gather_rows(yb.reshape(2 * cap, d // 4), jnp.concatenate(
        [pos1, pos1 + cap, pos2, pos2 + cap, jnp.zeros((n_dst - 4 * n,), jnp.int32)]))
    return (x, y12, gate, lp['ln2_g'], lp['ln2_b']), ckv, kr, p_r, cu, wkv_new


def kernel(x_prompt, x_sample, cache_ckv, cache_krope, state_wkv, state_shift, state_conv, page_table, ln_in_g, ln_in_b, w_in, g_qn, w_uq, g_kvn, w_uk, w_uv, mu_shift, w0, w_decay, a0, w_iclr, w_gate_out, k_k, k_a, r_k, lnx_g, lnx_b, conv_w, w_out, ln1_g, ln1_b, w_group_router, b_group_router, w_expert_router, b_expert_router, w_exp_gate, w_exp_up, w_exp_down, ln2_g, ln2_b):
    bp, sp, d = x_prompt.shape
    bd, td, _ = x_sample.shape
    past_len = page_table.shape[1] * PAGE_SIZE
    depth = w_in.shape[0]
    np_, ns = bp * sp, bd * td

    ctab_p, stab_p = _rope_tables(jnp.arange(sp))
    ctab_s, stab_s = _rope_tables(jnp.repeat(past_len + jnp.arange(td), bd))

    src_p = ('ln_in', (x_prompt.reshape(np_, d), ln_in_g, ln_in_b))
    src_s = ('ln_in', (jnp.swapaxes(x_sample, 0, 1).reshape(ns, d), ln_in_g, ln_in_b))

    krope_t = jnp.swapaxes(cache_krope, 2, 3)
    zero_shift = jnp.zeros((bp, RWKV_PROJ), F32)
    zero_wkv = jnp.zeros((bp, RWKV_HEADS, RWKV_HEAD_DIM, RWKV_HEAD_DIM), F32)
    zero_conv = jnp.zeros((bp, 2, CONV_DIM), F32)

    outs_p = [[] for _ in range(5)]
    outs_s = [[] for _ in range(5)]
    for l in range(depth):
        lp = _layer_params(l, w_in, g_qn, w_uq, g_kvn, w_uk, w_uv, mu_shift, w0, w_decay, a0,
                           w_iclr, w_gate_out, k_k, k_a, r_k, lnx_g, lnx_b, conv_w, w_out, ln1_g,
                           ln1_b, w_group_router, b_group_router, w_expert_router, b_expert_router,
                           w_exp_gate, w_exp_up, w_exp_down, ln2_g, ln2_b)

        def attend_s(q, kcat, l=l):
            qs = q.reshape(MLA_HEADS, td, bd, QHEAD).transpose(2, 0, 1, 3).reshape(
                bd, MLA_HEADS * td, QHEAD)
            kn = kcat.reshape(td, bd, KCAT).transpose(1, 0, 2)
            o = _mla_sample(qs, kn, cache_ckv, krope_t, page_table, l)
            return o.reshape(bd, MLA_HEADS, td, MLA_KV_RANK).transpose(2, 0, 1, 3).reshape(
                ns, MLA_HEADS * MLA_KV_RANK)

        pend_s, ckv, kr, p_r, cu, wkv = _layer(
            *src_s, lp, n_seq=bd, t_len=td, time_major=True, ctab=ctab_s, stab=stab_s,
            attend=attend_s, shift0=state_shift[l], wkv0=state_wkv[l], conv0=state_conv[l])
        tmaj = lambda x: jnp.swapaxes(x.reshape(td, bd, -1), 0, 1)
        outs_s[0].append(tmaj(ckv))
        outs_s[1].append(tmaj(kr))
        outs_s[2].append(wkv)
        outs_s[3].append(p_r.reshape(td, bd, RWKV_PROJ)[-1])
        outs_s[4].append(tmaj(cu)[:, -2:])
        src_s = ('ln2', pend_s)

        pend_p, ckv, kr, p_r, cu, wkv = _layer(
            *src_p, lp, n_seq=bp, t_len=sp, time_major=False, ctab=ctab_p, stab=stab_p,
            attend=lambda q, kcat: _mla_prompt(q, kcat, bp, sp),
            shift0=zero_shift, wkv0=zero_wkv, conv0=zero_conv)
        outs_p[0].append(ckv.reshape(-1, PAGE_SIZE, MLA_KV_RANK))
        outs_p[1].append(kr.reshape(-1, PAGE_SIZE, MLA_ROPE_DIM))
        outs_p[2].append(wkv)
        outs_p[3].append(p_r.reshape(bp, sp, RWKV_PROJ)[:, -1])
        outs_p[4].append(cu.reshape(bp, sp, CONV_DIM)[:, -2:])
        src_p = ('ln2', pend_p)

    hp = _ln2(*pend_p, min(512, np_))
    hs = _ln2(*pend_s, min(512, ns))
    y_p = hp.reshape(bp, sp, d)
    y_s = jnp.swapaxes(hs.reshape(td, bd, d), 0, 1)
    return (y_p, y_s) + tuple(jnp.stack(o) for o in outs_p) + tuple(jnp.stack(o) for o in outs_s)
```

```python
import functools

import numpy as np
import jax
import jax.numpy as jnp
from jax import lax
from jax.experimental import pallas as pl
from jax.experimental.pallas import tpu as pltpu
from jax.experimental.pallas import tpu_sc as plsc

F32 = jnp.float32
BF16 = jnp.bfloat16

MLA_HEADS = 8
MLA_NOPE_DIM = 64
MLA_ROPE_DIM = 32
MLA_Q_RANK = 192
MLA_KV_RANK = 128
MLA_V_DIM = 64
ROPE_THETA = 10000.0
ATTN_SCALE = (MLA_NOPE_DIM + MLA_ROPE_DIM) ** -0.5
RWKV_DIM = 256
RWKV_HEADS = 4
RWKV_HEAD_DIM = 64
RWKV_PROJ = 3 * RWKV_DIM + 64 + 64 + 128
CONV_DIM = 256
GN_EPS = 64e-5
N_GROUPS = 4
EXPERTS_PER_GROUP = 8
N_EXPERTS = N_GROUPS * EXPERTS_PER_GROUP
EXPERT_FF = 256
DEPTH = 2
ALPHA = (2 * DEPTH) ** 0.25
LN_EPS = 1e-5
RMS_EPS = 1e-6
PAGE_SIZE = 128

QPAD = 256
QHEAD = 256
KCAT = 256
ONE_LANE = KCAT - 1
LOG2E = 1.4426950408889634
COL_CKV = QPAD
COL_KR = COL_CKV + MLA_KV_RANK
COL_RWKV = COL_KR + 128
COL_CONV = COL_RWKV + RWKV_PROJ
IN_PAD = COL_CONV + 3 * CONV_DIM

LANES = 128
ATT_TQ = 256
ATT_TK = 512
PAGES_PER_STEP = 32
SAMPLE_GROUPS = 4
SAMPLE_SLOTS = 3
NEW_PAD = 16
MOE_ROWS = 256
VMEM_LIMIT = 56 * 1024 * 1024


def _cparams(sem):
    return pltpu.CompilerParams(dimension_semantics=sem, vmem_limit_bytes=VMEM_LIMIT)


def _ln_rows(x, g, b):
    mu = jnp.mean(x, -1, keepdims=True)
    xc = x - mu
    var = jnp.mean(xc * xc, -1, keepdims=True)
    return xc * lax.rsqrt(var + LN_EPS) * g + b


def _sigmoid(x):
    return 1.0 / (1.0 + jnp.exp(-x))


HI_HALF = -65536


def _pack_bf16_pairs(x):
    w = x.shape[1] // 2
    bits = lax.bitcast_convert_type(x.astype(BF16).astype(F32), jnp.int32)
    return lax.shift_right_logical(bits[:, :w], 16) | (bits[:, w:] & HI_HALF)


def _unpack_bf16_pairs(p):
    lo = lax.bitcast_convert_type(lax.shift_left(p, 16), F32)
    hi = lax.bitcast_convert_type(p & HI_HALF, F32)
    return jnp.concatenate([lo, hi], axis=-1)


def _pack_row_halves(x, ref):
    half = x.shape[1] // 2
    ref[0] = _pack_bf16_pairs(x[:, :half])
    ref[1] = _pack_bf16_pairs(x[:, half:])


def _unpack_row_halves(p0, p1):
    return jnp.concatenate([_unpack_bf16_pairs(p0), _unpack_bf16_pairs(p1)], axis=-1)


def _moe_mix(y10_ref, y11_ref, y20_ref, y21_ref, gate_ref):
    gate = gate_ref[...]
    return (gate[:, 0:1] * _unpack_row_halves(y10_ref[...], y11_ref[...])
            + gate[:, 1:2] * _unpack_row_halves(y20_ref[...], y21_ref[...]))


def _ln2_kernel(x_ref, y10_ref, y11_ref, y20_ref, y21_ref, gate_ref, g_ref, b_ref, o_ref):
    y = _moe_mix(y10_ref, y11_ref, y20_ref, y21_ref, gate_ref)
    o_ref[...] = _ln_rows(ALPHA * x_ref[...] + y, g_ref[...], b_ref[...])


def _ln2(x, y12, gate, g, b, tm):
    n, d = x.shape
    nt = n // tm
    row = pl.BlockSpec((tm, d), lambda i: (i, 0))
    vec = pl.BlockSpec((1, d), lambda i: (0, 0))
    part = lambda k: pl.BlockSpec((tm, d // 4), lambda i: (i + k * nt, 0))
    return pl.pallas_call(
        _ln2_kernel, grid=(nt,),
        in_specs=[row, part(0), part(1), part(2), part(3),
                  pl.BlockSpec((tm, LANES), lambda i: (i, 0)), vec, vec],
        out_specs=row,
        out_shape=jax.ShapeDtypeStruct((n, d), F32), compiler_params=_cparams(("parallel",)),
    )(x, y12, y12, y12, y12, gate, g.reshape(1, d), b.reshape(1, d))


IN_PROJ_SOURCES = {'plain': 1, 'ln_in': 3, 'ln2': 8}


def _in_proj_kernel(*refs, source):
    n_src = IN_PROJ_SOURCES[source]
    src = refs[:n_src]
    w_ref, gq_ref, wq_ref, gkv_ref, ct_ref, st_ref = refs[n_src:n_src + 6]
    outs = refs[n_src + 6:]
    if source == 'plain':
        h = src[0][...]
    else:
        if source == 'ln_in':
            h = _ln_rows(src[0][...], src[1][...], src[2][...])
        else:
            h = _ln_rows(ALPHA * src[0][...] + _moe_mix(*src[1:6]), src[6][...], src[7][...])
        outs[0][...] = h
        outs = outs[1:]
    q_ref, kcat_ref, ckv_ref, kr_ref, pr_ref, pc_ref = outs
    p = jnp.dot(h.astype(BF16), w_ref[...], preferred_element_type=F32)
    ct = ct_ref[...]
    st = st_ref[...]

    def rope(tile):
        return tile * ct + pltpu.roll(tile, LANES - MLA_ROPE_DIM, 1) * st

    cq = p[:, :QPAD]
    ms = jnp.sum(cq * cq, -1, keepdims=True) * (1.0 / MLA_Q_RANK)
    cqn = (cq * lax.rsqrt(ms + RMS_EPS) * gq_ref[...]).astype(BF16)
    q = jnp.dot(cqn, wq_ref[...], preferred_element_type=F32)
    for h in range(MLA_HEADS):
        base = h * QHEAD
        q_ref[h, :, :LANES] = q[:, base:base + LANES].astype(BF16)
        q_ref[h, :, LANES:] = rope(q[:, base + LANES:base + QHEAD]).astype(BF16)

    c = p[:, COL_CKV:COL_CKV + MLA_KV_RANK]
    ckv = c * lax.rsqrt(jnp.mean(c * c, -1, keepdims=True) + RMS_EPS) * gkv_ref[...]
    ckv_ref[...] = ckv
    kr = rope(p[:, COL_KR:COL_KR + LANES])
    kr_ref[...] = kr[:, :MLA_ROPE_DIM]
    kcat_ref[:, :LANES] = ckv.astype(BF16)
    one = lax.broadcasted_iota(jnp.int32, kr.shape, 1) == ONE_LANE - LANES
    kcat_ref[:, LANES:] = jnp.where(one, 1.0, kr).astype(BF16)
    pr_ref[...] = p[:, COL_RWKV:COL_RWKV + RWKV_PROJ]
    pc_ref[...] = p[:, COL_CONV:COL_CONV + 3 * CONV_DIM]


def _in_proj(source, src, w_in_p, gq_p, w_q, g_kvn, ctab, stab, tm):
    n, d = src[0].shape
    nt = n // tm
    tab_blocks = ctab.shape[0] // tm
    row = lambda c: pl.BlockSpec((tm, c), lambda i: (i, 0))
    full = lambda a: pl.BlockSpec(a.shape, lambda i: (0,) * a.ndim)
    tab = pl.BlockSpec((tm, LANES), lambda i: (i % tab_blocks, 0))
    vec = pl.BlockSpec((1, d), lambda i: (0, 0))
    gkv = g_kvn.reshape(1, MLA_KV_RANK)
    if source == 'plain':
        src_args, src_specs = [src[0]], [row(d)]
    elif source == 'ln_in':
        src_args = [src[0], src[1].reshape(1, d), src[2].reshape(1, d)]
        src_specs = [row(d), vec, vec]
    else:
        x, y12, gate, ln_g, ln_b = src
        part = lambda k: pl.BlockSpec((tm, d // 4), lambda i: (i + k * nt, 0))
        src_args = [x, y12, y12, y12, y12, gate, ln_g.reshape(1, d), ln_b.reshape(1, d)]
        src_specs = [row(d), part(0), part(1), part(2), part(3), row(LANES), vec, vec]
    out_shape = (
        jax.ShapeDtypeStruct((MLA_HEADS, n, QHEAD), BF16),
        jax.ShapeDtypeStruct((n, KCAT), BF16),
        jax.ShapeDtypeStruct((n, MLA_KV_RANK), F32),
        jax.ShapeDtypeStruct((n, MLA_ROPE_DIM), F32),
        jax.ShapeDtypeStruct((n, RWKV_PROJ), F32),
        jax.ShapeDtypeStruct((n, 3 * CONV_DIM), F32),
    )
    out_specs = (
        pl.BlockSpec((MLA_HEADS, tm, QHEAD), lambda i: (0, i, 0)),
        row(KCAT), row(MLA_KV_RANK), row(MLA_ROPE_DIM), row(RWKV_PROJ), row(3 * CONV_DIM),
    )
    if source != 'plain':
        out_shape = (jax.ShapeDtypeStruct((n, d), F32),) + out_shape
        out_specs = (row(d),) + out_specs
    outs = pl.pallas_call(
        functools.partial(_in_proj_kernel, source=source), grid=(nt,),
        in_specs=src_specs + [full(w_in_p), full(gq_p), full(w_q), full(gkv), tab, tab],
        out_specs=out_specs, out_shape=out_shape, compiler_params=_cparams(("parallel",)),
    )(*src_args, w_in_p, gq_p, w_q, gkv, ctab, stab)
    return outs if source != 'plain' else (src[0],) + tuple(outs)


def _mla_prompt_kernel(q_ref, k_ref, o_ref, diag_ref, *, n_variants):
    i = pl.program_id(1)
    rows = MLA_HEADS * ATT_TQ

    @pl.when((pl.program_id(0) == 0) & (i == 0))
    def _():
        qoff = lax.broadcasted_iota(jnp.int32, (MLA_HEADS, ATT_TQ, ATT_TK), 1).reshape(rows, ATT_TK)
        diag_ref[...] = lax.broadcasted_iota(jnp.int32, (rows, ATT_TK), 1) - qoff

    q = q_ref[...].reshape(rows, QHEAD)
    n_full = (i * ATT_TQ) // ATT_TK
    lead = i * ATT_TQ - n_full * ATT_TK

    def chunk(j, carry, masked):
        m, acc = carry
        k = k_ref[j * ATT_TK:(j + 1) * ATT_TK, :]
        s = lax.dot_general(q, k, (((1,), (1,)), ((), ())), preferred_element_type=F32)
        if masked:
            s = jnp.where(diag_ref[...] <= lead, s, -jnp.inf)
        m_new = jnp.maximum(m, jnp.max(s, -1, keepdims=True))
        pr = jnp.exp2(s - m_new).astype(BF16)
        acc = jnp.exp2(m - m_new) * acc + jnp.dot(pr, k, preferred_element_type=F32)
        return m_new, acc

    def variant(v):
        carry = (jnp.full((rows, 1), -jnp.inf, F32), jnp.zeros((rows, KCAT), F32))
        for j in range(v):
            carry = chunk(j, carry, False)
        _, acc = chunk(v, carry, True)
        o = acc[:, :MLA_KV_RANK] / acc[:, ONE_LANE:ONE_LANE + 1]
        for h in range(MLA_HEADS):
            o_ref[:, h * MLA_KV_RANK:(h + 1) * MLA_KV_RANK] = (
                o[h * ATT_TQ:(h + 1) * ATT_TQ].astype(BF16))

    for v in range(n_variants):
        pl.when(n_full == v)(functools.partial(variant, v))


def _mla_prompt(q, kcat, batch, seq):
    n = batch * seq
    nq = seq // ATT_TQ
    return pl.pallas_call(
        functools.partial(_mla_prompt_kernel, n_variants=seq // ATT_TK), grid=(batch, nq),
        in_specs=[pl.BlockSpec((MLA_HEADS, ATT_TQ, QHEAD), lambda b, i: (0, b * nq + i, 0)),
                  pl.BlockSpec((seq, KCAT), lambda b, i: (b, 0))],
        out_specs=pl.BlockSpec((ATT_TQ, MLA_HEADS * MLA_KV_RANK), lambda b, i: (b * nq + i, 0)),
        out_shape=jax.ShapeDtypeStruct((n, MLA_HEADS * MLA_KV_RANK), BF16),
        scratch_shapes=[pltpu.VMEM((MLA_HEADS * ATT_TQ, ATT_TK), jnp.int32)],
        compiler_params=_cparams(("arbitrary", "arbitrary")),
    )(q, kcat)


def _mla_sample_kernel(pt_ref, q_ref, kn_ref, ckv_hbm, krt_hbm, o_ref,
                       ckv_buf, krt_buf, sem, m_ref, l_ref, acc_ref,
                       *, n_new, layer, n_seq, n_chunks):
    npg = PAGES_PER_STEP
    b = pl.program_id(0)
    c = pl.program_id(1)
    step = b * n_chunks + c
    n_steps = n_seq * n_chunks
    slot = step % SAMPLE_SLOTS
    ahead = SAMPLE_SLOTS - 1

    def page_copies(bb, cc, sl):
        copies = []
        for j in range(npg):
            pid = 0 if bb is None else pt_ref[bb, cc * npg + j]
            copies.append(pltpu.make_async_copy(
                ckv_hbm.at[layer, pid], ckv_buf.at[sl, pl.ds(j * PAGE_SIZE, PAGE_SIZE), :],
                sem.at[sl, 0]))
            copies.append(pltpu.make_async_copy(
                krt_hbm.at[layer, pid], krt_buf.at[sl, j], sem.at[sl, 1]))
        return copies

    def start_chunk(t):
        @pl.when(t < n_steps)
        def _():
            for cp in page_copies(t // n_chunks, t % n_chunks, t % SAMPLE_SLOTS):
                cp.start()

    @pl.when(step == 0)
    def _():
        for t in range(ahead):
            start_chunk(t)

    start_chunk(step + ahead)

    for cp in page_copies(None, None, slot):
        cp.wait()

    @pl.when(c == 0)
    def _():
        m_ref[...] = jnp.full_like(m_ref, -jnp.inf)
        l_ref[...] = jnp.zeros_like(l_ref)
        acc_ref[...] = jnp.zeros_like(acc_ref)

    q = q_ref[...]
    q_lat = q[:, :MLA_KV_RANK]
    q_rope = q[:, MLA_KV_RANK:MLA_KV_RANK + MLA_ROPE_DIM]
    nt = (((1,), (1,)), ((), ()))

    def update(stats, s, v):
        m, l, acc = stats
        m_new = jnp.maximum(m, jnp.max(s, -1, keepdims=True))
        a = jnp.exp2(m - m_new)
        pr = jnp.exp2(s - m_new)
        return (m_new, a * l + jnp.sum(pr, -1, keepdims=True),
                a * acc + jnp.dot(pr.astype(BF16), v, preferred_element_type=F32))

    groups = range(SAMPLE_GROUPS)
    stats = [(m_ref[g], l_ref[g], acc_ref[g]) for g in groups]
    ppg = npg // SAMPLE_GROUPS
    ks = [ckv_buf[slot, pl.ds(g * ppg * PAGE_SIZE, ppg * PAGE_SIZE), :].astype(BF16) for g in groups]
    krs = [jnp.concatenate([krt_buf[slot, g * ppg + j] for j in range(ppg)], axis=1).astype(BF16)
           for g in groups]
    ss = [lax.dot_general(q_lat, ks[g], nt, preferred_element_type=F32)
          + jnp.dot(q_rope, krs[g], preferred_element_type=F32) for g in groups]
    m_new = [jnp.maximum(stats[g][0], jnp.max(ss[g], -1, keepdims=True)) for g in groups]
    prs = [jnp.exp2(ss[g] - m_new[g]) for g in groups]
    pvs = [jnp.dot(prs[g].astype(BF16), ks[g], preferred_element_type=F32) for g in groups]
    for g in groups:
        a = jnp.exp2(stats[g][0] - m_new[g])
        stats[g] = (m_new[g], a * stats[g][1] + jnp.sum(prs[g], -1, keepdims=True),
                    a * stats[g][2] + pvs[g])
        m_ref[g], l_ref[g], acc_ref[g] = stats[g]

    @pl.when(c == pl.num_programs(1) - 1)
    def _():
        kn = kn_ref[...]
        sn = lax.dot_general(q, kn, nt, preferred_element_type=F32)
        t_row = lax.broadcasted_iota(jnp.int32, sn.shape, 0) % n_new
        sn = jnp.where(lax.broadcasted_iota(jnp.int32, sn.shape, 1) <= t_row, sn, -jnp.inf)
        final = list(stats)
        final[0] = update(final[0], sn, kn[:, :MLA_KV_RANK])
        m_all = functools.reduce(jnp.maximum, [st[0] for st in final])
        l = jnp.zeros_like(m_all)
        acc = jnp.zeros(acc_ref.shape[1:], F32)
        for m_g, l_g, acc_g in final:
            wg = jnp.exp2(m_g - m_all)
            l = l + wg * l_g
            acc = acc + wg * acc_g
        o_ref[...] = (acc / l).astype(BF16)


def _mla_sample(q, k_new, cache_ckv, cache_krope_t, page_table, layer):
    bd, rows, _ = q.shape
    n_new = k_new.shape[1]
    k_new = jnp.pad(k_new, ((0, 0), (0, NEW_PAD - n_new), (0, 0)))
    n_pages = page_table.shape[1]
    npg = PAGES_PER_STEP
    assert n_pages % npg == 0
    n_chunks = n_pages // npg
    hbm = pl.BlockSpec(memory_space=pl.ANY)
    in_specs = [pl.BlockSpec((None, rows, QHEAD), lambda b, c, pt: (b, 0, 0)),
                pl.BlockSpec((None, NEW_PAD, KCAT), lambda b, c, pt: (b, 0, 0)), hbm, hbm]
    grid_spec = pltpu.PrefetchScalarGridSpec(
        num_scalar_prefetch=1, grid=(bd, n_chunks), in_specs=in_specs,
        out_specs=pl.BlockSpec((None, rows, MLA_KV_RANK), lambda b, c, pt: (b, 0, 0)),
        scratch_shapes=[pltpu.VMEM((SAMPLE_SLOTS, npg * PAGE_SIZE, MLA_KV_RANK), F32),
                        pltpu.VMEM((SAMPLE_SLOTS, npg, MLA_ROPE_DIM, PAGE_SIZE), F32),
                        pltpu.SemaphoreType.DMA((SAMPLE_SLOTS, 2)),
                        pltpu.VMEM((SAMPLE_GROUPS, rows, 1), F32),
                        pltpu.VMEM((SAMPLE_GROUPS, rows, 1), F32),
                        pltpu.VMEM((SAMPLE_GROUPS, rows, MLA_KV_RANK), F32)])
    return pl.pallas_call(
        functools.partial(_mla_sample_kernel, n_new=n_new, layer=layer, n_seq=bd,
                          n_chunks=n_chunks),
        grid_spec=grid_spec, out_shape=jax.ShapeDtypeStruct((bd, rows, MLA_KV_RANK), BF16),
        compiler_params=_cparams(("arbitrary", "arbitrary")),
    )(page_table, q, k_new, cache_ckv, cache_krope_t)


def _prev_rows(x, before, shift):
    if shift == 1:
        row = lax.broadcasted_iota(jnp.int32, x.shape, 0)
        return jnp.where(row == 0, before, pltpu.roll(x, 1, 0))
    return jnp.concatenate([before, x[:x.shape[0] - shift]], axis=0)


def _mix_prep_kernel(pr_ref, pc_ref, sh0_ref, cv0_ref, mu_ref, w0_ref, a0_ref, wdec_ref, wiclr_ref,
                     wgate_ref, cw_ref,
                     r_ref, k_ref, v_ref, w_ref, a_ref, g_ref, co_ref, cu_ref,
                     sh_c, cu1_c, cu2_c, *, shift, carry):
    first = pl.program_id(1) == 0

    if carry:
        @pl.when(first)
        def _():
            sh_c[...] = sh0_ref[...]
            cu1_c[...] = cv0_ref[1]
            cu2_c[...] = cv0_ref[0]
        sh_before, cu1_before, cu2_before = sh_c[...], cu1_c[...], cu2_c[...]
    else:
        sh_before, cu1_before, cu2_before = sh0_ref[...], cv0_ref[1], cv0_ref[0]

    p = pr_ref[...]
    xs = p + (_prev_rows(p, sh_before, shift) - p) * mu_ref[...]
    r_ref[...] = xs[:, :RWKV_DIM]
    k_ref[...] = xs[:, RWKV_DIM:2 * RWKV_DIM]
    v_ref[...] = xs[:, 2 * RWKV_DIM:3 * RWKV_DIM]
    lora = xs[:, 3 * RWKV_DIM:3 * RWKV_DIM + LANES]
    z = w0_ref[...] + jnp.dot(jnp.tanh(lora).astype(BF16), wdec_ref[...], preferred_element_type=F32)
    nz = -z
    softplus = jnp.maximum(nz, 0.0) + jnp.log(1.0 + jnp.exp(-jnp.abs(nz)))
    w_ref[...] = jnp.exp(-jnp.exp(-softplus - 0.5))
    a_ref[...] = _sigmoid(a0_ref[...] + jnp.dot(lora.astype(BF16), wiclr_ref[...],
                                                preferred_element_type=F32))
    xg = xs[:, 3 * RWKV_DIM + LANES:]
    g_ref[...] = jnp.dot(_sigmoid(xg).astype(BF16), wgate_ref[...], preferred_element_type=F32)

    pc = pc_ref[...]
    cu = pc[:, CONV_DIM:2 * CONV_DIM] * pc[:, 2 * CONV_DIM:]
    cu1 = _prev_rows(cu, cu1_before, shift)
    cu2 = _prev_rows(cu1, cu2_before, shift)
    cw = cw_ref[...]
    co_ref[...] = (pc[:, :CONV_DIM] * (cw[0:1] * cu2 + cw[1:2] * cu1 + cw[2:3] * cu)).astype(BF16)
    cu_ref[...] = cu

    if carry:
        tm = p.shape[0]
        sh_c[...] = p[tm - 1:tm]
        cu1_c[...] = cu[tm - 1:tm]
        cu2_c[...] = cu1[tm - 1:tm]


def _mix_prep(p_r, p_c, shift0, conv0, mu, w0, a0, wdec_p, wiclr_p, wgate, conv_w, *, n_seq, tm, shift):
    n = p_r.shape[0]
    carry = shift == 1
    tps = n // n_seq // tm if carry else 1
    grid = (n_seq, tps) if carry else (1, 1)
    row = lambda c: pl.BlockSpec((tm, c), lambda b, t: (b * tps + t, 0))
    full = lambda a: pl.BlockSpec(a.shape, lambda b, t: (0,) * a.ndim)
    if carry:
        sh_spec = pl.BlockSpec((None, 1, RWKV_PROJ), lambda b, t: (b, 0, 0))
        cv_spec = pl.BlockSpec((None, 2, 1, CONV_DIM), lambda b, t: (b, 0, 0, 0))
        shift0 = shift0.reshape(n_seq, 1, RWKV_PROJ)
        conv0 = conv0.reshape(n_seq, 2, 1, CONV_DIM)
    else:
        sh_spec = full(shift0)
        conv0 = jnp.swapaxes(conv0, 0, 1)
        cv_spec = full(conv0)
    vec = lambda a: a.reshape(1, -1)
    args = (p_r, p_c, shift0, conv0, vec(mu), vec(w0), vec(a0), wdec_p, wiclr_p, wgate, conv_w)
    in_specs = [row(RWKV_PROJ), row(3 * CONV_DIM), sh_spec, cv_spec] + [full(a) for a in args[4:]]
    f32o = jax.ShapeDtypeStruct((n, RWKV_DIM), F32)
    out_shape = (f32o,) * 6 + (jax.ShapeDtypeStruct((n, CONV_DIM), BF16), f32o)
    rows_before = shift
    return pl.pallas_call(
        functools.partial(_mix_prep_kernel, shift=shift, carry=carry), grid=grid,
        in_specs=in_specs, out_specs=(row(RWKV_DIM),) * 8, out_shape=out_shape,
        scratch_shapes=[pltpu.VMEM((rows_before, RWKV_PROJ), F32),
                        pltpu.VMEM((rows_before, CONV_DIM), F32),
                        pltpu.VMEM((rows_before, CONV_DIM), F32)],
        compiler_params=_cparams(("parallel", "arbitrary")),
    )(*args)


def _wkv_kernel(r_ref, k_ref, v_ref, w_ref, a_ref, kk_ref, ka_ref, rk_ref, gn_ref, bn_ref, s0_ref,
                y_ref, sout_ref, s_ref, al_ref, wr_ref, be_ref, km_ref, wd_ref):
    tb = pl.program_id(1)
    hd = RWKV_HEAD_DIM

    @pl.when(tb == 0)
    def _():
        s_ref[...] = s0_ref[...]

    kk_t, ka_t, rk_t = kk_ref[...], ka_ref[...], rk_ref[...]
    gn_t, bn_t = gn_ref[...], bn_ref[...]

    def step(t, _):
        r, k, v, w, a = r_ref[t], k_ref[t], v_ref[t], w_ref[t], a_ref[t]
        kk = k * kk_t
        kk = kk / jnp.maximum(jnp.sqrt(jnp.sum(kk * kk, 0, keepdims=True)), 1e-12)
        km = k * (1.0 + (a - 1.0) * ka_t)
        be = kk * a
        al_ref[...] = -kk
        wr_ref[...] = w * r
        be_ref[...] = be
        km_ref[...] = km
        wd_ref[...] = w
        sa = jnp.zeros((hd, LANES), F32)
        y0 = jnp.zeros((hd, LANES), F32)
        for i in range(hd):
            s_i = s_ref[i]
            sa = sa + s_i * al_ref[i:i + 1, :]
            y0 = y0 + s_i * wr_ref[i:i + 1, :]
        br = jnp.sum(be * r, 0, keepdims=True)
        kr = jnp.sum(km * r, 0, keepdims=True)
        y = y0 + sa * br + v * kr
        for i in range(hd):
            s_ref[i] = s_ref[i] * wd_ref[i:i + 1, :] + sa * be_ref[i:i + 1, :] + v * km_ref[i:i + 1, :]
        mean = jnp.mean(y, 0, keepdims=True)
        yc = y - mean
        var = jnp.mean(yc * yc, 0, keepdims=True)
        bonus = jnp.sum(r * km * rk_t, 0, keepdims=True) * v
        y_ref[t] = yc * lax.rsqrt(var + GN_EPS) * gn_t + bn_t + bonus
        return 0

    lax.fori_loop(0, r_ref.shape[0], step, 0)

    @pl.when(tb == pl.num_programs(1) - 1)
    def _():
        sout_ref[...] = s_ref[...]


def _wkv(r, k, v, w, a, kk_t, ka_t, rk_t, gn_t, bn_t, s0, tt):
    t_len, hd, lanes = r.shape
    seq = pl.BlockSpec((tt, hd, LANES), lambda g, t: (t, 0, g))
    par = pl.BlockSpec((hd, LANES), lambda g, t: (0, g))
    st = pl.BlockSpec((hd, hd, LANES), lambda g, t: (0, 0, g))
    return pl.pallas_call(
        _wkv_kernel, grid=(lanes // LANES, t_len // tt),
        in_specs=[seq] * 5 + [par] * 5 + [st], out_specs=(seq, st),
        out_shape=(jax.ShapeDtypeStruct((t_len, hd, lanes), F32),
                   jax.ShapeDtypeStruct((hd, hd, lanes), F32)),
        scratch_shapes=[pltpu.VMEM((hd, hd, LANES), F32)] + [pltpu.VMEM((hd, LANES), F32)] * 5,
        compiler_params=_cparams(("parallel", "arbitrary")),
    )(r, k, v, w, a, kk_t, ka_t, rk_t, gn_t, bn_t, s0)


WKV_CHUNK = 64
WKV_BLOCK = 512


def _bf16_parts(x, n):
    parts = []
    for _ in range(n):
        p = x.astype(BF16)
        parts.append(p)
        x = x - p.astype(F32)
    return parts


def _dot(a, b):
    return jnp.dot(a.astype(BF16), b.astype(BF16), preferred_element_type=F32)


def _dot_nt(a, b):
    return lax.dot_general(a.astype(BF16), b.astype(BF16), (((1,), (1,)), ((), ())),
                           preferred_element_type=F32)


def _wkv_chunk_kernel(r_ref, k_ref, v_ref, w_ref, a_ref, par_ref, s0_ref, y_ref, sout_ref, st_ref):
    c_len = WKV_CHUNK
    width = RWKV_DIM
    shift = RWKV_HEAD_DIM.bit_length() - 1

    @pl.when(pl.program_id(1) == 0)
    def _():
        st_ref[...] = s0_ref[...]

    row = lax.broadcasted_iota(jnp.int32, (width, width), 0)
    col = lax.broadcasted_iota(jnp.int32, (width, width), 1)
    same_head = (row >> shift) == (col >> shift)
    eye = row == col
    head_ones = jnp.where(same_head, 1.0, 0.0).astype(BF16)
    t_idx = lax.broadcasted_iota(jnp.int32, (c_len, width), 0)
    s_idx = lax.broadcasted_iota(jnp.int32, (c_len, width), 1) & (c_len - 1)
    strict, incl = s_idx < t_idx, s_idx <= t_idx

    def head_sum(x):
        return jnp.dot(x.astype(BF16), head_ones, preferred_element_type=F32)

    def blockdiag(z):
        return jnp.where(same_head, jnp.concatenate([z] * RWKV_HEADS, axis=0), 0.0)

    par = par_ref[...]
    k_k, k_a, r_k, gn, bn = (par[i:i + 1] for i in range(5))
    r_all, k_all, v_all, w_all, a_all = r_ref[...], k_ref[...], v_ref[...], w_ref[...], a_ref[...]
    kk_all = k_all * k_k
    kk_all = kk_all / jnp.maximum(jnp.sqrt(head_sum(kk_all * kk_all)), 1e-12)
    km_all = k_all * (1.0 + (a_all - 1.0) * k_a)
    logw_all = jnp.log(w_all)
    bonus_all = head_sum(r_all * km_all * r_k) * v_all

    t_blk = r_all.shape[0]
    n_chunks = t_blk // c_len
    t_row = lax.broadcasted_iota(jnp.int32, (t_blk, t_blk), 0)
    t_col = lax.broadcasted_iota(jnp.int32, (t_blk, t_blk), 1)
    tri = jnp.where(((t_row >> shift) == (t_col >> shift)) & (t_col <= t_row), 1.0, 0.0).astype(BF16)
    logp = sum(jnp.dot(tri, p, preferred_element_type=F32) for p in _bf16_parts(logw_all, 3))
    p_inc, p_inv, p_prev = jnp.exp(logp), jnp.exp(-logp), jnp.exp(logp - logw_all)
    at_all, rt_all = -kk_all * p_prev, r_all * p_inc
    bt_all, kt_all = kk_all * a_all * p_inv, km_all * p_inv

    chunks = range(n_chunks)
    rows = [slice(c * c_len, (c + 1) * c_len) for c in chunks]
    p_end = [p_inc[(c + 1) * c_len - 1:(c + 1) * c_len] for c in chunks]
    at, rt, bt, kt = ([x[sl] for sl in rows] for x in (at_all, rt_all, bt_all, kt_all))
    bt_bd = [blockdiag(x) for x in bt]
    kt_bd = [blockdiag(x) for x in kt]
    v_bd = [blockdiag(v_all[sl]) for sl in rows]
    at_bd = [blockdiag(x) for x in at]
    a_ab = [jnp.where(strict, _dot_nt(at[c], bt_bd[c]), 0.0) for c in chunks]
    a_ak = [jnp.where(strict, _dot_nt(at[c], kt_bd[c]), 0.0) for c in chunks]
    a_rb = [jnp.where(incl, _dot_nt(rt[c], bt_bd[c]), 0.0) for c in chunks]
    a_rk = [jnp.where(incl, _dot_nt(rt[c], kt_bd[c]), 0.0) for c in chunks]
    lp = [blockdiag(x) for x in a_ab]
    ident = jnp.where(eye, 1.0, 0.0)
    tinv = [ident + x for x in lp]
    for _ in range(c_len.bit_length() - 2):
        lp = [_dot(x, x) for x in lp]
        tinv = [tinv[c] + _dot(tinv[c], lp[c]) for c in chunks]
    g_bd = [_dot(blockdiag(a_ak[c]), v_bd[c]) for c in chunks]
    w_bd = [_dot(tinv[c], at_bd[c]) for c in chunks]
    u0_bd = [_dot(tinv[c], g_bd[c]) for c in chunks]
    q = [rt[c] + _dot(a_rb[c], w_bd[c]) for c in chunks]
    y0 = [_dot(a_rb[c], u0_bd[c]) + _dot(a_rk[c], v_bd[c]) for c in chunks]
    bs_t = [blockdiag(bt[c] * p_end[c]).T for c in chunks]
    ks_t = [blockdiag(kt[c] * p_end[c]).T for c in chunks]
    m = [jnp.where(eye, p_end[c], 0.0) + _dot(bs_t[c], w_bd[c]) for c in chunks]
    n0 = [_dot(bs_t[c], u0_bd[c]) + _dot(ks_t[c], v_bd[c]) for c in chunks]

    st = st_ref[...]
    ys = []
    for c in chunks:
        ys.append(_dot(q[c], st) + y0[c])
        st = _dot(m[c], st) + n0[c]
    st_ref[...] = st

    y = jnp.concatenate(ys, axis=0)
    mean = head_sum(y) * (1.0 / RWKV_HEAD_DIM)
    yc = y - mean
    var = head_sum(yc * yc) * (1.0 / RWKV_HEAD_DIM)
    y_ref[...] = yc * lax.rsqrt(var + GN_EPS) * gn + bn + bonus_all

    @pl.when(pl.program_id(1) == pl.num_programs(1) - 1)
    def _():
        sout_ref[...] = st_ref[...]


def _wkv_chunked(r, k, v, w, a, par, s0_bd, n_seq):
    n, width = r.shape
    nblk = n // n_seq // WKV_BLOCK
    seq = pl.BlockSpec((WKV_BLOCK, width), lambda b, j: (b * nblk + j, 0))
    st = pl.BlockSpec((None, width, width), lambda b, j: (b, 0, 0))
    return pl.pallas_call(
        _wkv_chunk_kernel, grid=(n_seq, nblk),
        in_specs=[seq] * 5 + [pl.BlockSpec(par.shape, lambda b, j: (0, 0)), st],
        out_specs=(seq, st),
        out_shape=(jax.ShapeDtypeStruct((n, width), F32),
                   jax.ShapeDtypeStruct((n_seq, width, width), F32)),
        scratch_shapes=[pltpu.VMEM((width, width), F32)],
        compiler_params=_cparams(("parallel", "arbitrary")),
    )(r, k, v, w, a, par, s0_bd)


def _mix_out_kernel(h_ref, o_ref, y_ref, g_ref, co_ref, wo_ref, g1_ref, b1_ref, wr_ref, br_ref,
                    x_ref, xp_ref, gate_ref, route_ref, cnt_ref):
    mix = jnp.concatenate(
        [o_ref[...], (y_ref[...] * g_ref[...]).astype(BF16), co_ref[...]], axis=-1)
    pre = ALPHA * h_ref[...] + jnp.dot(mix, wo_ref[...], preferred_element_type=F32)
    x = _ln_rows(pre, g1_ref[...], b1_ref[...])
    x_ref[...] = x
    xb = x.astype(BF16)
    _pack_row_halves(x, xp_ref)

    logits = jnp.dot(xb, wr_ref[...], preferred_element_type=F32) + br_ref[...]
    lane = lax.broadcasted_iota(jnp.int32, logits.shape, 1)
    ninf = -jnp.inf
    lg = jnp.where(lane < N_GROUPS, logits, ninf)
    mg = jnp.max(lg, -1, keepdims=True)
    g_p = 1.0 / jnp.sum(jnp.exp(lg - mg), -1, keepdims=True)
    g_idx = jnp.min(jnp.where(lg == mg, lane, LANES), -1, keepdims=True)
    lo = N_GROUPS + EXPERTS_PER_GROUP * g_idx
    le = jnp.where((lane >= lo) & (lane < lo + EXPERTS_PER_GROUP), logits, ninf)
    m1 = jnp.max(le, -1, keepdims=True)
    i1 = jnp.min(jnp.where(le == m1, lane, LANES), -1, keepdims=True)
    le2 = jnp.where(lane == i1, ninf, le)
    m2 = jnp.max(le2, -1, keepdims=True)
    i2 = jnp.min(jnp.where(le2 == m2, lane, LANES), -1, keepdims=True)
    e2 = jnp.exp(m2 - m1)
    gate1 = g_p / (1.0 + e2)
    gate2 = g_p * e2 / (1.0 + e2)
    gate_ref[...] = jnp.where(lane == 0, gate1, jnp.where(lane == 1, gate2, 0.0))

    pick1, pick2 = lane == i1, lane == i2
    picks = jnp.where(pick1 | pick2, 1.0, 0.0)
    tm = logits.shape[0]
    tri = jnp.where(lax.broadcasted_iota(jnp.int32, (tm, tm), 1)
                    < lax.broadcasted_iota(jnp.int32, (tm, tm), 0), 1.0, 0.0).astype(BF16)
    before = jnp.dot(tri, picks.astype(BF16), preferred_element_type=F32)
    rank1 = jnp.sum(jnp.where(pick1, before, 0.0), -1, keepdims=True).astype(jnp.int32)
    rank2 = jnp.sum(jnp.where(pick2, before, 0.0), -1, keepdims=True).astype(jnp.int32)
    route_ref[...] = jnp.where(lane == 0, i1, jnp.where(lane == 1, i2, jnp.where(
        lane == 2, rank1, jnp.where(lane == 3, rank2, 0))))
    cnt_ref[...] = jnp.sum(picks, 0, keepdims=True).astype(jnp.int32)


def _moe_pos_kernel(route_ref, base_ref, pos_ref):
    route = route_ref[...]
    base = base_ref[...]
    lane = lax.broadcasted_iota(jnp.int32, route.shape, 1)
    pos = []
    for c in range(2):
        start = jnp.sum(jnp.where(lane == route[:, c:c + 1], base, 0), -1, keepdims=True)
        pos.append(start + route[:, 2 + c:3 + c])
    pos_ref[...] = jnp.where(lane == 0, pos[0], jnp.where(lane == 1, pos[1], 0))


def _moe_pos(route, base, tm):
    n = route.shape[0]
    row = pl.BlockSpec((tm, LANES), lambda i: (i, 0))
    return pl.pallas_call(
        _moe_pos_kernel, grid=(n // tm,),
        in_specs=[row, pl.BlockSpec((None, 1, LANES), lambda i: (i, 0, 0))], out_specs=row,
        out_shape=jax.ShapeDtypeStruct((n, LANES), jnp.int32),
        compiler_params=_cparams(("parallel",)),
    )(route, base)


def _mix_out(h, o_lat, y, g, c_out, w_o, ln_g, ln_b, w_router, b_router, tm):
    n, d = h.shape
    row = lambda c: pl.BlockSpec((tm, c), lambda i: (i, 0))
    full = lambda a: pl.BlockSpec(a.shape, lambda i: (0,) * a.ndim)
    vec = lambda a: a.reshape(1, -1)
    args = (h, o_lat, y, g, c_out, w_o, vec(ln_g), vec(ln_b), w_router, b_router)
    in_specs = [row(d), row(o_lat.shape[1]), row(RWKV_DIM), row(RWKV_DIM), row(CONV_DIM)] + [
        full(a) for a in args[5:]]
    out_shape = (jax.ShapeDtypeStruct((n, d), F32), jax.ShapeDtypeStruct((2, n, d // 4), jnp.int32),
                 jax.ShapeDtypeStruct((n, LANES), F32), jax.ShapeDtypeStruct((n, LANES), jnp.int32),
                 jax.ShapeDtypeStruct((n // tm, 1, LANES), jnp.int32))
    return pl.pallas_call(
        _mix_out_kernel, grid=(n // tm,), in_specs=in_specs,
        out_specs=(row(d), pl.BlockSpec((2, tm, d // 4), lambda i: (0, i, 0)), row(LANES), row(LANES),
                   pl.BlockSpec((None, 1, LANES), lambda i: (i, 0, 0))),
        out_shape=out_shape, compiler_params=_cparams(("parallel",)),
    )(*args)


def _moe_kernel(be_ref, nv_ref, x0_ref, x1_ref, wgu_ref, wd_ref, o_ref):
    valid = nv_ref[pl.program_id(0)]

    @pl.when(valid > 0)
    def _():
        keep = lax.broadcasted_iota(jnp.int32, x0_ref.shape, 0) < valid
        x = _unpack_row_halves(jnp.where(keep, x0_ref[...], 0),
                               jnp.where(keep, x1_ref[...], 0)).astype(BF16)
        gu = jnp.dot(x, wgu_ref[...], preferred_element_type=F32)
        hg = gu[:, :EXPERT_FF]
        hid = hg * _sigmoid(hg) * gu[:, EXPERT_FF:]
        _pack_row_halves(jnp.dot(hid.astype(BF16), wd_ref[...], preferred_element_type=F32), o_ref)

    @pl.when(valid == 0)
    def _():
        o_ref[...] = jnp.zeros_like(o_ref)


def _moe_experts(blk_e, blk_valid, xs, cap, w_gu, w_d):
    quarter = xs.shape[1]
    d = 4 * quarter
    nb = cap // MOE_ROWS
    grid_spec = pltpu.PrefetchScalarGridSpec(
        num_scalar_prefetch=2, grid=(nb,),
        in_specs=[pl.BlockSpec((MOE_ROWS, quarter), lambda i, be, nv: (i, 0)),
                  pl.BlockSpec((MOE_ROWS, quarter), lambda i, be, nv: (i + nb, 0)),
                  pl.BlockSpec((None, d, 2 * EXPERT_FF), lambda i, be, nv: (be[i], 0, 0)),
                  pl.BlockSpec((None, EXPERT_FF, d), lambda i, be, nv: (be[i], 0, 0))],
        out_specs=pl.BlockSpec((2, MOE_ROWS, quarter), lambda i, be, nv: (0, i, 0)))
    return pl.pallas_call(
        _moe_kernel, grid_spec=grid_spec,
        out_shape=jax.ShapeDtypeStruct((2, cap, quarter), jnp.int32),
        compiler_params=_cparams(("arbitrary",)),
    )(blk_e, blk_valid, xs, xs, w_gu, w_d)


def _sc_mesh():
    return plsc.VectorSubcoreMesh(core_axis_name="core", subcore_axis_name="subcore")


SC_WINDOW = 128


def _sc_rows(n):
    mesh = _sc_mesh()
    unit = SC_WINDOW * mesh.num_cores * mesh.num_subcores
    return -(-n // unit) * unit


def _scatter_rows2(x, idx1, idx2, n_out):
    n, w = x.shape
    mesh = _sc_mesh()
    win = SC_WINDOW
    assert n == _sc_rows(n)

    @pl.kernel(out_type=jax.ShapeDtypeStruct((n_out, w), x.dtype), mesh=mesh, scratch_types=[])
    def scatter(x_hbm, i1_hbm, i2_hbm, o_hbm):
        def body(x_vmem, i1_vmem, i2_vmem):
            pltpu.sync_copy(x_vmem, o_hbm.at[i1_vmem.at[0]])
            pltpu.sync_copy(x_vmem, o_hbm.at[i2_vmem.at[0]])

        idx_spec = pl.BlockSpec((1, win), lambda i: (0, i))
        pltpu.emit_pipeline(
            body, grid=(n // win,),
            in_specs=[pl.BlockSpec((win, w), lambda i: (i, 0)), idx_spec, idx_spec],
            out_specs=[], core_axis_name=('core', 'subcore'),
            dimension_semantics=(pltpu.PARALLEL,),
        )(x_hbm, i1_hbm, i2_hbm)

    return scatter(x, idx1.reshape(1, n), idx2.reshape(1, n))


def _gather_rows(x, idx):
    n = idx.shape[0]
    w = x.shape[1]
    mesh = _sc_mesh()
    win = SC_WINDOW
    assert n == _sc_rows(n)

    @pl.kernel(out_type=jax.ShapeDtypeStruct((n, w), x.dtype), mesh=mesh, scratch_types=[])
    def gather(x_hbm, i_hbm, o_hbm):
        def body(i_vmem, o_vmem):
            pltpu.sync_copy(x_hbm.at[i_vmem.at[0]], o_vmem)

        pltpu.emit_pipeline(
            body, grid=(n // win,),
            in_specs=[pl.BlockSpec((1, win), lambda i: (0, i))],
            out_specs=[pl.BlockSpec((win, w), lambda i: (i, 0))],
            core_axis_name=('core', 'subcore'), dimension_semantics=(pltpu.PARALLEL,),
        )(i_hbm, o_hbm)

    return gather(x, idx.reshape(1, n))


def _moe_layout(tile_counts, n_tok):
    cnt = tile_counts[:, 0, N_GROUPS:N_GROUPS + N_EXPERTS]
    tile_off = jnp.cumsum(cnt, axis=0) - cnt
    total = jnp.sum(cnt, axis=0)
    pcounts = (total + MOE_ROWS - 1) // MOE_ROWS * MOE_ROWS
    pends = jnp.cumsum(pcounts)
    base = (pends - pcounts)[None, :] + tile_off
    base = jnp.pad(base, ((0, 0), (N_GROUPS, LANES - N_GROUPS - N_EXPERTS)))[:, None, :]
    n_blocks = 2 * n_tok // MOE_ROWS + N_EXPERTS
    blk_start = jnp.arange(n_blocks, dtype=jnp.int32) * MOE_ROWS
    blk_e = jnp.minimum(jnp.sum((pends[None, :] <= blk_start[:, None]).astype(jnp.int32), axis=1),
                        N_EXPERTS - 1)
    blk_valid = jnp.clip((pends - pcounts + total)[blk_e] - blk_start, 0, MOE_ROWS)
    return (base.astype(jnp.int32), blk_e.astype(jnp.int32), blk_valid.astype(jnp.int32),
            n_blocks * MOE_ROWS)


def _rope_tables(pos):
    half = MLA_ROPE_DIM // 2
    freqs = ROPE_THETA ** (-jnp.arange(half, dtype=F32) / half)
    ang = pos.astype(F32)[:, None] * freqs
    cos, sin = jnp.cos(ang), jnp.sin(ang)
    pad = jnp.zeros((pos.shape[0], LANES - MLA_ROPE_DIM), F32)
    return (jnp.concatenate([cos, cos, pad], -1), jnp.concatenate([-sin, sin, pad], -1))


def _swap_halves(w):
    half = MLA_ROPE_DIM // 2
    return jnp.concatenate([w[..., half:], w[..., :half]], -1)


def _layer_params(l, w_in, g_qn, w_uq, g_kvn, w_uk, w_uv, mu_shift, w0, w_decay, a0, w_iclr,
                  w_gate_out, k_k, k_a, r_k, lnx_g, lnx_b, conv_w, w_out, ln1_g, ln1_b,
                  w_group_router, b_group_router, w_expert_router, b_expert_router,
                  w_exp_gate, w_exp_up, w_exp_down, ln2_g, ln2_b):
    d = w_in.shape[1]
    hi = lax.Precision.HIGHEST
    wi = w_in[l]
    zc = lambda c: jnp.zeros((d, c), F32)
    c0 = MLA_Q_RANK
    c1 = c0 + MLA_KV_RANK
    c2 = c1 + MLA_ROPE_DIM
    w_kr = wi[:, c1:c2]
    w_in_p = jnp.concatenate([
        wi[:, :c0], zc(QPAD - MLA_Q_RANK), wi[:, c0:c1],
        w_kr, _swap_halves(w_kr), zc(LANES - 2 * MLA_ROPE_DIM), wi[:, c2:]], -1).astype(BF16)
    gq_p = jnp.concatenate([g_qn[l], jnp.zeros((QPAD - MLA_Q_RANK,), F32)]).reshape(1, QPAD)
    uq = w_uq[l]
    q_lat = jnp.einsum('rhn,chn->rhc', uq[..., :MLA_NOPE_DIM], w_uk[l], precision=hi)
    q_rope = uq[..., MLA_NOPE_DIM:]
    wq = jnp.concatenate([q_lat, q_rope, _swap_halves(q_rope),
                          jnp.zeros((MLA_Q_RANK, MLA_HEADS, QHEAD - LANES - 2 * MLA_ROPE_DIM), F32)], -1)
    wq = (wq * (ATTN_SCALE * LOG2E)).reshape(MLA_Q_RANK, MLA_HEADS * QHEAD)
    wq = jnp.concatenate([wq, jnp.zeros((QPAD - MLA_Q_RANK, MLA_HEADS * QHEAD), F32)], 0).astype(BF16)
    wo = w_out[l]
    mla_dim = MLA_HEADS * MLA_V_DIM
    wo_att = jnp.einsum('chv,hvd->hcd', w_uv[l], wo[:mla_dim].reshape(MLA_HEADS, MLA_V_DIM, -1),
                        precision=hi).reshape(MLA_HEADS * MLA_KV_RANK, -1)
    w_o = jnp.concatenate([wo_att, wo[mla_dim:]], 0).astype(BF16)
    z64 = jnp.zeros((64, RWKV_DIM), F32)
    wdec_p = jnp.concatenate([w_decay[l], z64], 0).astype(BF16)
    wiclr_p = jnp.concatenate([z64, w_iclr[l]], 0).astype(BF16)
    w_router = jnp.concatenate(
        [w_group_router[l], w_expert_router[l],
         jnp.zeros((d, LANES - N_GROUPS - N_EXPERTS), F32)], -1).astype(BF16)
    b_router = jnp.concatenate(
        [b_group_router[l], b_expert_router[l],
         jnp.zeros((LANES - N_GROUPS - N_EXPERTS,), F32)]).reshape(1, LANES)
    w_gu = jnp.concatenate([w_exp_gate[l], w_exp_up[l]], -1).astype(BF16)
    return dict(
        w_in_p=w_in_p, gq_p=gq_p, wq=wq, g_kvn=g_kvn[l], mu=mu_shift[l], w0=w0[l], a0=a0[l],
        wdec_p=wdec_p, wiclr_p=wiclr_p, wgate=w_gate_out[l].astype(BF16), conv_w=conv_w[l],
        k_k=k_k[l], k_a=k_a[l], r_k=r_k[l].reshape(-1), lnx_g=lnx_g[l], lnx_b=lnx_b[l],
        w_o=w_o, ln1_g=ln1_g[l], ln1_b=ln1_b[l], w_router=w_router, b_router=b_router,
        w_gu=w_gu, w_d=w_exp_down[l].astype(BF16), ln2_g=ln2_g[l], ln2_b=ln2_b[l])


def _head_tile(p, n_seq):
    t = p.reshape(RWKV_HEADS, RWKV_HEAD_DIM).T
    return jnp.tile(t, (1, n_seq))


def _layer(source, src, lp, *, n_seq, t_len, time_major, ctab, stab, attend, shift0, wkv0, conv0):
    n, d = src[0].shape
    tm = min(512, n)
    h, q, kcat, ckv, kr, p_r, p_c = _in_proj(source, src, lp['w_in_p'], lp['gq_p'], lp['wq'],
                                             lp['g_kvn'], ctab, stab, tm)
    o_lat = attend(q, kcat)

    r, k, v, w, a, g, c_out, cu = _mix_prep(
        p_r, p_c, shift0, conv0, lp['mu'], lp['w0'], lp['a0'], lp['wdec_p'], lp['wiclr_p'],
        lp['wgate'], lp['conv_w'], n_seq=n_seq, tm=tm, shift=n_seq if time_major else 1)

    names = ('k_k', 'k_a', 'r_k', 'lnx_g', 'lnx_b')
    if not time_major and t_len % WKV_BLOCK == 0:
        par = jnp.stack([lp[nm] for nm in names] + [jnp.zeros_like(lp['k_k'])] * 3)
        head_eye = jnp.eye(RWKV_HEADS, dtype=F32)
        s0_bd = jnp.einsum('bhvk,hg->bhkgv', wkv0, head_eye).reshape(n_seq, RWKV_DIM, RWKV_DIM)
        y, s_fin = _wkv_chunked(r, k, v, w, a, par, s0_bd, n_seq)
        wkv_new = jnp.einsum(
            'bhkgv,hg->bhvk',
            s_fin.reshape(n_seq, RWKV_HEADS, RWKV_HEAD_DIM, RWKV_HEADS, RWKV_HEAD_DIM), head_eye)
    else:
        lanes = n_seq * RWKV_HEADS

        def to_scan(x):
            if time_major:
                x = x.reshape(t_len, n_seq, RWKV_HEADS, RWKV_HEAD_DIM).transpose(0, 3, 1, 2)
            else:
                x = x.reshape(n_seq, t_len, RWKV_HEADS, RWKV_HEAD_DIM).transpose(1, 3, 0, 2)
            return x.reshape(t_len, RWKV_HEAD_DIM, lanes)

        tiles = [_head_tile(lp[nm], n_seq) for nm in names]
        s0 = wkv0.transpose(3, 2, 0, 1).reshape(RWKV_HEAD_DIM, RWKV_HEAD_DIM, lanes)
        y, s_fin = _wkv(*[to_scan(x) for x in (r, k, v, w, a)], *tiles, s0, min(16, t_len))
        y = y.reshape(t_len, RWKV_HEAD_DIM, n_seq, RWKV_HEADS)
        y = (y.transpose(0, 2, 3, 1) if time_major else y.transpose(2, 0, 3, 1)).reshape(n, RWKV_DIM)
        wkv_new = s_fin.reshape(RWKV_HEAD_DIM, RWKV_HEAD_DIM, n_seq, RWKV_HEADS).transpose(2, 3, 1, 0)

    x, xp, gate, route, tile_counts = _mix_out(
        h, o_lat, y, g, c_out, lp['w_o'], lp['ln1_g'], lp['ln1_b'], lp['w_router'], lp['b_router'], tm)
    base, blk_e, blk_valid, cap = _moe_layout(tile_counts, n)
    pos = _moe_pos(route, base, tm)
    pos1, pos2 = pos[:, 0], pos[:, 1]
    n_src = _sc_rows(2 * n)
    spare = 2 * cap + jnp.arange(n_src - 2 * n, dtype=jnp.int32)
    xp = jnp.pad(xp.reshape(2 * n, d // 4), ((0, n_src - 2 * n), (0, 0)))
    xs = _scatter_rows2(xp, jnp.concatenate([pos1, pos1 + cap, spare]),
                        jnp.concatenate([pos2, pos2 + cap, spare]), 2 * cap + n_src - 2 * n)
    yb = _moe_experts(blk_e, blk_valid, xs, cap, lp['w_gu'], lp['w_d'])
    n_dst = _sc_rows(4 * n)
    y12 = _gather_rows(yb.reshape(2 * cap, d // 4), jnp.concatenate(
        [pos1, pos1 + cap, pos2, pos2 + cap, jnp.zeros((n_dst - 4 * n,), jnp.int32)]))
    return (x, y12, gate, lp['ln2_g'], lp['ln2_b']), ckv, kr, p_r, cu, wkv_new


def kernel(x_prompt, x_sample, cache_ckv, cache_krope, state_wkv, state_shift, state_conv, page_table, ln_in_g, ln_in_b, w_in, g_qn, w_uq, g_kvn, w_uk, w_uv, mu_shift, w0, w_decay, a0, w_iclr, w_gate_out, k_k, k_a, r_k, lnx_g, lnx_b, conv_w, w_out, ln1_g, ln1_b, w_group_router, b_group_router, w_expert_router, b_expert_router, w_exp_gate, w_exp_up, w_exp_down, ln2_g, ln2_b):
    bp, sp, d = x_prompt.shape
    bd, td, _ = x_sample.shape
    past_len = page_table.shape[1] * PAGE_SIZE
    depth = w_in.shape[0]
    np_, ns = bp * sp, bd * td

    ctab_p, stab_p = _rope_tables(jnp.arange(sp))
    ctab_s, stab_s = _rope_tables(jnp.repeat(past_len + jnp.arange(td), bd))

    src_p = ('ln_in', (x_prompt.reshape(np_, d), ln_in_g, ln_in_b))
    src_s = ('ln_in', (jnp.swapaxes(x_sample, 0, 1).reshape(ns, d), ln_in_g, ln_in_b))

    krope_t = jnp.swapaxes(cache_krope, 2, 3)
    zero_shift = jnp.zeros((bp, RWKV_PROJ), F32)
    zero_wkv = jnp.zeros((bp, RWKV_HEADS, RWKV_HEAD_DIM, RWKV_HEAD_DIM), F32)
    zero_conv = jnp.zeros((bp, 2, CONV_DIM), F32)

    outs_p = [[] for _ in range(5)]
    outs_s = [[] for _ in range(5)]
    for l in range(depth):
        lp = _layer_params(l, w_in, g_qn, w_uq, g_kvn, w_uk, w_uv, mu_shift, w0, w_decay, a0,
                           w_iclr, w_gate_out, k_k, k_a, r_k, lnx_g, lnx_b, conv_w, w_out, ln1_g,
                           ln1_b, w_group_router, b_group_router, w_expert_router, b_expert_router,
                           w_exp_gate, w_exp_up, w_exp_down, ln2_g, ln2_b)

        def attend_s(q, kcat, l=l):
            qs = q.reshape(MLA_HEADS, td, bd, QHEAD).transpose(2, 0, 1, 3).reshape(
                bd, MLA_HEADS * td, QHEAD)
            kn = kcat.reshape(td, bd, KCAT).transpose(1, 0, 2)
            o = _mla_sample(qs, kn, cache_ckv, krope_t, page_table, l)
            return o.reshape(bd, MLA_HEADS, td, MLA_KV_RANK).transpose(2, 0, 1, 3).reshape(
                ns, MLA_HEADS * MLA_KV_RANK)

        pend_s, ckv, kr, p_r, cu, wkv = _layer(
            *src_s, lp, n_seq=bd, t_len=td, time_major=True, ctab=ctab_s, stab=stab_s,
            attend=attend_s, shift0=state_shift[l], wkv0=state_wkv[l], conv0=state_conv[l])
        tmaj = lambda x: jnp.swapaxes(x.reshape(td, bd, -1), 0, 1)
        outs_s[0].append(tmaj(ckv))
        outs_s[1].append(tmaj(kr))
        outs_s[2].append(wkv)
        outs_s[3].append(p_r.reshape(td, bd, RWKV_PROJ)[-1])
        outs_s[4].append(tmaj(cu)[:, -2:])
        src_s = ('ln2', pend_s)

        pend_p, ckv, kr, p_r, cu, wkv = _layer(
            *src_p, lp, n_seq=bp, t_len=sp, time_major=False, ctab=ctab_p, stab=stab_p,
            attend=lambda q, kcat: _mla_prompt(q, kcat, bp, sp),
            shift0=zero_shift, wkv0=zero_wkv, conv0=zero_conv)
        outs_p[0].append(ckv.reshape(-1, PAGE_SIZE, MLA_KV_RANK))
        outs_p[1].append(kr.reshape(-1, PAGE_SIZE, MLA_ROPE_DIM))
        outs_p[2].append(wkv)
        outs_p[3].append(p_r.reshape(bp, sp, RWKV_PROJ)[:, -1])
        outs_p[4].append(cu.reshape(bp, sp, CONV_DIM)[:, -2:])
        src_p = ('ln2', pend_p)

    hp = _ln2(*pend_p, min(512, np_))
    hs = _ln2(*pend_s, min(512, ns))
    y_p = hp.reshape(bp, sp, d)
    y_s = jnp.swapaxes(hs.reshape(td, bd, d), 0, 1)
    return (y_p, y_s) + tuple(jnp.stack(o) for o in outs_p) + tuple(jnp.stack(o) for o in outs_s)
```

```python
import functools

import numpy as np
import jax
import jax.numpy as jnp
from jax import lax
from jax.experimental import pallas as pl
from jax.experimental.pallas import tpu as pltpu
from jax.experimental.pallas import tpu_sc as plsc

F32 = jnp.float32
BF16 = jnp.bfloat16

MLA_HEADS = 8
MLA_NOPE_DIM = 64
MLA_ROPE_DIM = 32
MLA_Q_RANK = 192
MLA_KV_RANK = 128
MLA_V_DIM = 64
ROPE_THETA = 10000.0
ATTN_SCALE = (MLA_NOPE_DIM + MLA_ROPE_DIM) ** -0.5
RWKV_DIM = 256
RWKV_HEADS = 4
RWKV_HEAD_DIM = 64
RWKV_PROJ = 3 * RWKV_DIM + 64 + 64 + 128
CONV_DIM = 256
GN_EPS = 64e-5
N_GROUPS = 4
EXPERTS_PER_GROUP = 8
N_EXPERTS = N_GROUPS * EXPERTS_PER_GROUP
EXPERT_FF = 256
DEPTH = 2
ALPHA = (2 * DEPTH) ** 0.25
LN_EPS = 1e-5
RMS_EPS = 1e-6
PAGE_SIZE = 128

QPAD = 256
QHEAD = 256
KCAT = 256
ONE_LANE = KCAT - 1
LOG2E = 1.4426950408889634
COL_CKV = QPAD
COL_KR = COL_CKV + MLA_KV_RANK
COL_RWKV = COL_KR + 128
COL_CONV = COL_RWKV + RWKV_PROJ
IN_PAD = COL_CONV + 3 * CONV_DIM

LANES = 128
ATT_TQ = 256
ATT_TK = 512
PAGES_PER_STEP = 32
SAMPLE_GROUPS = 4
SAMPLE_SLOTS = 3
NEW_PAD = 16
MOE_ROWS = 256
MOE_POS_TILES = 8
VMEM_LIMIT = 56 * 1024 * 1024


def _cparams(sem):
    return pltpu.CompilerParams(dimension_semantics=sem, vmem_limit_bytes=VMEM_LIMIT)


def _ln_rows(x, g, b):
    mu = jnp.mean(x, -1, keepdims=True)
    xc = x - mu
    var = jnp.mean(xc * xc, -1, keepdims=True)
    return xc * lax.rsqrt(var + LN_EPS) * g + b


def _sigmoid(x):
    return 1.0 / (1.0 + jnp.exp(-x))


HI_HALF = -65536


def _pack_bf16_pairs(x):
    w = x.shape[1] // 2
    bits = lax.bitcast_convert_type(x.astype(BF16).astype(F32), jnp.int32)
    return lax.shift_right_logical(bits[:, :w], 16) | (bits[:, w:] & HI_HALF)


def _unpack_bf16_pairs(p):
    lo = lax.bitcast_convert_type(lax.shift_left(p, 16), F32)
    hi = lax.bitcast_convert_type(p & HI_HALF, F32)
    return jnp.concatenate([lo, hi], axis=-1)


def _pack_row_halves(x, ref):
    half = x.shape[1] // 2
    ref[0] = _pack_bf16_pairs(x[:, :half])
    ref[1] = _pack_bf16_pairs(x[:, half:])


def _unpack_row_halves(p0, p1):
    return jnp.concatenate([_unpack_bf16_pairs(p0), _unpack_bf16_pairs(p1)], axis=-1)


def _moe_mix(y10_ref, y11_ref, y20_ref, y21_ref, gate_ref):
    gate = gate_ref[...]
    return (gate[:, 0:1] * _unpack_row_halves(y10_ref[...], y11_ref[...])
            + gate[:, 1:2] * _unpack_row_halves(y20_ref[...], y21_ref[...]))


def _ln2_kernel(x_ref, y10_ref, y11_ref, y20_ref, y21_ref, gate_ref, g_ref, b_ref, o_ref):
    y = _moe_mix(y10_ref, y11_ref, y20_ref, y21_ref, gate_ref)
    o_ref[...] = _ln_rows(ALPHA * x_ref[...] + y, g_ref[...], b_ref[...])


def _ln2(x, y12, gate, g, b, tm):
    n, d = x.shape
    nt = n // tm
    row = pl.BlockSpec((tm, d), lambda i: (i, 0))
    vec = pl.BlockSpec((1, d), lambda i: (0, 0))
    part = lambda k: pl.BlockSpec((tm, d // 4), lambda i: (i + k * nt, 0))
    return pl.pallas_call(
        _ln2_kernel, grid=(nt,),
        in_specs=[row, part(0), part(1), part(2), part(3),
                  pl.BlockSpec((tm, LANES), lambda i: (i, 0)), vec, vec],
        out_specs=row,
        out_shape=jax.ShapeDtypeStruct((n, d), F32), compiler_params=_cparams(("parallel",)),
    )(x, y12, y12, y12, y12, gate, g.reshape(1, d), b.reshape(1, d))


IN_PROJ_SOURCES = {'plain': 1, 'ln_in': 3, 'ln2': 8}


def _in_proj_kernel(*refs, source):
    n_src = IN_PROJ_SOURCES[source]
    src = refs[:n_src]
    w_ref, gq_ref, wq_ref, gkv_ref, ct_ref, st_ref = refs[n_src:n_src + 6]
    outs = refs[n_src + 6:]
    if source == 'plain':
        h = src[0][...]
    else:
        if source == 'ln_in':
            h = _ln_rows(src[0][...], src[1][...], src[2][...])
        else:
            h = _ln_rows(ALPHA * src[0][...] + _moe_mix(*src[1:6]), src[6][...], src[7][...])
        outs[0][...] = h
        outs = outs[1:]
    q_ref, kcat_ref, ckv_ref, kr_ref, pr_ref, pc_ref = outs
    p = jnp.dot(h.astype(BF16), w_ref[...], preferred_element_type=F32)
    ct = ct_ref[...]
    st = st_ref[...]

    def rope(tile):
        return tile * ct + pltpu.roll(tile, LANES - MLA_ROPE_DIM, 1) * st

    cq = p[:, :QPAD]
    ms = jnp.sum(cq * cq, -1, keepdims=True) * (1.0 / MLA_Q_RANK)
    cqn = (cq * lax.rsqrt(ms + RMS_EPS) * gq_ref[...]).astype(BF16)
    q = jnp.dot(cqn, wq_ref[...], preferred_element_type=F32)
    for h in range(MLA_HEADS):
        base = h * QHEAD
        q_ref[h, :, :LANES] = q[:, base:base + LANES].astype(BF16)
        q_ref[h, :, LANES:] = rope(q[:, base + LANES:base + QHEAD]).astype(BF16)

    c = p[:, COL_CKV:COL_CKV + MLA_KV_RANK]
    ckv = c * lax.rsqrt(jnp.mean(c * c, -1, keepdims=True) + RMS_EPS) * gkv_ref[...]
    ckv_ref[...] = ckv
    kr = rope(p[:, COL_KR:COL_KR + LANES])
    kr_ref[...] = kr[:, :MLA_ROPE_DIM]
    kcat_ref[:, :LANES] = ckv.astype(BF16)
    one = lax.broadcasted_iota(jnp.int32, kr.shape, 1) == ONE_LANE - LANES
    kcat_ref[:, LANES:] = jnp.where(one, 1.0, kr).astype(BF16)
    pr_ref[...] = p[:, COL_RWKV:COL_RWKV + RWKV_PROJ]
    pc_ref[...] = p[:, COL_CONV:COL_CONV + 3 * CONV_DIM]


def _in_proj(source, src, w_in_p, gq_p, w_q, g_kvn, ctab, stab, tm):
    n, d = src[0].shape
    nt = n // tm
    tab_blocks = ctab.shape[0] // tm
    row = lambda c: pl.BlockSpec((tm, c), lambda i: (i, 0))
    full = lambda a: pl.BlockSpec(a.shape, lambda i: (0,) * a.ndim)
    tab = pl.BlockSpec((tm, LANES), lambda i: (i % tab_blocks, 0))
    vec = pl.BlockSpec((1, d), lambda i: (0, 0))
    gkv = g_kvn.reshape(1, MLA_KV_RANK)
    if source == 'plain':
        src_args, src_specs = [src[0]], [row(d)]
    elif source == 'ln_in':
        src_args = [src[0], src[1].reshape(1, d), src[2].reshape(1, d)]
        src_specs = [row(d), vec, vec]
    else:
        x, y12, gate, ln_g, ln_b = src
        part = lambda k: pl.BlockSpec((tm, d // 4), lambda i: (i + k * nt, 0))
        src_args = [x, y12, y12, y12, y12, gate, ln_g.reshape(1, d), ln_b.reshape(1, d)]
        src_specs = [row(d), part(0), part(1), part(2), part(3), row(LANES), vec, vec]
    out_shape = (
        jax.ShapeDtypeStruct((MLA_HEADS, n, QHEAD), BF16),
        jax.ShapeDtypeStruct((n, KCAT), BF16),
        jax.ShapeDtypeStruct((n, MLA_KV_RANK), F32),
        jax.ShapeDtypeStruct((n, MLA_ROPE_DIM), F32),
        jax.ShapeDtypeStruct((n, RWKV_PROJ), F32),
        jax.ShapeDtypeStruct((n, 3 * CONV_DIM), F32),
    )
    out_specs = (
        pl.BlockSpec((MLA_HEADS, tm, QHEAD), lambda i: (0, i, 0)),
        row(KCAT), row(MLA_KV_RANK), row(MLA_ROPE_DIM), row(RWKV_PROJ), row(3 * CONV_DIM),
    )
    if source != 'plain':
        out_shape = (jax.ShapeDtypeStruct((n, d), F32),) + out_shape
        out_specs = (row(d),) + out_specs
    outs = pl.pallas_call(
        functools.partial(_in_proj_kernel, source=source), grid=(nt,),
        in_specs=src_specs + [full(w_in_p), full(gq_p), full(w_q), full(gkv), tab, tab],
        out_specs=out_specs, out_shape=out_shape, compiler_params=_cparams(("parallel",)),
    )(*src_args, w_in_p, gq_p, w_q, gkv, ctab, stab)
    return outs if source != 'plain' else (src[0],) + tuple(outs)


def _mla_prompt_kernel(q_ref, k_ref, o_ref, diag_ref, *, n_variants):
    i = pl.program_id(1)
    rows = MLA_HEADS * ATT_TQ

    @pl.when((pl.program_id(0) == 0) & (i == 0))
    def _():
        qoff = lax.broadcasted_iota(jnp.int32, (MLA_HEADS, ATT_TQ, ATT_TK), 1).reshape(rows, ATT_TK)
        diag_ref[...] = lax.broadcasted_iota(jnp.int32, (rows, ATT_TK), 1) - qoff

    q = q_ref[...].reshape(rows, QHEAD)
    n_full = (i * ATT_TQ) // ATT_TK
    lead = i * ATT_TQ - n_full * ATT_TK

    def chunk(j, carry, masked, width=ATT_TK):
        m, acc = carry
        k = k_ref[j * ATT_TK:j * ATT_TK + width, :]
        s = lax.dot_general(q, k, (((1,), (1,)), ((), ())), preferred_element_type=F32)
        if masked:
            s = jnp.where(diag_ref[:, :width] <= lead, s, -jnp.inf)
        m_new = jnp.maximum(m, jnp.max(s, -1, keepdims=True))
        pr = jnp.exp2(s - m_new).astype(BF16)
        acc = jnp.exp2(m - m_new) * acc + jnp.dot(pr, k, preferred_element_type=F32)
        return m_new, acc

    def variant(v, width):
        carry = (jnp.full((rows, 1), -jnp.inf, F32), jnp.zeros((rows, KCAT), F32))
        for j in range(v):
            carry = chunk(j, carry, False)
        _, acc = chunk(v, carry, True, width)
        o = acc[:, :MLA_KV_RANK] / acc[:, ONE_LANE:ONE_LANE + 1]
        for h in range(MLA_HEADS):
            o_ref[:, h * MLA_KV_RANK:(h + 1) * MLA_KV_RANK] = (
                o[h * ATT_TQ:(h + 1) * ATT_TQ].astype(BF16))

    for v in range(n_variants):
        for lead_v in range(0, ATT_TK, ATT_TQ):
            pl.when((n_full == v) & (lead == lead_v))(
                functools.partial(variant, v, lead_v + ATT_TQ))


def _mla_prompt(q, kcat, batch, seq):
    n = batch * seq
    nq = seq // ATT_TQ
    return pl.pallas_call(
        functools.partial(_mla_prompt_kernel, n_variants=seq // ATT_TK), grid=(batch, nq),
        in_specs=[pl.BlockSpec((MLA_HEADS, ATT_TQ, QHEAD), lambda b, i: (0, b * nq + i, 0)),
                  pl.BlockSpec((seq, KCAT), lambda b, i: (b, 0))],
        out_specs=pl.BlockSpec((ATT_TQ, MLA_HEADS * MLA_KV_RANK), lambda b, i: (b * nq + i, 0)),
        out_shape=jax.ShapeDtypeStruct((n, MLA_HEADS * MLA_KV_RANK), BF16),
        scratch_shapes=[pltpu.VMEM((MLA_HEADS * ATT_TQ, ATT_TK), jnp.int32)],
        compiler_params=_cparams(("arbitrary", "arbitrary")),
    )(q, kcat)


def _mla_sample_kernel(pt_ref, q_ref, kn_ref, ckv_hbm, krt_hbm, o_ref,
                       ckv_buf, krt_buf, sem, m_ref, l_ref, acc_ref,
                       *, n_new, layer, n_seq, n_chunks):
    npg = PAGES_PER_STEP
    b = pl.program_id(0)
    c = pl.program_id(1)
    step = b * n_chunks + c
    n_steps = n_seq * n_chunks
    slot = step % SAMPLE_SLOTS
    ahead = SAMPLE_SLOTS - 1

    def page_copies(bb, cc, sl):
        copies = []
        for j in range(npg):
            pid = 0 if bb is None else pt_ref[bb, cc * npg + j]
            copies.append(pltpu.make_async_copy(
                ckv_hbm.at[layer, pid], ckv_buf.at[sl, pl.ds(j * PAGE_SIZE, PAGE_SIZE), :],
                sem.at[sl, 0]))
            copies.append(pltpu.make_async_copy(
                krt_hbm.at[layer, pid], krt_buf.at[sl, j], sem.at[sl, 1]))
        return copies

    def start_chunk(t):
        @pl.when(t < n_steps)
        def _():
            for cp in page_copies(t // n_chunks, t % n_chunks, t % SAMPLE_SLOTS):
                cp.start()

    @pl.when(step == 0)
    def _():
        for t in range(ahead):
            start_chunk(t)

    start_chunk(step + ahead)

    for cp in page_copies(None, None, slot):
        cp.wait()

    @pl.when(c == 0)
    def _():
        m_ref[...] = jnp.full_like(m_ref, -jnp.inf)
        l_ref[...] = jnp.zeros_like(l_ref)
        acc_ref[...] = jnp.zeros_like(acc_ref)

    q = q_ref[...]
    q_lat = q[:, :MLA_KV_RANK]
    q_rope = q[:, MLA_KV_RANK:MLA_KV_RANK + MLA_ROPE_DIM]
    nt = (((1,), (1,)), ((), ()))

    def update(stats, s, v):
        m, l, acc = stats
        m_new = jnp.maximum(m, jnp.max(s, -1, keepdims=True))
        a = jnp.exp2(m - m_new)
        pr = jnp.exp2(s - m_new)
        return (m_new, a * l + jnp.sum(pr, -1, keepdims=True),
                a * acc + jnp.dot(pr.astype(BF16), v, preferred_element_type=F32))

    groups = range(SAMPLE_GROUPS)
    stats = [(m_ref[g], l_ref[g], acc_ref[g]) for g in groups]
    ppg = npg // SAMPLE_GROUPS
    ks = [ckv_buf[slot, pl.ds(g * ppg * PAGE_SIZE, ppg * PAGE_SIZE), :].astype(BF16) for g in groups]
    krs = [jnp.concatenate([krt_buf[slot, g * ppg + j] for j in range(ppg)], axis=1).astype(BF16)
           for g in groups]
    ss = [lax.dot_general(q_lat, ks[g], nt, preferred_element_type=F32)
          + jnp.dot(q_rope, krs[g], preferred_element_type=F32) for g in groups]
    m_new = [jnp.maximum(stats[g][0], jnp.max(ss[g], -1, keepdims=True)) for g in groups]
    prs = [jnp.exp2(ss[g] - m_new[g]) for g in groups]
    pvs = [jnp.dot(prs[g].astype(BF16), ks[g], preferred_element_type=F32) for g in groups]
    for g in groups:
        a = jnp.exp2(stats[g][0] - m_new[g])
        stats[g] = (m_new[g], a * stats[g][1] + jnp.sum(prs[g], -1, keepdims=True),
                    a * stats[g][2] + pvs[g])
        m_ref[g], l_ref[g], acc_ref[g] = stats[g]

    @pl.when(c == pl.num_programs(1) - 1)
    def _():
        kn = kn_ref[...]
        sn = lax.dot_general(q, kn, nt, preferred_element_type=F32)
        t_row = lax.broadcasted_iota(jnp.int32, sn.shape, 0) % n_new
        sn = jnp.where(lax.broadcasted_iota(jnp.int32, sn.shape, 1) <= t_row, sn, -jnp.inf)
        final = list(stats)
        final[0] = update(final[0], sn, kn[:, :MLA_KV_RANK])
        m_all = functools.reduce(jnp.maximum, [st[0] for st in final])
        l = jnp.zeros_like(m_all)
        acc = jnp.zeros(acc_ref.shape[1:], F32)
        for m_g, l_g, acc_g in final:
            wg = jnp.exp2(m_g - m_all)
            l = l + wg * l_g
            acc = acc + wg * acc_g
        o_ref[...] = (acc / l).astype(BF16)


def _mla_sample(q, k_new, cache_ckv, cache_krope_t, page_table, layer):
    bd, rows, _ = q.shape
    n_new = k_new.shape[1]
    k_new = jnp.pad(k_new, ((0, 0), (0, NEW_PAD - n_new), (0, 0)))
    n_pages = page_table.shape[1]
    npg = PAGES_PER_STEP
    assert n_pages % npg == 0
    n_chunks = n_pages // npg
    hbm = pl.BlockSpec(memory_space=pl.ANY)
    in_specs = [pl.BlockSpec((None, rows, QHEAD), lambda b, c, pt: (b, 0, 0)),
                pl.BlockSpec((None, NEW_PAD, KCAT), lambda b, c, pt: (b, 0, 0)), hbm, hbm]
    grid_spec = pltpu.PrefetchScalarGridSpec(
        num_scalar_prefetch=1, grid=(bd, n_chunks), in_specs=in_specs,
        out_specs=pl.BlockSpec((None, rows, MLA_KV_RANK), lambda b, c, pt: (b, 0, 0)),
        scratch_shapes=[pltpu.VMEM((SAMPLE_SLOTS, npg * PAGE_SIZE, MLA_KV_RANK), F32),
                        pltpu.VMEM((SAMPLE_SLOTS, npg, MLA_ROPE_DIM, PAGE_SIZE), F32),
                        pltpu.SemaphoreType.DMA((SAMPLE_SLOTS, 2)),
                        pltpu.VMEM((SAMPLE_GROUPS, rows, 1), F32),
                        pltpu.VMEM((SAMPLE_GROUPS, rows, 1), F32),
                        pltpu.VMEM((SAMPLE_GROUPS, rows, MLA_KV_RANK), F32)])
    return pl.pallas_call(
        functools.partial(_mla_sample_kernel, n_new=n_new, layer=layer, n_seq=bd,
                          n_chunks=n_chunks),
        grid_spec=grid_spec, out_shape=jax.ShapeDtypeStruct((bd, rows, MLA_KV_RANK), BF16),
        compiler_params=_cparams(("arbitrary", "arbitrary")),
    )(page_table, q, k_new, cache_ckv, cache_krope_t)


def _prev_rows(x, before, shift):
    if shift == 1:
        row = lax.broadcasted_iota(jnp.int32, x.shape, 0)
        return jnp.where(row == 0, before, pltpu.roll(x, 1, 0))
    return jnp.concatenate([before, x[:x.shape[0] - shift]], axis=0)


def _mix_prep_kernel(pr_ref, pc_ref, sh0_ref, cv0_ref, mu_ref, w0_ref, a0_ref, wdec_ref, wiclr_ref,
                     wgate_ref, cw_ref,
                     r_ref, k_ref, v_ref, w_ref, a_ref, g_ref, co_ref, cu_ref,
                     sh_c, cu1_c, cu2_c, *, shift, carry):
    first = pl.program_id(1) == 0

    if carry:
        @pl.when(first)
        def _():
            sh_c[...] = sh0_ref[...]
            cu1_c[...] = cv0_ref[1]
            cu2_c[...] = cv0_ref[0]
        sh_before, cu1_before, cu2_before = sh_c[...], cu1_c[...], cu2_c[...]
    else:
        sh_before, cu1_before, cu2_before = sh0_ref[...], cv0_ref[1], cv0_ref[0]

    p = pr_ref[...]
    xs = p + (_prev_rows(p, sh_before, shift) - p) * mu_ref[...]
    r_ref[...] = xs[:, :RWKV_DIM]
    k_ref[...] = xs[:, RWKV_DIM:2 * RWKV_DIM]
    v_ref[...] = xs[:, 2 * RWKV_DIM:3 * RWKV_DIM]
    lora = xs[:, 3 * RWKV_DIM:3 * RWKV_DIM + LANES]
    z = w0_ref[...] + jnp.dot(jnp.tanh(lora).astype(BF16), wdec_ref[...], preferred_element_type=F32)
    nz = -z
    softplus = jnp.maximum(nz, 0.0) + jnp.log(1.0 + jnp.exp(-jnp.abs(nz)))
    w_ref[...] = jnp.exp(-jnp.exp(-softplus - 0.5))
    a_ref[...] = _sigmoid(a0_ref[...] + jnp.dot(lora.astype(BF16), wiclr_ref[...],
                                                preferred_element_type=F32))
    xg = xs[:, 3 * RWKV_DIM + LANES:]
    g_ref[...] = jnp.dot(_sigmoid(xg).astype(BF16), wgate_ref[...], preferred_element_type=F32)

    pc = pc_ref[...]
    cu = pc[:, CONV_DIM:2 * CONV_DIM] * pc[:, 2 * CONV_DIM:]
    cu1 = _prev_rows(cu, cu1_before, shift)
    cu2 = _prev_rows(cu1, cu2_before, shift)
    cw = cw_ref[...]
    co_ref[...] = (pc[:, :CONV_DIM] * (cw[0:1] * cu2 + cw[1:2] * cu1 + cw[2:3] * cu)).astype(BF16)
    cu_ref[...] = cu

    if carry:
        tm = p.shape[0]
        sh_c[...] = p[tm - 1:tm]
        cu1_c[...] = cu[tm - 1:tm]
        cu2_c[...] = cu1[tm - 1:tm]


def _mix_prep(p_r, p_c, shift0, conv0, mu, w0, a0, wdec_p, wiclr_p, wgate, conv_w, *, n_seq, tm, shift):
    n = p_r.shape[0]
    carry = shift == 1
    tps = n // n_seq // tm if carry else 1
    grid = (n_seq, tps) if carry else (1, 1)
    row = lambda c: pl.BlockSpec((tm, c), lambda b, t: (b * tps + t, 0))
    full = lambda a: pl.BlockSpec(a.shape, lambda b, t: (0,) * a.ndim)
    if carry:
        sh_spec = pl.BlockSpec((None, 1, RWKV_PROJ), lambda b, t: (b, 0, 0))
        cv_spec = pl.BlockSpec((None, 2, 1, CONV_DIM), lambda b, t: (b, 0, 0, 0))
        shift0 = shift0.reshape(n_seq, 1, RWKV_PROJ)
        conv0 = conv0.reshape(n_seq, 2, 1, CONV_DIM)
    else:
        sh_spec = full(shift0)
        conv0 = jnp.swapaxes(conv0, 0, 1)
        cv_spec = full(conv0)
    vec = lambda a: a.reshape(1, -1)
    args = (p_r, p_c, shift0, conv0, vec(mu), vec(w0), vec(a0), wdec_p, wiclr_p, wgate, conv_w)
    in_specs = [row(RWKV_PROJ), row(3 * CONV_DIM), sh_spec, cv_spec] + [full(a) for a in args[4:]]
    f32o = jax.ShapeDtypeStruct((n, RWKV_DIM), F32)
    out_shape = (f32o,) * 6 + (jax.ShapeDtypeStruct((n, CONV_DIM), BF16), f32o)
    rows_before = shift
    return pl.pallas_call(
        functools.partial(_mix_prep_kernel, shift=shift, carry=carry), grid=grid,
        in_specs=in_specs, out_specs=(row(RWKV_DIM),) * 8, out_shape=out_shape,
        scratch_shapes=[pltpu.VMEM((rows_before, RWKV_PROJ), F32),
                        pltpu.VMEM((rows_before, CONV_DIM), F32),
                        pltpu.VMEM((rows_before, CONV_DIM), F32)],
        compiler_params=_cparams(("parallel", "arbitrary")),
    )(*args)


def _wkv_kernel(r_ref, k_ref, v_ref, w_ref, a_ref, kk_ref, ka_ref, rk_ref, gn_ref, bn_ref, s0_ref,
                y_ref, sout_ref, s_ref, al_ref, wr_ref, be_ref, km_ref, wd_ref):
    tb = pl.program_id(1)
    hd = RWKV_HEAD_DIM

    @pl.when(tb == 0)
    def _():
        s_ref[...] = s0_ref[...]

    kk_t, ka_t, rk_t = kk_ref[...], ka_ref[...], rk_ref[...]
    gn_t, bn_t = gn_ref[...], bn_ref[...]

    def step(t, _):
        r, k, v, w, a = r_ref[t], k_ref[t], v_ref[t], w_ref[t], a_ref[t]
        kk = k * kk_t
        kk = kk / jnp.maximum(jnp.sqrt(jnp.sum(kk * kk, 0, keepdims=True)), 1e-12)
        km = k * (1.0 + (a - 1.0) * ka_t)
        be = kk * a
        al_ref[...] = -kk
        wr_ref[...] = w * r
        be_ref[...] = be
        km_ref[...] = km
        wd_ref[...] = w
        sa = jnp.zeros((hd, LANES), F32)
        y0 = jnp.zeros((hd, LANES), F32)
        for i in range(hd):
            s_i = s_ref[i]
            sa = sa + s_i * al_ref[i:i + 1, :]
            y0 = y0 + s_i * wr_ref[i:i + 1, :]
        br = jnp.sum(be * r, 0, keepdims=True)
        kr = jnp.sum(km * r, 0, keepdims=True)
        y = y0 + sa * br + v * kr
        for i in range(hd):
            s_ref[i] = s_ref[i] * wd_ref[i:i + 1, :] + sa * be_ref[i:i + 1, :] + v * km_ref[i:i + 1, :]
        mean = jnp.mean(y, 0, keepdims=True)
        yc = y - mean
        var = jnp.mean(yc * yc, 0, keepdims=True)
        bonus = jnp.sum(r * km * rk_t, 0, keepdims=True) * v
        y_ref[t] = yc * lax.rsqrt(var + GN_EPS) * gn_t + bn_t + bonus
        return 0

    lax.fori_loop(0, r_ref.shape[0], step, 0)

    @pl.when(tb == pl.num_programs(1) - 1)
    def _():
        sout_ref[...] = s_ref[...]


def _wkv(r, k, v, w, a, kk_t, ka_t, rk_t, gn_t, bn_t, s0, tt):
    t_len, hd, lanes = r.shape
    seq = pl.BlockSpec((tt, hd, LANES), lambda g, t: (t, 0, g))
    par = pl.BlockSpec((hd, LANES), lambda g, t: (0, g))
    st = pl.BlockSpec((hd, hd, LANES), lambda g, t: (0, 0, g))
    return pl.pallas_call(
        _wkv_kernel, grid=(lanes // LANES, t_len // tt),
        in_specs=[seq] * 5 + [par] * 5 + [st], out_specs=(seq, st),
        out_shape=(jax.ShapeDtypeStruct((t_len, hd, lanes), F32),
                   jax.ShapeDtypeStruct((hd, hd, lanes), F32)),
        scratch_shapes=[pltpu.VMEM((hd, hd, LANES), F32)] + [pltpu.VMEM((hd, LANES), F32)] * 5,
        compiler_params=_cparams(("parallel", "arbitrary")),
    )(r, k, v, w, a, kk_t, ka_t, rk_t, gn_t, bn_t, s0)


WKV_CHUNK = 64
WKV_BLOCK = 512


def _bf16_parts(x, n):
    parts = []
    for _ in range(n):
        p = x.astype(BF16)
        parts.append(p)
        x = x - p.astype(F32)
    return parts


def _dot(a, b):
    return jnp.dot(a.astype(BF16), b.astype(BF16), preferred_element_type=F32)


def _dot_nt(a, b):
    return lax.dot_general(a.astype(BF16), b.astype(BF16), (((1,), (1,)), ((), ())),
                           preferred_element_type=F32)


def _wkv_chunk_kernel(r_ref, k_ref, v_ref, w_ref, a_ref, par_ref, s0_ref, y_ref, sout_ref, st_ref):
    c_len = WKV_CHUNK
    width = RWKV_DIM
    shift = RWKV_HEAD_DIM.bit_length() - 1

    @pl.when(pl.program_id(1) == 0)
    def _():
        st_ref[...] = s0_ref[...]

    row = lax.broadcasted_iota(jnp.int32, (width, width), 0)
    col = lax.broadcasted_iota(jnp.int32, (width, width), 1)
    same_head = (row >> shift) == (col >> shift)
    eye = row == col
    head_ones = jnp.where(same_head, 1.0, 0.0).astype(BF16)
    t_idx = lax.broadcasted_iota(jnp.int32, (c_len, width), 0)
    s_idx = lax.broadcasted_iota(jnp.int32, (c_len, width), 1) & (c_len - 1)
    strict, incl = s_idx < t_idx, s_idx <= t_idx

    def head_sum(x):
        return jnp.dot(x.astype(BF16), head_ones, preferred_element_type=F32)

    def blockdiag(z):
        return jnp.where(same_head, jnp.concatenate([z] * RWKV_HEADS, axis=0), 0.0)

    par = par_ref[...]
    k_k, k_a, r_k, gn, bn = (par[i:i + 1] for i in range(5))
    r_all, k_all, v_all, w_all, a_all = r_ref[...], k_ref[...], v_ref[...], w_ref[...], a_ref[...]
    kk_all = k_all * k_k
    kk_all = kk_all / jnp.maximum(jnp.sqrt(head_sum(kk_all * kk_all)), 1e-12)
    km_all = k_all * (1.0 + (a_all - 1.0) * k_a)
    logw_all = jnp.log(w_all)
    bonus_all = head_sum(r_all * km_all * r_k) * v_all

    t_blk = r_all.shape[0]
    n_chunks = t_blk // c_len
    tri = jnp.where(same_head & (col <= row), 1.0, 0.0).astype(BF16)
    logw_parts = _bf16_parts(logw_all, 3)
    logp = jnp.concatenate(
        [sum(jnp.dot(tri, p[t0:t0 + width], preferred_element_type=F32) for p in logw_parts)
         for t0 in range(0, t_blk, width)], axis=0)
    p_inc, p_inv, p_prev = jnp.exp(logp), jnp.exp(-logp), jnp.exp(logp - logw_all)
    at_all, rt_all = -kk_all * p_prev, r_all * p_inc
    bt_all, kt_all = kk_all * a_all * p_inv, km_all * p_inv

    chunks = range(n_chunks)
    rows = [slice(c * c_len, (c + 1) * c_len) for c in chunks]
    p_end = [p_inc[(c + 1) * c_len - 1:(c + 1) * c_len] for c in chunks]
    at, rt, bt, kt = ([x[sl] for sl in rows] for x in (at_all, rt_all, bt_all, kt_all))
    bt_bd = [blockdiag(x) for x in bt]
    kt_bd = [blockdiag(x) for x in kt]
    v_bd = [blockdiag(v_all[sl]) for sl in rows]
    at_bd = [blockdiag(x) for x in at]
    a_ab = [jnp.where(strict, _dot_nt(at[c], bt_bd[c]), 0.0) for c in chunks]
    a_ak = [jnp.where(strict, _dot_nt(at[c], kt_bd[c]), 0.0) for c in chunks]
    a_rb = [jnp.where(incl, _dot_nt(rt[c], bt_bd[c]), 0.0) for c in chunks]
    a_rk = [jnp.where(incl, _dot_nt(rt[c], kt_bd[c]), 0.0) for c in chunks]
    lp = [blockdiag(x) for x in a_ab]
    ident = jnp.where(eye, 1.0, 0.0)
    tinv = [ident + x for x in lp]
    for _ in range(c_len.bit_length() - 2):
        lp = [_dot(x, x) for x in lp]
        tinv = [tinv[c] + _dot(tinv[c], lp[c]) for c in chunks]
    g_bd = [_dot(blockdiag(a_ak[c]), v_bd[c]) for c in chunks]
    w_bd = [_dot(tinv[c], at_bd[c]) for c in chunks]
    u0_bd = [_dot(tinv[c], g_bd[c]) for c in chunks]
    q = [rt[c] + _dot(a_rb[c], w_bd[c]) for c in chunks]
    y0 = [_dot(a_rb[c], u0_bd[c]) + _dot(a_rk[c], v_bd[c]) for c in chunks]
    bs_t = [blockdiag(bt[c] * p_end[c]).T for c in chunks]
    ks_t = [blockdiag(kt[c] * p_end[c]).T for c in chunks]
    m = [jnp.where(eye, p_end[c], 0.0) + _dot(bs_t[c], w_bd[c]) for c in chunks]
    n0 = [_dot(bs_t[c], u0_bd[c]) + _dot(ks_t[c], v_bd[c]) for c in chunks]

    st = st_ref[...]
    ys = []
    for c in chunks:
        ys.append(_dot(q[c], st) + y0[c])
        st = _dot(m[c], st) + n0[c]
    st_ref[...] = st

    y = jnp.concatenate(ys, axis=0)
    mean = head_sum(y) * (1.0 / RWKV_HEAD_DIM)
    yc = y - mean
    var = head_sum(yc * yc) * (1.0 / RWKV_HEAD_DIM)
    y_ref[...] = yc * lax.rsqrt(var + GN_EPS) * gn + bn + bonus_all

    @pl.when(pl.program_id(1) == pl.num_programs(1) - 1)
    def _():
        sout_ref[...] = st_ref[...]


def _wkv_chunked(r, k, v, w, a, par, s0_bd, n_seq):
    n, width = r.shape
    nblk = n // n_seq // WKV_BLOCK
    seq = pl.BlockSpec((WKV_BLOCK, width), lambda b, j: (b * nblk + j, 0))
    st = pl.BlockSpec((None, width, width), lambda b, j: (b, 0, 0))
    return pl.pallas_call(
        _wkv_chunk_kernel, grid=(n_seq, nblk),
        in_specs=[seq] * 5 + [pl.BlockSpec(par.shape, lambda b, j: (0, 0)), st],
        out_specs=(seq, st),
        out_shape=(jax.ShapeDtypeStruct((n, width), F32),
                   jax.ShapeDtypeStruct((n_seq, width, width), F32)),
        scratch_shapes=[pltpu.VMEM((width, width), F32)],
        compiler_params=_cparams(("parallel", "arbitrary")),
    )(r, k, v, w, a, par, s0_bd)


def _mix_out_kernel(h_ref, o_ref, y_ref, g_ref, co_ref, wo_ref, g1_ref, b1_ref, wr_ref, br_ref,
                    x_ref, xp_ref, gate_ref, route_ref, cnt_ref):
    mix = jnp.concatenate(
        [o_ref[...], (y_ref[...] * g_ref[...]).astype(BF16), co_ref[...]], axis=-1)
    pre = ALPHA * h_ref[...] + jnp.dot(mix, wo_ref[...], preferred_element_type=F32)
    x = _ln_rows(pre, g1_ref[...], b1_ref[...])
    x_ref[...] = x
    xb = x.astype(BF16)
    _pack_row_halves(x, xp_ref)

    logits = jnp.dot(xb, wr_ref[...], preferred_element_type=F32) + br_ref[...]
    lane = lax.broadcasted_iota(jnp.int32, logits.shape, 1)
    ninf = -jnp.inf
    lg = jnp.where(lane < N_GROUPS, logits, ninf)
    mg = jnp.max(lg, -1, keepdims=True)
    g_p = 1.0 / jnp.sum(jnp.exp(lg - mg), -1, keepdims=True)
    g_idx = jnp.min(jnp.where(lg == mg, lane, LANES), -1, keepdims=True)
    lo = N_GROUPS + EXPERTS_PER_GROUP * g_idx
    le = jnp.where((lane >= lo) & (lane < lo + EXPERTS_PER_GROUP), logits, ninf)
    m1 = jnp.max(le, -1, keepdims=True)
    i1 = jnp.min(jnp.where(le == m1, lane, LANES), -1, keepdims=True)
    le2 = jnp.where(lane == i1, ninf, le)
    m2 = jnp.max(le2, -1, keepdims=True)
    i2 = jnp.min(jnp.where(le2 == m2, lane, LANES), -1, keepdims=True)
    e2 = jnp.exp(m2 - m1)
    gate1 = g_p / (1.0 + e2)
    gate2 = g_p * e2 / (1.0 + e2)
    gate_ref[...] = jnp.where(lane == 0, gate1, jnp.where(lane == 1, gate2, 0.0))

    pick1, pick2 = lane == i1, lane == i2
    picks = jnp.where(pick1 | pick2, 1.0, 0.0)
    tm = logits.shape[0]
    tri = jnp.where(lax.broadcasted_iota(jnp.int32, (tm, tm), 1)
                    < lax.broadcasted_iota(jnp.int32, (tm, tm), 0), 1.0, 0.0).astype(BF16)
    before = jnp.dot(tri, picks.astype(BF16), preferred_element_type=F32)
    rank1 = jnp.sum(jnp.where(pick1, before, 0.0), -1, keepdims=True).astype(jnp.int32)
    rank2 = jnp.sum(jnp.where(pick2, before, 0.0), -1, keepdims=True).astype(jnp.int32)
    route_ref[...] = jnp.where(lane == 0, i1, jnp.where(lane == 1, i2, jnp.where(
        lane == 2, rank1, jnp.where(lane == 3, rank2, 0))))
    cnt_ref[...] = jnp.sum(picks, 0, keepdims=True).astype(jnp.int32)


def _moe_pos_kernel(route_ref, base_ref, pos_ref, *, tm):
    for u in range(base_ref.shape[0]):
        route = route_ref[u * tm:(u + 1) * tm, :]
        base = base_ref[u]
        lane = lax.broadcasted_iota(jnp.int32, route.shape, 1)
        pos = []
        for c in range(2):
            start = jnp.sum(jnp.where(lane == route[:, c:c + 1], base, 0), -1, keepdims=True)
            pos.append(start + route[:, 2 + c:3 + c])
        pos_ref[u * tm:(u + 1) * tm, :] = jnp.where(lane == 0, pos[0], jnp.where(lane == 1, pos[1], 0))


def _moe_pos(route, base, tm):
    n = route.shape[0]
    tiles = min(MOE_POS_TILES, n // tm)
    row = pl.BlockSpec((tiles * tm, LANES), lambda i: (i, 0))
    return pl.pallas_call(
        functools.partial(_moe_pos_kernel, tm=tm), grid=(n // (tiles * tm),),
        in_specs=[row, pl.BlockSpec((tiles, 1, LANES), lambda i: (i, 0, 0))], out_specs=row,
        out_shape=jax.ShapeDtypeStruct((n, LANES), jnp.int32),
        compiler_params=_cparams(("parallel",)),
    )(route, base)


def _mix_out(h, o_lat, y, g, c_out, w_o, ln_g, ln_b, w_router, b_router, tm):
    n, d = h.shape
    row = lambda c: pl.BlockSpec((tm, c), lambda i: (i, 0))
    full = lambda a: pl.BlockSpec(a.shape, lambda i: (0,) * a.ndim)
    vec = lambda a: a.reshape(1, -1)
    args = (h, o_lat, y, g, c_out, w_o, vec(ln_g), vec(ln_b), w_router, b_router)
    in_specs = [row(d), row(o_lat.shape[1]), row(RWKV_DIM), row(RWKV_DIM), row(CONV_DIM)] + [
        full(a) for a in args[5:]]
    out_shape = (jax.ShapeDtypeStruct((n, d), F32), jax.ShapeDtypeStruct((2, n, d // 4), jnp.int32),
                 jax.ShapeDtypeStruct((n, LANES), F32), jax.ShapeDtypeStruct((n, LANES), jnp.int32),
                 jax.ShapeDtypeStruct((n // tm, 1, LANES), jnp.int32))
    return pl.pallas_call(
        _mix_out_kernel, grid=(n // tm,), in_specs=in_specs,
        out_specs=(row(d), pl.BlockSpec((2, tm, d // 4), lambda i: (0, i, 0)), row(LANES), row(LANES),
                   pl.BlockSpec((None, 1, LANES), lambda i: (i, 0, 0))),
        out_shape=out_shape, compiler_params=_cparams(("parallel",)),
    )(*args)


def _moe_kernel(be_ref, nv_ref, x0_ref, x1_ref, wgu_ref, wd_ref, o_ref):
    valid = nv_ref[pl.program_id(0)]

    @pl.when(valid > 0)
    def _():
        keep = lax.broadcasted_iota(jnp.int32, x0_ref.shape, 0) < valid
        x = _unpack_row_halves(jnp.where(keep, x0_ref[...], 0),
                               jnp.where(keep, x1_ref[...], 0)).astype(BF16)
        gu = jnp.dot(x, wgu_ref[...], preferred_element_type=F32)
        hg = gu[:, :EXPERT_FF]
        hid = hg * _sigmoid(hg) * gu[:, EXPERT_FF:]
        _pack_row_halves(jnp.dot(hid.astype(BF16), wd_ref[...], preferred_element_type=F32), o_ref)

    @pl.when(valid == 0)
    def _():
        o_ref[...] = jnp.zeros_like(o_ref)


def _moe_experts(blk_e, blk_valid, xs, cap, w_gu, w_d):
    quarter = xs.shape[1]
    d = 4 * quarter
    nb = cap // MOE_ROWS
    grid_spec = pltpu.PrefetchScalarGridSpec(
        num_scalar_prefetch=2, grid=(nb,),
        in_specs=[pl.BlockSpec((MOE_ROWS, quarter), lambda i, be, nv: (i, 0)),
                  pl.BlockSpec((MOE_ROWS, quarter), lambda i, be, nv: (i + nb, 0)),
                  pl.BlockSpec((None, d, 2 * EXPERT_FF), lambda i, be, nv: (be[i], 0, 0)),
                  pl.BlockSpec((None, EXPERT_FF, d), lambda i, be, nv: (be[i], 0, 0))],
        out_specs=pl.BlockSpec((2, MOE_ROWS, quarter), lambda i, be, nv: (0, i, 0)))
    return pl.pallas_call(
        _moe_kernel, grid_spec=grid_spec,
        out_shape=jax.ShapeDtypeStruct((2, cap, quarter), jnp.int32),
        compiler_params=_cparams(("arbitrary",)),
    )(blk_e, blk_valid, xs, xs, w_gu, w_d)


def _sc_mesh():
    return plsc.VectorSubcoreMesh(core_axis_name="core", subcore_axis_name="subcore")


SC_WINDOW = 128


def _sc_rows(n):
    mesh = _sc_mesh()
    unit = SC_WINDOW * mesh.num_cores * mesh.num_subcores
    return -(-n // unit) * unit


def _scatter_rows2(x, idx1, idx2, n_out):
    n, w = x.shape
    mesh = _sc_mesh()
    win = SC_WINDOW
    assert n == _sc_rows(n)

    @pl.kernel(out_type=jax.ShapeDtypeStruct((n_out, w), x.dtype), mesh=mesh, scratch_types=[])
    def scatter(x_hbm, i1_hbm, i2_hbm, o_hbm):
        def body(x_vmem, i1_vmem, i2_vmem):
            pltpu.sync_copy(x_vmem, o_hbm.at[i1_vmem.at[0]])
            pltpu.sync_copy(x_vmem, o_hbm.at[i2_vmem.at[0]])

        idx_spec = pl.BlockSpec((1, win), lambda i: (0, i))
        pltpu.emit_pipeline(
            body, grid=(n // win,),
            in_specs=[pl.BlockSpec((win, w), lambda i: (i, 0)), idx_spec, idx_spec],
            out_specs=[], core_axis_name=('core', 'subcore'),
            dimension_semantics=(pltpu.PARALLEL,),
        )(x_hbm, i1_hbm, i2_hbm)

    return scatter(x, idx1.reshape(1, n), idx2.reshape(1, n))


def _gather_rows(x, idx):
    n = idx.shape[0]
    w = x.shape[1]
    mesh = _sc_mesh()
    win = SC_WINDOW
    assert n == _sc_rows(n)

    @pl.kernel(out_type=jax.ShapeDtypeStruct((n, w), x.dtype), mesh=mesh, scratch_types=[])
    def gather(x_hbm, i_hbm, o_hbm):
        def body(i_vmem, o_vmem):
            pltpu.sync_copy(x_hbm.at[i_vmem.at[0]], o_vmem)

        pltpu.emit_pipeline(
            body, grid=(n // win,),
            in_specs=[pl.BlockSpec((1, win), lambda i: (0, i))],
            out_specs=[pl.BlockSpec((win, w), lambda i: (i, 0))],
            core_axis_name=('core', 'subcore'), dimension_semantics=(pltpu.PARALLEL,),
        )(i_hbm, o_hbm)

    return gather(x, idx.reshape(1, n))


def _moe_layout(tile_counts, n_tok):
    cnt = tile_counts[:, 0, N_GROUPS:N_GROUPS + N_EXPERTS]
    tile_off = jnp.cumsum(cnt, axis=0) - cnt
    total = jnp.sum(cnt, axis=0)
    pcounts = (total + MOE_ROWS - 1) // MOE_ROWS * MOE_ROWS
    pends = jnp.cumsum(pcounts)
    base = (pends - pcounts)[None, :] + tile_off
    base = jnp.pad(base, ((0, 0), (N_GROUPS, LANES - N_GROUPS - N_EXPERTS)))[:, None, :]
    n_blocks = 2 * n_tok // MOE_ROWS + N_EXPERTS
    blk_start = jnp.arange(n_blocks, dtype=jnp.int32) * MOE_ROWS
    blk_e = jnp.minimum(jnp.sum((pends[None, :] <= blk_start[:, None]).astype(jnp.int32), axis=1),
                        N_EXPERTS - 1)
    blk_valid = jnp.clip((pends - pcounts + total)[blk_e] - blk_start, 0, MOE_ROWS)
    return (base.astype(jnp.int32), blk_e.astype(jnp.int32), blk_valid.astype(jnp.int32),
            n_blocks * MOE_ROWS)


def _rope_tables(pos):
    half = MLA_ROPE_DIM // 2
    freqs = ROPE_THETA ** (-jnp.arange(half, dtype=F32) / half)
    ang = pos.astype(F32)[:, None] * freqs
    cos, sin = jnp.cos(ang), jnp.sin(ang)
    pad = jnp.zeros((pos.shape[0], LANES - MLA_ROPE_DIM), F32)
    return (jnp.concatenate([cos, cos, pad], -1), jnp.concatenate([-sin, sin, pad], -1))


def _swap_halves(w):
    half = MLA_ROPE_DIM // 2
    return jnp.concatenate([w[..., half:], w[..., :half]], -1)


def _layer_params(l, w_in, g_qn, w_uq, g_kvn, w_uk, w_uv, mu_shift, w0, w_decay, a0, w_iclr,
                  w_gate_out, k_k, k_a, r_k, lnx_g, lnx_b, conv_w, w_out, ln1_g, ln1_b,
                  w_group_router, b_group_router, w_expert_router, b_expert_router,
                  w_exp_gate, w_exp_up, w_exp_down, ln2_g, ln2_b):
    d = w_in.shape[1]
    hi = lax.Precision.HIGHEST
    wi = w_in[l]
    zc = lambda c: jnp.zeros((d, c), F32)
    c0 = MLA_Q_RANK
    c1 = c0 + MLA_KV_RANK
    c2 = c1 + MLA_ROPE_DIM
    w_kr = wi[:, c1:c2]
    w_in_p = jnp.concatenate([
        wi[:, :c0], zc(QPAD - MLA_Q_RANK), wi[:, c0:c1],
        w_kr, _swap_halves(w_kr), zc(LANES - 2 * MLA_ROPE_DIM), wi[:, c2:]], -1).astype(BF16)
    gq_p = jnp.concatenate([g_qn[l], jnp.zeros((QPAD - MLA_Q_RANK,), F32)]).reshape(1, QPAD)
    uq = w_uq[l]
    q_lat = jnp.einsum('rhn,chn->rhc', uq[..., :MLA_NOPE_DIM], w_uk[l], precision=hi)
    q_rope = uq[..., MLA_NOPE_DIM:]
    wq = jnp.concatenate([q_lat, q_rope, _swap_halves(q_rope),
                          jnp.zeros((MLA_Q_RANK, MLA_HEADS, QHEAD - LANES - 2 * MLA_ROPE_DIM), F32)], -1)
    wq = (wq * (ATTN_SCALE * LOG2E)).reshape(MLA_Q_RANK, MLA_HEADS * QHEAD)
    wq = jnp.concatenate([wq, jnp.zeros((QPAD - MLA_Q_RANK, MLA_HEADS * QHEAD), F32)], 0).astype(BF16)
    wo = w_out[l]
    mla_dim = MLA_HEADS * MLA_V_DIM
    wo_att = jnp.einsum('chv,hvd->hcd', w_uv[l], wo[:mla_dim].reshape(MLA_HEADS, MLA_V_DIM, -1),
                        precision=hi).reshape(MLA_HEADS * MLA_KV_RANK, -1)
    w_o = jnp.concatenate([wo_att, wo[mla_dim:]], 0).astype(BF16)
    z64 = jnp.zeros((64, RWKV_DIM), F32)
    wdec_p = jnp.concatenate([w_decay[l], z64], 0).astype(BF16)
    wiclr_p = jnp.concatenate([z64, w_iclr[l]], 0).astype(BF16)
    w_router = jnp.concatenate(
        [w_group_router[l], w_expert_router[l],
         jnp.zeros((d, LANES - N_GROUPS - N_EXPERTS), F32)], -1).astype(BF16)
    b_router = jnp.concatenate(
        [b_group_router[l], b_expert_router[l],
         jnp.zeros((LANES - N_GROUPS - N_EXPERTS,), F32)]).reshape(1, LANES)
    w_gu = jnp.concatenate([w_exp_gate[l], w_exp_up[l]], -1).astype(BF16)
    return dict(
        w_in_p=w_in_p, gq_p=gq_p, wq=wq, g_kvn=g_kvn[l], mu=mu_shift[l], w0=w0[l], a0=a0[l],
        wdec_p=wdec_p, wiclr_p=wiclr_p, wgate=w_gate_out[l].astype(BF16), conv_w=conv_w[l],
        k_k=k_k[l], k_a=k_a[l], r_k=r_k[l].reshape(-1), lnx_g=lnx_g[l], lnx_b=lnx_b[l],
        w_o=w_o, ln1_g=ln1_g[l], ln1_b=ln1_b[l], w_router=w_router, b_router=b_router,
        w_gu=w_gu, w_d=w_exp_down[l].astype(BF16), ln2_g=ln2_g[l], ln2_b=ln2_b[l])


def _head_tile(p, n_seq):
    t = p.reshape(RWKV_HEADS, RWKV_HEAD_DIM).T
    return jnp.tile(t, (1, n_seq))


def _layer(source, src, lp, *, n_seq, t_len, time_major, ctab, stab, attend, shift0, wkv0, conv0):
    n, d = src[0].shape
    tm = min(512, n)
    h, q, kcat, ckv, kr, p_r, p_c = _in_proj(source, src, lp['w_in_p'], lp['gq_p'], lp['wq'],
                                             lp['g_kvn'], ctab, stab, tm)
    o_lat = attend(q, kcat)

    r, k, v, w, a, g, c_out, cu = _mix_prep(
        p_r, p_c, shift0, conv0, lp['mu'], lp['w0'], lp['a0'], lp['wdec_p'], lp['wiclr_p'],
        lp['wgate'], lp['conv_w'], n_seq=n_seq, tm=tm, shift=n_seq if time_major else 1)

    names = ('k_k', 'k_a', 'r_k', 'lnx_g', 'lnx_b')
    if not time_major and t_len % WKV_BLOCK == 0:
        par = jnp.stack([lp[nm] for nm in names] + [jnp.zeros_like(lp['k_k'])] * 3)
        head_eye = jnp.eye(RWKV_HEADS, dtype=F32)
        s0_bd = jnp.einsum('bhvk,hg->bhkgv', wkv0, head_eye).reshape(n_seq, RWKV_DIM, RWKV_DIM)
        y, s_fin = _wkv_chunked(r, k, v, w, a, par, s0_bd, n_seq)
        wkv_new = jnp.einsum(
            'bhkgv,hg->bhvk',
            s_fin.reshape(n_seq, RWKV_HEADS, RWKV_HEAD_DIM, RWKV_HEADS, RWKV_HEAD_DIM), head_eye)
    else:
        lanes = n_seq * RWKV_HEADS

        def to_scan(x):
            if time_major:
                x = x.reshape(t_len, n_seq, RWKV_HEADS, RWKV_HEAD_DIM).transpose(0, 3, 1, 2)
            else:
                x = x.reshape(n_seq, t_len, RWKV_HEADS, RWKV_HEAD_DIM).transpose(1, 3, 0, 2)
            return x.reshape(t_len, RWKV_HEAD_DIM, lanes)

        tiles = [_head_tile(lp[nm], n_seq) for nm in names]
        s0 = wkv0.transpose(3, 2, 0, 1).reshape(RWKV_HEAD_DIM, RWKV_HEAD_DIM, lanes)
        y, s_fin = _wkv(*[to_scan(x) for x in (r, k, v, w, a)], *tiles, s0, min(16, t_len))
        y = y.reshape(t_len, RWKV_HEAD_DIM, n_seq, RWKV_HEADS)
        y = (y.transpose(0, 2, 3, 1) if time_major else y.transpose(2, 0, 3, 1)).reshape(n, RWKV_DIM)
        wkv_new = s_fin.reshape(RWKV_HEAD_DIM, RWKV_HEAD_DIM, n_seq, RWKV_HEADS).transpose(2, 3, 1, 0)

    x, xp, gate, route, tile_counts = _mix_out(
        h, o_lat, y, g, c_out, lp['w_o'], lp['ln1_g'], lp['ln1_b'], lp['w_router'], lp['b_router'], tm)
    base, blk_e, blk_valid, cap = _moe_layout(tile_counts, n)
    pos = _moe_pos(route, base, tm)
    pos1, pos2 = pos[:, 0], pos[:, 1]
    n_src = _sc_rows(2 * n)
    spare = 2 * cap + jnp.arange(n_src - 2 * n, dtype=jnp.int32)
    xp = jnp.pad(xp.reshape(2 * n, d // 4), ((0, n_src - 2 * n), (0, 0)))
    xs = _scatter_rows2(xp, jnp.concatenate([pos1, pos1 + cap, spare]),
                        jnp.concatenate([pos2, pos2 + cap, spare]), 2 * cap + n_src - 2 * n)
    yb = _moe_experts(blk_e, blk_valid, xs, cap, lp['w_gu'], lp['w_d'])
    n_dst = _sc_rows(4 * n)
    y12 = _gather_rows(yb.reshape(2 * cap, d // 4), jnp.concatenate(
        [pos1, pos1 + cap, pos2, pos2 + cap, jnp.zeros((n_dst - 4 * n,), jnp.int32)]))
    return (x, y12, gate, lp['ln2_g'], lp['ln2_b']), ckv, kr, p_r, cu, wkv_new


def kernel(x_prompt, x_sample, cache_ckv, cache_krope, state_wkv, state_shift, state_conv, page_table, ln_in_g, ln_in_b, w_in, g_qn, w_uq, g_kvn, w_uk, w_uv, mu_shift, w0, w_decay, a0, w_iclr, w_gate_out, k_k, k_a, r_k, lnx_g, lnx_b, conv_w, w_out, ln1_g, ln1_b, w_group_router, b_group_router, w_expert_router, b_expert_router, w_exp_gate, w_exp_up, w_exp_down, ln2_g, ln2_b):
    bp, sp, d = x_prompt.shape
    bd, td, _ = x_sample.shape
    past_len = page_table.shape[1] * PAGE_SIZE
    depth = w_in.shape[0]
    np_, ns = bp * sp, bd * td

    ctab_p, stab_p = _rope_tables(jnp.arange(sp))
    ctab_s, stab_s = _rope_tables(jnp.repeat(past_len + jnp.arange(td), bd))

    src_p = ('ln_in', (x_prompt.reshape(np_, d), ln_in_g, ln_in_b))
    src_s = ('ln_in', (jnp.swapaxes(x_sample, 0, 1).reshape(ns, d), ln_in_g, ln_in_b))

    krope_t = jnp.swapaxes(cache_krope, 2, 3)
    zero_shift = jnp.zeros((bp, RWKV_PROJ), F32)
    zero_wkv = jnp.zeros((bp, RWKV_HEADS, RWKV_HEAD_DIM, RWKV_HEAD_DIM), F32)
    zero_conv = jnp.zeros((bp, 2, CONV_DIM), F32)

    outs_p = [[] for _ in range(5)]
    outs_s = [[] for _ in range(5)]
    for l in range(depth):
        lp = _layer_params(l, w_in, g_qn, w_uq, g_kvn, w_uk, w_uv, mu_shift, w0, w_decay, a0,
                           w_iclr, w_gate_out, k_k, k_a, r_k, lnx_g, lnx_b, conv_w, w_out, ln1_g,
                           ln1_b, w_group_router, b_group_router, w_expert_router, b_expert_router,
                           w_exp_gate, w_exp_up, w_exp_down, ln2_g, ln2_b)

        def attend_s(q, kcat, l=l):
            qs = q.reshape(MLA_HEADS, td, bd, QHEAD).transpose(2, 0, 1, 3).reshape(
                bd, MLA_HEADS * td, QHEAD)
            kn = kcat.reshape(td, bd, KCAT).transpose(1, 0, 2)
            o = _mla_sample(qs, kn, cache_ckv, krope_t, page_table, l)
            return o.reshape(bd, MLA_HEADS, td, MLA_KV_RANK).transpose(2, 0, 1, 3).reshape(
                ns, MLA_HEADS * MLA_KV_RANK)

        pend_s, ckv, kr, p_r, cu, wkv = _layer(
            *src_s, lp, n_seq=bd, t_len=td, time_major=True, ctab=ctab_s, stab=stab_s,
            attend=attend_s, shift0=state_shift[l], wkv0=state_wkv[l], conv0=state_conv[l])
        tmaj = lambda x: jnp.swapaxes(x.reshape(td, bd, -1), 0, 1)
        outs_s[0].append(tmaj(ckv))
        outs_s[1].append(tmaj(kr))
        outs_s[2].append(wkv)
        outs_s[3].append(p_r.reshape(td, bd, RWKV_PROJ)[-1])
        outs_s[4].append(tmaj(cu)[:, -2:])
        src_s = ('ln2', pend_s)

        pend_p, ckv, kr, p_r, cu, wkv = _layer(
            *src_p, lp, n_seq=bp, t_len=sp, time_major=False, ctab=ctab_p, stab=stab_p,
            attend=lambda q, kcat: _mla_prompt(q, kcat, bp, sp),
            shift0=zero_shift, wkv0=zero_wkv, conv0=zero_conv)
        outs_p[0].append(ckv.reshape(-1, PAGE_SIZE, MLA_KV_RANK))
        outs_p[1].append(kr.reshape(-1, PAGE_SIZE, MLA_ROPE_DIM))
        outs_p[2].append(wkv)
        outs_p[3].append(p_r.reshape(bp, sp, RWKV_PROJ)[:, -1])
        outs_p[4].append(cu.reshape(bp, sp, CONV_DIM)[:, -2:])
        src_p = ('ln2', pend_p)

    hp = _ln2(*pend_p, min(512, np_))
    hs = _ln2(*pend_s, min(512, ns))
    y_p = hp.reshape(bp, sp, d)
    y_s = jnp.swapaxes(hs.reshape(td, bd, d), 0, 1)
    return (y_p, y_s) + tuple(jnp.stack(o) for o in outs_p) + tuple(jnp.stack(o) for o in outs_s)
```

```python
import functools

import numpy as np
import jax
import jax.numpy as jnp
from jax import lax
from jax.experimental import pallas as pl
from jax.experimental.pallas import tpu as pltpu
from jax.experimental.pallas import tpu_sc as plsc

F32 = jnp.float32
BF16 = jnp.bfloat16

MLA_HEADS = 8
MLA_NOPE_DIM = 64
MLA_ROPE_DIM = 32
MLA_Q_RANK = 192
MLA_KV_RANK = 128
MLA_V_DIM = 64
ROPE_THETA = 10000.0
ATTN_SCALE = (MLA_NOPE_DIM + MLA_ROPE_DIM) ** -0.5
RWKV_DIM = 256
RWKV_HEADS = 4
RWKV_HEAD_DIM = 64
RWKV_PROJ = 3 * RWKV_DIM + 64 + 64 + 128
CONV_DIM = 256
GN_EPS = 64e-5
N_GROUPS = 4
EXPERTS_PER_GROUP = 8
N_EXPERTS = N_GROUPS * EXPERTS_PER_GROUP
EXPERT_FF = 256
DEPTH = 2
ALPHA = (2 * DEPTH) ** 0.25
LN_EPS = 1e-5
RMS_EPS = 1e-6
PAGE_SIZE = 128

QPAD = 256
QHEAD = 256
KCAT = 256
ONE_LANE = KCAT - 1
LOG2E = 1.4426950408889634
COL_CKV = QPAD
COL_KR = COL_CKV + MLA_KV_RANK
COL_RWKV = COL_KR + 128
COL_CONV = COL_RWKV + RWKV_PROJ
IN_PAD = COL_CONV + 3 * CONV_DIM

LANES = 128
ATT_TQ = 256
ATT_TK = 512
PAGES_PER_STEP = 64
SAMPLE_GROUPS = 4
SAMPLE_SLOTS = 3
NEW_PAD = 16
MOE_ROWS = 256
MOE_POS_TILES = 8
VMEM_LIMIT = 56 * 1024 * 1024


def _cparams(sem):
    return pltpu.CompilerParams(dimension_semantics=sem, vmem_limit_bytes=VMEM_LIMIT)


def _ln_rows(x, g, b):
    mu = jnp.mean(x, -1, keepdims=True)
    xc = x - mu
    var = jnp.mean(xc * xc, -1, keepdims=True)
    return xc * lax.rsqrt(var + LN_EPS) * g + b


def _sigmoid(x):
    return 1.0 / (1.0 + jnp.exp(-x))


HI_HALF = -65536


def _pack_bf16_pairs(x):
    w = x.shape[1] // 2
    bits = lax.bitcast_convert_type(x.astype(BF16).astype(F32), jnp.int32)
    return lax.shift_right_logical(bits[:, :w], 16) | (bits[:, w:] & HI_HALF)


def _unpack_bf16_pairs(p):
    lo = lax.bitcast_convert_type(lax.shift_left(p, 16), F32)
    hi = lax.bitcast_convert_type(p & HI_HALF, F32)
    return jnp.concatenate([lo, hi], axis=-1)


def _pack_row_halves(x, ref):
    half = x.shape[1] // 2
    ref[0] = _pack_bf16_pairs(x[:, :half])
    ref[1] = _pack_bf16_pairs(x[:, half:])


def _unpack_row_halves(p0, p1):
    return jnp.concatenate([_unpack_bf16_pairs(p0), _unpack_bf16_pairs(p1)], axis=-1)


def _moe_mix(y10_ref, y11_ref, y20_ref, y21_ref, gate_ref):
    gate = gate_ref[...]
    return (gate[:, 0:1] * _unpack_row_halves(y10_ref[...], y11_ref[...])
            + gate[:, 1:2] * _unpack_row_halves(y20_ref[...], y21_ref[...]))


def _ln2_kernel(x_ref, y10_ref, y11_ref, y20_ref, y21_ref, gate_ref, g_ref, b_ref, o_ref):
    y = _moe_mix(y10_ref, y11_ref, y20_ref, y21_ref, gate_ref)
    o_ref[...] = _ln_rows(ALPHA * x_ref[...] + y, g_ref[...], b_ref[...])


def _ln2(x, y12, gate, g, b, tm):
    n, d = x.shape
    nt = n // tm
    row = pl.BlockSpec((tm, d), lambda i: (i, 0))
    vec = pl.BlockSpec((1, d), lambda i: (0, 0))
    part = lambda k: pl.BlockSpec((tm, d // 4), lambda i: (i + k * nt, 0))
    return pl.pallas_call(
        _ln2_kernel, grid=(nt,),
        in_specs=[row, part(0), part(1), part(2), part(3),
                  pl.BlockSpec((tm, LANES), lambda i: (i, 0)), vec, vec],
        out_specs=row,
        out_shape=jax.ShapeDtypeStruct((n, d), F32), compiler_params=_cparams(("parallel",)),
    )(x, y12, y12, y12, y12, gate, g.reshape(1, d), b.reshape(1, d))


IN_PROJ_SOURCES = {'plain': 1, 'ln_in': 3, 'ln2': 8}


def _in_proj_kernel(*refs, source):
    n_src = IN_PROJ_SOURCES[source]
    src = refs[:n_src]
    w_ref, gq_ref, wq_ref, gkv_ref, ct_ref, st_ref = refs[n_src:n_src + 6]
    outs = refs[n_src + 6:]
    if source == 'plain':
        h = src[0][...]
    else:
        if source == 'ln_in':
            h = _ln_rows(src[0][...], src[1][...], src[2][...])
        else:
            h = _ln_rows(ALPHA * src[0][...] + _moe_mix(*src[1:6]), src[6][...], src[7][...])
        outs[0][...] = h
        outs = outs[1:]
    q_ref, kcat_ref, ckv_ref, kr_ref, pr_ref, pc_ref = outs
    p = jnp.dot(h.astype(BF16), w_ref[...], preferred_element_type=F32)
    ct = ct_ref[...]
    st = st_ref[...]

    def rope(tile):
        return tile * ct + pltpu.roll(tile, LANES - MLA_ROPE_DIM, 1) * st

    cq = p[:, :QPAD]
    ms = jnp.sum(cq * cq, -1, keepdims=True) * (1.0 / MLA_Q_RANK)
    cqn = (cq * lax.rsqrt(ms + RMS_EPS) * gq_ref[...]).astype(BF16)
    q = jnp.dot(cqn, wq_ref[...], preferred_element_type=F32)
    for h in range(MLA_HEADS):
        base = h * QHEAD
        q_ref[h, :, :LANES] = q[:, base:base + LANES].astype(BF16)
        q_ref[h, :, LANES:] = rope(q[:, base + LANES:base + QHEAD]).astype(BF16)

    c = p[:, COL_CKV:COL_CKV + MLA_KV_RANK]
    ckv = c * lax.rsqrt(jnp.mean(c * c, -1, keepdims=True) + RMS_EPS) * gkv_ref[...]
    ckv_ref[...] = ckv
    kr = rope(p[:, COL_KR:COL_KR + LANES])
    kr_ref[...] = kr[:, :MLA_ROPE_DIM]
    kcat_ref[:, :LANES] = ckv.astype(BF16)
    one = lax.broadcasted_iota(jnp.int32, kr.shape, 1) == ONE_LANE - LANES
    kcat_ref[:, LANES:] = jnp.where(one, 1.0, kr).astype(BF16)
    pr_ref[...] = p[:, COL_RWKV:COL_RWKV + RWKV_PROJ]
    pc_ref[...] = p[:, COL_CONV:COL_CONV + 3 * CONV_DIM]


def _in_proj(source, src, w_in_p, gq_p, w_q, g_kvn, ctab, stab, tm):
    n, d = src[0].shape
    nt = n // tm
    tab_blocks = ctab.shape[0] // tm
    row = lambda c: pl.BlockSpec((tm, c), lambda i: (i, 0))
    full = lambda a: pl.BlockSpec(a.shape, lambda i: (0,) * a.ndim)
    tab = pl.BlockSpec((tm, LANES), lambda i: (i % tab_blocks, 0))
    vec = pl.BlockSpec((1, d), lambda i: (0, 0))
    gkv = g_kvn.reshape(1, MLA_KV_RANK)
    if source == 'plain':
        src_args, src_specs = [src[0]], [row(d)]
    elif source == 'ln_in':
        src_args = [src[0], src[1].reshape(1, d), src[2].reshape(1, d)]
        src_specs = [row(d), vec, vec]
    else:
        x, y12, gate, ln_g, ln_b = src
        part = lambda k: pl.BlockSpec((tm, d // 4), lambda i: (i + k * nt, 0))
        src_args = [x, y12, y12, y12, y12, gate, ln_g.reshape(1, d), ln_b.reshape(1, d)]
        src_specs = [row(d), part(0), part(1), part(2), part(3), row(LANES), vec, vec]
    out_shape = (
        jax.ShapeDtypeStruct((MLA_HEADS, n, QHEAD), BF16),
        jax.ShapeDtypeStruct((n, KCAT), BF16),
        jax.ShapeDtypeStruct((n, MLA_KV_RANK), F32),
        jax.ShapeDtypeStruct((n, MLA_ROPE_DIM), F32),
        jax.ShapeDtypeStruct((n, RWKV_PROJ), F32),
        jax.ShapeDtypeStruct((n, 3 * CONV_DIM), F32),
    )
    out_specs = (
        pl.BlockSpec((MLA_HEADS, tm, QHEAD), lambda i: (0, i, 0)),
        row(KCAT), row(MLA_KV_RANK), row(MLA_ROPE_DIM), row(RWKV_PROJ), row(3 * CONV_DIM),
    )
    if source != 'plain':
        out_shape = (jax.ShapeDtypeStruct((n, d), F32),) + out_shape
        out_specs = (row(d),) + out_specs
    outs = pl.pallas_call(
        functools.partial(_in_proj_kernel, source=source), grid=(nt,),
        in_specs=src_specs + [full(w_in_p), full(gq_p), full(w_q), full(gkv), tab, tab],
        out_specs=out_specs, out_shape=out_shape, compiler_params=_cparams(("parallel",)),
    )(*src_args, w_in_p, gq_p, w_q, gkv, ctab, stab)
    return outs if source != 'plain' else (src[0],) + tuple(outs)


def _mla_prompt_kernel(q_ref, k_ref, o_ref, diag_ref, *, n_variants):
    i = pl.program_id(1)
    rows = MLA_HEADS * ATT_TQ

    @pl.when((pl.program_id(0) == 0) & (i == 0))
    def _():
        qoff = lax.broadcasted_iota(jnp.int32, (MLA_HEADS, ATT_TQ, ATT_TK), 1).reshape(rows, ATT_TK)
        diag_ref[...] = lax.broadcasted_iota(jnp.int32, (rows, ATT_TK), 1) - qoff

    q = q_ref[...].reshape(rows, QHEAD)
    n_full = (i * ATT_TQ) // ATT_TK
    lead = i * ATT_TQ - n_full * ATT_TK

    def chunk(j, carry, masked, width=ATT_TK):
        m, acc = carry
        k = k_ref[j * ATT_TK:j * ATT_TK + width, :]
        s = lax.dot_general(q, k, (((1,), (1,)), ((), ())), preferred_element_type=F32)
        if masked:
            s = jnp.where(diag_ref[:, :width] <= lead, s, -jnp.inf)
        m_new = jnp.maximum(m, jnp.max(s, -1, keepdims=True))
        pr = jnp.exp2(s - m_new).astype(BF16)
        acc = jnp.exp2(m - m_new) * acc + jnp.dot(pr, k, preferred_element_type=F32)
        return m_new, acc

    def variant(v, width):
        carry = (jnp.full((rows, 1), -jnp.inf, F32), jnp.zeros((rows, KCAT), F32))
        for j in range(v):
            carry = chunk(j, carry, False)
        _, acc = chunk(v, carry, True, width)
        o = acc[:, :MLA_KV_RANK] / acc[:, ONE_LANE:ONE_LANE + 1]
        for h in range(MLA_HEADS):
            o_ref[:, h * MLA_KV_RANK:(h + 1) * MLA_KV_RANK] = (
                o[h * ATT_TQ:(h + 1) * ATT_TQ].astype(BF16))

    for v in range(n_variants):
        for lead_v in range(0, ATT_TK, ATT_TQ):
            pl.when((n_full == v) & (lead == lead_v))(
                functools.partial(variant, v, lead_v + ATT_TQ))


def _mla_prompt(q, kcat, batch, seq):
    n = batch * seq
    nq = seq // ATT_TQ
    return pl.pallas_call(
        functools.partial(_mla_prompt_kernel, n_variants=seq // ATT_TK), grid=(batch, nq),
        in_specs=[pl.BlockSpec((MLA_HEADS, ATT_TQ, QHEAD), lambda b, i: (0, b * nq + i, 0)),
                  pl.BlockSpec((seq, KCAT), lambda b, i: (b, 0))],
        out_specs=pl.BlockSpec((ATT_TQ, MLA_HEADS * MLA_KV_RANK), lambda b, i: (b * nq + i, 0)),
        out_shape=jax.ShapeDtypeStruct((n, MLA_HEADS * MLA_KV_RANK), BF16),
        scratch_shapes=[pltpu.VMEM((MLA_HEADS * ATT_TQ, ATT_TK), jnp.int32)],
        compiler_params=_cparams(("arbitrary", "arbitrary")),
    )(q, kcat)


def _mla_sample_kernel(pt_ref, q_ref, kn_ref, ckv_hbm, krt_hbm, o_ref,
                       ckv_buf, krt_buf, sem, m_ref, l_ref, acc_ref,
                       *, n_new, layer, n_seq, n_chunks):
    npg = PAGES_PER_STEP
    b = pl.program_id(0)
    c = pl.program_id(1)
    step = b * n_chunks + c
    n_steps = n_seq * n_chunks
    slot = step % SAMPLE_SLOTS
    ahead = SAMPLE_SLOTS - 1

    def page_copies(bb, cc, sl):
        copies = []
        for j in range(npg):
            pid = 0 if bb is None else pt_ref[bb, cc * npg + j]
            copies.append(pltpu.make_async_copy(
                ckv_hbm.at[layer, pid], ckv_buf.at[sl, pl.ds(j * PAGE_SIZE, PAGE_SIZE), :],
                sem.at[sl, 0]))
            copies.append(pltpu.make_async_copy(
                krt_hbm.at[layer, pid], krt_buf.at[sl, j], sem.at[sl, 1]))
        return copies

    def start_chunk(t):
        @pl.when(t < n_steps)
        def _():
            for cp in page_copies(t // n_chunks, t % n_chunks, t % SAMPLE_SLOTS):
                cp.start()

    @pl.when(step == 0)
    def _():
        for t in range(ahead):
            start_chunk(t)

    start_chunk(step + ahead)

    for cp in page_copies(None, None, slot):
        cp.wait()

    @pl.when(c == 0)
    def _():
        m_ref[...] = jnp.full_like(m_ref, -jnp.inf)
        l_ref[...] = jnp.zeros_like(l_ref)
        acc_ref[...] = jnp.zeros_like(acc_ref)

    q = q_ref[...]
    q_lat = q[:, :MLA_KV_RANK]
    q_rope = q[:, MLA_KV_RANK:MLA_KV_RANK + MLA_ROPE_DIM]
    nt = (((1,), (1,)), ((), ()))

    def update(stats, s, v):
        m, l, acc = stats
        m_new = jnp.maximum(m, jnp.max(s, -1, keepdims=True))
        a = jnp.exp2(m - m_new)
        pr = jnp.exp2(s - m_new)
        return (m_new, a * l + jnp.sum(pr, -1, keepdims=True),
                a * acc + jnp.dot(pr.astype(BF16), v, preferred_element_type=F32))

    groups = range(SAMPLE_GROUPS)
    stats = [(m_ref[g], l_ref[g], acc_ref[g]) for g in groups]
    ppg = npg // SAMPLE_GROUPS
    ks = [ckv_buf[slot, pl.ds(g * ppg * PAGE_SIZE, ppg * PAGE_SIZE), :].astype(BF16) for g in groups]
    krs = [jnp.concatenate([krt_buf[slot, g * ppg + j] for j in range(ppg)], axis=1).astype(BF16)
           for g in groups]
    ss = [lax.dot_general(q_lat, ks[g], nt, preferred_element_type=F32)
          + jnp.dot(q_rope, krs[g], preferred_element_type=F32) for g in groups]
    m_new = [jnp.maximum(stats[g][0], jnp.max(ss[g], -1, keepdims=True)) for g in groups]
    prs = [jnp.exp2(ss[g] - m_new[g]) for g in groups]
    pvs = [jnp.dot(prs[g].astype(BF16), ks[g], preferred_element_type=F32) for g in groups]
    for g in groups:
        a = jnp.exp2(stats[g][0] - m_new[g])
        stats[g] = (m_new[g], a * stats[g][1] + jnp.sum(prs[g], -1, keepdims=True),
                    a * stats[g][2] + pvs[g])
        m_ref[g], l_ref[g], acc_ref[g] = stats[g]

    @pl.when(c == pl.num_programs(1) - 1)
    def _():
        kn = kn_ref[...]
        sn = lax.dot_general(q, kn, nt, preferred_element_type=F32)
        t_row = lax.broadcasted_iota(jnp.int32, sn.shape, 0) % n_new
        sn = jnp.where(lax.broadcasted_iota(jnp.int32, sn.shape, 1) <= t_row, sn, -jnp.inf)
        final = list(stats)
        final[0] = update(final[0], sn, kn[:, :MLA_KV_RANK])
        m_all = functools.reduce(jnp.maximum, [st[0] for st in final])
        l = jnp.zeros_like(m_all)
        acc = jnp.zeros(acc_ref.shape[1:], F32)
        for m_g, l_g, acc_g in final:
            wg = jnp.exp2(m_g - m_all)
            l = l + wg * l_g
            acc = acc + wg * acc_g
        o_ref[...] = (acc / l).astype(BF16)


def _mla_sample(q, k_new, cache_ckv, cache_krope_t, page_table, layer):
    bd, rows, _ = q.shape
    n_new = k_new.shape[1]
    k_new = jnp.pad(k_new, ((0, 0), (0, NEW_PAD - n_new), (0, 0)))
    n_pages = page_table.shape[1]
    npg = PAGES_PER_STEP
    assert n_pages % npg == 0
    n_chunks = n_pages // npg
    hbm = pl.BlockSpec(memory_space=pl.ANY)
    in_specs = [pl.BlockSpec((None, rows, QHEAD), lambda b, c, pt: (b, 0, 0)),
                pl.BlockSpec((None, NEW_PAD, KCAT), lambda b, c, pt: (b, 0, 0)), hbm, hbm]
    grid_spec = pltpu.PrefetchScalarGridSpec(
        num_scalar_prefetch=1, grid=(bd, n_chunks), in_specs=in_specs,
        out_specs=pl.BlockSpec((None, rows, MLA_KV_RANK), lambda b, c, pt: (b, 0, 0)),
        scratch_shapes=[pltpu.VMEM((SAMPLE_SLOTS, npg * PAGE_SIZE, MLA_KV_RANK), F32),
                        pltpu.VMEM((SAMPLE_SLOTS, npg, MLA_ROPE_DIM, PAGE_SIZE), F32),
                        pltpu.SemaphoreType.DMA((SAMPLE_SLOTS, 2)),
                        pltpu.VMEM((SAMPLE_GROUPS, rows, 1), F32),
                        pltpu.VMEM((SAMPLE_GROUPS, rows, 1), F32),
                        pltpu.VMEM((SAMPLE_GROUPS, rows, MLA_KV_RANK), F32)])
    return pl.pallas_call(
        functools.partial(_mla_sample_kernel, n_new=n_new, layer=layer, n_seq=bd,
                          n_chunks=n_chunks),
        grid_spec=grid_spec, out_shape=jax.ShapeDtypeStruct((bd, rows, MLA_KV_RANK), BF16),
        compiler_params=_cparams(("arbitrary", "arbitrary")),
    )(page_table, q, k_new, cache_ckv, cache_krope_t)


def _prev_rows(x, before, shift):
    if shift == 1:
        row = lax.broadcasted_iota(jnp.int32, x.shape, 0)
        return jnp.where(row == 0, before, pltpu.roll(x, 1, 0))
    return jnp.concatenate([before, x[:x.shape[0] - shift]], axis=0)


def _mix_prep_kernel(pr_ref, pc_ref, sh0_ref, cv0_ref, mu_ref, w0_ref, a0_ref, wdec_ref, wiclr_ref,
                     wgate_ref, cw_ref,
                     r_ref, k_ref, v_ref, w_ref, a_ref, g_ref, co_ref, cu_ref,
                     sh_c, cu1_c, cu2_c, *, shift, carry):
    first = pl.program_id(1) == 0

    if carry:
        @pl.when(first)
        def _():
            sh_c[...] = sh0_ref[...]
            cu1_c[...] = cv0_ref[1]
            cu2_c[...] = cv0_ref[0]
        sh_before, cu1_before, cu2_before = sh_c[...], cu1_c[...], cu2_c[...]
    else:
        sh_before, cu1_before, cu2_before = sh0_ref[...], cv0_ref[1], cv0_ref[0]

    p = pr_ref[...]
    xs = p + (_prev_rows(p, sh_before, shift) - p) * mu_ref[...]
    r_ref[...] = xs[:, :RWKV_DIM]
    k_ref[...] = xs[:, RWKV_DIM:2 * RWKV_DIM]
    v_ref[...] = xs[:, 2 * RWKV_DIM:3 * RWKV_DIM]
    lora = xs[:, 3 * RWKV_DIM:3 * RWKV_DIM + LANES]
    z = w0_ref[...] + jnp.dot(jnp.tanh(lora).astype(BF16), wdec_ref[...], preferred_element_type=F32)
    nz = -z
    softplus = jnp.maximum(nz, 0.0) + jnp.log(1.0 + jnp.exp(-jnp.abs(nz)))
    w_ref[...] = jnp.exp(-jnp.exp(-softplus - 0.5))
    a_ref[...] = _sigmoid(a0_ref[...] + jnp.dot(lora.astype(BF16), wiclr_ref[...],
                                                preferred_element_type=F32))
    xg = xs[:, 3 * RWKV_DIM + LANES:]
    g_ref[...] = jnp.dot(_sigmoid(xg).astype(BF16), wgate_ref[...], preferred_element_type=F32)

    pc = pc_ref[...]
    cu = pc[:, CONV_DIM:2 * CONV_DIM] * pc[:, 2 * CONV_DIM:]
    cu1 = _prev_rows(cu, cu1_before, shift)
    cu2 = _prev_rows(cu1, cu2_before, shift)
    cw = cw_ref[...]
    co_ref[...] = (pc[:, :CONV_DIM] * (cw[0:1] * cu2 + cw[1:2] * cu1 + cw[2:3] * cu)).astype(BF16)
    cu_ref[...] = cu

    if carry:
        tm = p.shape[0]
        sh_c[...] = p[tm - 1:tm]
        cu1_c[...] = cu[tm - 1:tm]
        cu2_c[...] = cu1[tm - 1:tm]


def _mix_prep(p_r, p_c, shift0, conv0, mu, w0, a0, wdec_p, wiclr_p, wgate, conv_w, *, n_seq, tm, shift):
    n = p_r.shape[0]
    carry = shift == 1
    tps = n // n_seq // tm if carry else 1
    grid = (n_seq, tps) if carry else (1, 1)
    row = lambda c: pl.BlockSpec((tm, c), lambda b, t: (b * tps + t, 0))
    full = lambda a: pl.BlockSpec(a.shape, lambda b, t: (0,) * a.ndim)
    if carry:
        sh_spec = pl.BlockSpec((None, 1, RWKV_PROJ), lambda b, t: (b, 0, 0))
        cv_spec = pl.BlockSpec((None, 2, 1, CONV_DIM), lambda b, t: (b, 0, 0, 0))
        shift0 = shift0.reshape(n_seq, 1, RWKV_PROJ)
        conv0 = conv0.reshape(n_seq, 2, 1, CONV_DIM)
    else:
        sh_spec = full(shift0)
        conv0 = jnp.swapaxes(conv0, 0, 1)
        cv_spec = full(conv0)
    vec = lambda a: a.reshape(1, -1)
    args = (p_r, p_c, shift0, conv0, vec(mu), vec(w0), vec(a0), wdec_p, wiclr_p, wgate, conv_w)
    in_specs = [row(RWKV_PROJ), row(3 * CONV_DIM), sh_spec, cv_spec] + [full(a) for a in args[4:]]
    f32o = jax.ShapeDtypeStruct((n, RWKV_DIM), F32)
    out_shape = (f32o,) * 6 + (jax.ShapeDtypeStruct((n, CONV_DIM), BF16), f32o)
    rows_before = shift
    return pl.pallas_call(
        functools.partial(_mix_prep_kernel, shift=shift, carry=carry), grid=grid,
        in_specs=in_specs, out_specs=(row(RWKV_DIM),) * 8, out_shape=out_shape,
        scratch_shapes=[pltpu.VMEM((rows_before, RWKV_PROJ), F32),
                        pltpu.VMEM((rows_before, CONV_DIM), F32),
                        pltpu.VMEM((rows_before, CONV_DIM), F32)],
        compiler_params=_cparams(("parallel", "arbitrary")),
    )(*args)


def _wkv_kernel(r_ref, k_ref, v_ref, w_ref, a_ref, kk_ref, ka_ref, rk_ref, gn_ref, bn_ref, s0_ref,
                y_ref, sout_ref, s_ref, al_ref, wr_ref, be_ref, km_ref, wd_ref):
    tb = pl.program_id(1)
    hd = RWKV_HEAD_DIM

    @pl.when(tb == 0)
    def _():
        s_ref[...] = s0_ref[...]

    kk_t, ka_t, rk_t = kk_ref[...], ka_ref[...], rk_ref[...]
    gn_t, bn_t = gn_ref[...], bn_ref[...]

    def step(t, _):
        r, k, v, w, a = r_ref[t], k_ref[t], v_ref[t], w_ref[t], a_ref[t]
        kk = k * kk_t
        kk = kk / jnp.maximum(jnp.sqrt(jnp.sum(kk * kk, 0, keepdims=True)), 1e-12)
        km = k * (1.0 + (a - 1.0) * ka_t)
        be = kk * a
        al_ref[...] = -kk
        wr_ref[...] = w * r
        be_ref[...] = be
        km_ref[...] = km
        wd_ref[...] = w
        sa = jnp.zeros((hd, LANES), F32)
        y0 = jnp.zeros((hd, LANES), F32)
        for i in range(hd):
            s_i = s_ref[i]
            sa = sa + s_i * al_ref[i:i + 1, :]
            y0 = y0 + s_i * wr_ref[i:i + 1, :]
        br = jnp.sum(be * r, 0, keepdims=True)
        kr = jnp.sum(km * r, 0, keepdims=True)
        y = y0 + sa * br + v * kr
        for i in range(hd):
            s_ref[i] = s_ref[i] * wd_ref[i:i + 1, :] + sa * be_ref[i:i + 1, :] + v * km_ref[i:i + 1, :]
        mean = jnp.mean(y, 0, keepdims=True)
        yc = y - mean
        var = jnp.mean(yc * yc, 0, keepdims=True)
        bonus = jnp.sum(r * km * rk_t, 0, keepdims=True) * v
        y_ref[t] = yc * lax.rsqrt(var + GN_EPS) * gn_t + bn_t + bonus
        return 0

    lax.fori_loop(0, r_ref.shape[0], step, 0)

    @pl.when(tb == pl.num_programs(1) - 1)
    def _():
        sout_ref[...] = s_ref[...]


def _wkv(r, k, v, w, a, kk_t, ka_t, rk_t, gn_t, bn_t, s0, tt):
    t_len, hd, lanes = r.shape
    seq = pl.BlockSpec((tt, hd, LANES), lambda g, t: (t, 0, g))
    par = pl.BlockSpec((hd, LANES), lambda g, t: (0, g))
    st = pl.BlockSpec((hd, hd, LANES), lambda g, t: (0, 0, g))
    return pl.pallas_call(
        _wkv_kernel, grid=(lanes // LANES, t_len // tt),
        in_specs=[seq] * 5 + [par] * 5 + [st], out_specs=(seq, st),
        out_shape=(jax.ShapeDtypeStruct((t_len, hd, lanes), F32),
                   jax.ShapeDtypeStruct((hd, hd, lanes), F32)),
        scratch_shapes=[pltpu.VMEM((hd, hd, LANES), F32)] + [pltpu.VMEM((hd, LANES), F32)] * 5,
        compiler_params=_cparams(("parallel", "arbitrary")),
    )(r, k, v, w, a, kk_t, ka_t, rk_t, gn_t, bn_t, s0)


WKV_CHUNK = 64
WKV_BLOCK = 512


def _bf16_parts(x, n):
    parts = []
    for _ in range(n):
        p = x.astype(BF16)
        parts.append(p)
        x = x - p.astype(F32)
    return parts


def _dot(a, b):
    return jnp.dot(a.astype(BF16), b.astype(BF16), preferred_element_type=F32)


def _dot_nt(a, b):
    return lax.dot_general(a.astype(BF16), b.astype(BF16), (((1,), (1,)), ((), ())),
                           preferred_element_type=F32)


def _wkv_chunk_kernel(r_ref, k_ref, v_ref, w_ref, a_ref, par_ref, s0_ref, y_ref, sout_ref, st_ref):
    c_len = WKV_CHUNK
    width = RWKV_DIM
    shift = RWKV_HEAD_DIM.bit_length() - 1

    @pl.when(pl.program_id(1) == 0)
    def _():
        st_ref[...] = s0_ref[...]

    row = lax.broadcasted_iota(jnp.int32, (width, width), 0)
    col = lax.broadcasted_iota(jnp.int32, (width, width), 1)
    same_head = (row >> shift) == (col >> shift)
    eye = row == col
    head_ones = jnp.where(same_head, 1.0, 0.0).astype(BF16)
    t_idx = lax.broadcasted_iota(jnp.int32, (c_len, width), 0)
    s_idx = lax.broadcasted_iota(jnp.int32, (c_len, width), 1) & (c_len - 1)
    strict, incl = s_idx < t_idx, s_idx <= t_idx

    def head_sum(x):
        return jnp.dot(x.astype(BF16), head_ones, preferred_element_type=F32)

    def blockdiag(z):
        return jnp.where(same_head, jnp.concatenate([z] * RWKV_HEADS, axis=0), 0.0)

    par = par_ref[...]
    k_k, k_a, r_k, gn, bn = (par[i:i + 1] for i in range(5))
    r_all, k_all, v_all, w_all, a_all = r_ref[...], k_ref[...], v_ref[...], w_ref[...], a_ref[...]
    kk_all = k_all * k_k
    kk_all = kk_all / jnp.maximum(jnp.sqrt(head_sum(kk_all * kk_all)), 1e-12)
    km_all = k_all * (1.0 + (a_all - 1.0) * k_a)
    logw_all = jnp.log(w_all)
    bonus_all = head_sum(r_all * km_all * r_k) * v_all

    t_blk = r_all.shape[0]
    n_chunks = t_blk // c_len
    tri = jnp.where(same_head & (col <= row), 1.0, 0.0).astype(BF16)
    logw_parts = _bf16_parts(logw_all, 3)
    logp = jnp.concatenate(
        [sum(jnp.dot(tri, p[t0:t0 + width], preferred_element_type=F32) for p in logw_parts)
         for t0 in range(0, t_blk, width)], axis=0)
    p_inc, p_inv, p_prev = jnp.exp(logp), jnp.exp(-logp), jnp.exp(logp - logw_all)
    at_all, rt_all = -kk_all * p_prev, r_all * p_inc
    bt_all, kt_all = kk_all * a_all * p_inv, km_all * p_inv

    chunks = range(n_chunks)
    rows = [slice(c * c_len, (c + 1) * c_len) for c in chunks]
    p_end = [p_inc[(c + 1) * c_len - 1:(c + 1) * c_len] for c in chunks]
    at, rt, bt, kt = ([x[sl] for sl in rows] for x in (at_all, rt_all, bt_all, kt_all))
    bt_bd = [blockdiag(x) for x in bt]
    kt_bd = [blockdiag(x) for x in kt]
    v_bd = [blockdiag(v_all[sl]) for sl in rows]
    at_bd = [blockdiag(x) for x in at]
    a_ab = [jnp.where(strict, _dot_nt(at[c], bt_bd[c]), 0.0) for c in chunks]
    a_ak = [jnp.where(strict, _dot_nt(at[c], kt_bd[c]), 0.0) for c in chunks]
    a_rb = [jnp.where(incl, _dot_nt(rt[c], bt_bd[c]), 0.0) for c in chunks]
    a_rk = [jnp.where(incl, _dot_nt(rt[c], kt_bd[c]), 0.0) for c in chunks]
    lp = [blockdiag(x) for x in a_ab]
    ident = jnp.where(eye, 1.0, 0.0)
    tinv = [ident + x for x in lp]
    for _ in range(c_len.bit_length() - 2):
        lp = [_dot(x, x) for x in lp]
        tinv = [tinv[c] + _dot(tinv[c], lp[c]) for c in chunks]
    g_bd = [_dot(blockdiag(a_ak[c]), v_bd[c]) for c in chunks]
    w_bd = [_dot(tinv[c], at_bd[c]) for c in chunks]
    u0_bd = [_dot(tinv[c], g_bd[c]) for c in chunks]
    q = [rt[c] + _dot(a_rb[c], w_bd[c]) for c in chunks]
    y0 = [_dot(a_rb[c], u0_bd[c]) + _dot(a_rk[c], v_bd[c]) for c in chunks]
    bs_t = [blockdiag(bt[c] * p_end[c]).T for c in chunks]
    ks_t = [blockdiag(kt[c] * p_end[c]).T for c in chunks]
    m = [jnp.where(eye, p_end[c], 0.0) + _dot(bs_t[c], w_bd[c]) for c in chunks]
    n0 = [_dot(bs_t[c], u0_bd[c]) + _dot(ks_t[c], v_bd[c]) for c in chunks]

    st = st_ref[...]
    ys = []
    for c in chunks:
        ys.append(_dot(q[c], st) + y0[c])
        st = _dot(m[c], st) + n0[c]
    st_ref[...] = st

    y = jnp.concatenate(ys, axis=0)
    mean = head_sum(y) * (1.0 / RWKV_HEAD_DIM)
    yc = y - mean
    var = head_sum(yc * yc) * (1.0 / RWKV_HEAD_DIM)
    y_ref[...] = yc * lax.rsqrt(var + GN_EPS) * gn + bn + bonus_all

    @pl.when(pl.program_id(1) == pl.num_programs(1) - 1)
    def _():
        sout_ref[...] = st_ref[...]


def _wkv_chunked(r, k, v, w, a, par, s0_bd, n_seq):
    n, width = r.shape
    nblk = n // n_seq // WKV_BLOCK
    seq = pl.BlockSpec((WKV_BLOCK, width), lambda b, j: (b * nblk + j, 0))
    st = pl.BlockSpec((None, width, width), lambda b, j: (b, 0, 0))
    return pl.pallas_call(
        _wkv_chunk_kernel, grid=(n_seq, nblk),
        in_specs=[seq] * 5 + [pl.BlockSpec(par.shape, lambda b, j: (0, 0)), st],
        out_specs=(seq, st),
        out_shape=(jax.ShapeDtypeStruct((n, width), F32),
                   jax.ShapeDtypeStruct((n_seq, width, width), F32)),
        scratch_shapes=[pltpu.VMEM((width, width), F32)],
        compiler_params=_cparams(("parallel", "arbitrary")),
    )(r, k, v, w, a, par, s0_bd)


def _mix_out_kernel(h_ref, o_ref, y_ref, g_ref, co_ref, wo_ref, g1_ref, b1_ref, wr_ref, br_ref,
                    x_ref, xp_ref, gate_ref, route_ref, cnt_ref):
    mix = jnp.concatenate(
        [o_ref[...], (y_ref[...] * g_ref[...]).astype(BF16), co_ref[...]], axis=-1)
    pre = ALPHA * h_ref[...] + jnp.dot(mix, wo_ref[...], preferred_element_type=F32)
    x = _ln_rows(pre, g1_ref[...], b1_ref[...])
    x_ref[...] = x
    xb = x.astype(BF16)
    _pack_row_halves(x, xp_ref)

    logits = jnp.dot(xb, wr_ref[...], preferred_element_type=F32) + br_ref[...]
    lane = lax.broadcasted_iota(jnp.int32, logits.shape, 1)
    ninf = -jnp.inf
    lg = jnp.where(lane < N_GROUPS, logits, ninf)
    mg = jnp.max(lg, -1, keepdims=True)
    g_p = 1.0 / jnp.sum(jnp.exp(lg - mg), -1, keepdims=True)
    g_idx = jnp.min(jnp.where(lg == mg, lane, LANES), -1, keepdims=True)
    lo = N_GROUPS + EXPERTS_PER_GROUP * g_idx
    le = jnp.where((lane >= lo) & (lane < lo + EXPERTS_PER_GROUP), logits, ninf)
    m1 = jnp.max(le, -1, keepdims=True)
    i1 = jnp.min(jnp.where(le == m1, lane, LANES), -1, keepdims=True)
    le2 = jnp.where(lane == i1, ninf, le)
    m2 = jnp.max(le2, -1, keepdims=True)
    i2 = jnp.min(jnp.where(le2 == m2, lane, LANES), -1, keepdims=True)
    e2 = jnp.exp(m2 - m1)
    gate1 = g_p / (1.0 + e2)
    gate2 = g_p * e2 / (1.0 + e2)
    gate_ref[...] = jnp.where(lane == 0, gate1, jnp.where(lane == 1, gate2, 0.0))

    pick1, pick2 = lane == i1, lane == i2
    picks = jnp.where(pick1 | pick2, 1.0, 0.0)
    tm = logits.shape[0]
    tri = jnp.where(lax.broadcasted_iota(jnp.int32, (tm, tm), 1)
                    < lax.broadcasted_iota(jnp.int32, (tm, tm), 0), 1.0, 0.0).astype(BF16)
    before = jnp.dot(tri, picks.astype(BF16), preferred_element_type=F32)
    rank1 = jnp.sum(jnp.where(pick1, before, 0.0), -1, keepdims=True).astype(jnp.int32)
    rank2 = jnp.sum(jnp.where(pick2, before, 0.0), -1, keepdims=True).astype(jnp.int32)
    route_ref[...] = jnp.where(lane == 0, i1, jnp.where(lane == 1, i2, jnp.where(
        lane == 2, rank1, jnp.where(lane == 3, rank2, 0))))
    cnt_ref[...] = jnp.sum(picks, 0, keepdims=True).astype(jnp.int32)


def _moe_pos_kernel(route_ref, base_ref, pos_ref, *, tm):
    for u in range(base_ref.shape[0]):
        route = route_ref[u * tm:(u + 1) * tm, :]
        base = base_ref[u]
        lane = lax.broadcasted_iota(jnp.int32, route.shape, 1)
        pos = []
        for c in range(2):
            start = jnp.sum(jnp.where(lane == route[:, c:c + 1], base, 0), -1, keepdims=True)
            pos.append(start + route[:, 2 + c:3 + c])
        pos_ref[u * tm:(u + 1) * tm, :] = jnp.where(lane == 0, pos[0], jnp.where(lane == 1, pos[1], 0))


def _moe_pos(route, base, tm):
    n = route.shape[0]
    tiles = min(MOE_POS_TILES, n // tm)
    row = pl.BlockSpec((tiles * tm, LANES), lambda i: (i, 0))
    return pl.pallas_call(
        functools.partial(_moe_pos_kernel, tm=tm), grid=(n // (tiles * tm),),
        in_specs=[row, pl.BlockSpec((tiles, 1, LANES), lambda i: (i, 0, 0))], out_specs=row,
        out_shape=jax.ShapeDtypeStruct((n, LANES), jnp.int32),
        compiler_params=_cparams(("parallel",)),
    )(route, base)


def _mix_out(h, o_lat, y, g, c_out, w_o, ln_g, ln_b, w_router, b_router, tm):
    n, d = h.shape
    row = lambda c: pl.BlockSpec((tm, c), lambda i: (i, 0))
    full = lambda a: pl.BlockSpec(a.shape, lambda i: (0,) * a.ndim)
    vec = lambda a: a.reshape(1, -1)
    args = (h, o_lat, y, g, c_out, w_o, vec(ln_g), vec(ln_b), w_router, b_router)
    in_specs = [row(d), row(o_lat.shape[1]), row(RWKV_DIM), row(RWKV_DIM), row(CONV_DIM)] + [
        full(a) for a in args[5:]]
    out_shape = (jax.ShapeDtypeStruct((n, d), F32), jax.ShapeDtypeStruct((2, n, d // 4), jnp.int32),
                 jax.ShapeDtypeStruct((n, LANES), F32), jax.ShapeDtypeStruct((n, LANES), jnp.int32),
                 jax.ShapeDtypeStruct((n // tm, 1, LANES), jnp.int32))
    return pl.pallas_call(
        _mix_out_kernel, grid=(n // tm,), in_specs=in_specs,
        out_specs=(row(d), pl.BlockSpec((2, tm, d // 4), lambda i: (0, i, 0)), row(LANES), row(LANES),
                   pl.BlockSpec((None, 1, LANES), lambda i: (i, 0, 0))),
        out_shape=out_shape, compiler_params=_cparams(("parallel",)),
    )(*args)


def _moe_kernel(be_ref, nv_ref, x0_ref, x1_ref, wgu_ref, wd_ref, o_ref):
    valid = nv_ref[pl.program_id(0)]

    @pl.when(valid > 0)
    def _():
        keep = lax.broadcasted_iota(jnp.int32, x0_ref.shape, 0) < valid
        x = _unpack_row_halves(jnp.where(keep, x0_ref[...], 0),
                               jnp.where(keep, x1_ref[...], 0)).astype(BF16)
        gu = jnp.dot(x, wgu_ref[...], preferred_element_type=F32)
        hg = gu[:, :EXPERT_FF]
        hid = hg * _sigmoid(hg) * gu[:, EXPERT_FF:]
        _pack_row_halves(jnp.dot(hid.astype(BF16), wd_ref[...], preferred_element_type=F32), o_ref)

    @pl.when(valid == 0)
    def _():
        o_ref[...] = jnp.zeros_like(o_ref)


def _moe_experts(blk_e, blk_valid, xs, cap, w_gu, w_d):
    quarter = xs.shape[1]
    d = 4 * quarter
    nb = cap // MOE_ROWS
    grid_spec = pltpu.PrefetchScalarGridSpec(
        num_scalar_prefetch=2, grid=(nb,),
        in_specs=[pl.BlockSpec((MOE_ROWS, quarter), lambda i, be, nv: (i, 0)),
                  pl.BlockSpec((MOE_ROWS, quarter), lambda i, be, nv: (i + nb, 0)),
                  pl.BlockSpec((None, d, 2 * EXPERT_FF), lambda i, be, nv: (be[i], 0, 0)),
                  pl.BlockSpec((None, EXPERT_FF, d), lambda i, be, nv: (be[i], 0, 0))],
        out_specs=pl.BlockSpec((2, MOE_ROWS, quarter), lambda i, be, nv: (0, i, 0)))
    return pl.pallas_call(
        _moe_kernel, grid_spec=grid_spec,
        out_shape=jax.ShapeDtypeStruct((2, cap, quarter), jnp.int32),
        compiler_params=_cparams(("arbitrary",)),
    )(blk_e, blk_valid, xs, xs, w_gu, w_d)


def _sc_mesh():
    return plsc.VectorSubcoreMesh(core_axis_name="core", subcore_axis_name="subcore")


SC_WINDOW = 128


def _sc_rows(n):
    mesh = _sc_mesh()
    unit = SC_WINDOW * mesh.num_cores * mesh.num_subcores
    return -(-n // unit) * unit


def _scatter_rows2(x, idx1, idx2, n_out):
    n, w = x.shape
    mesh = _sc_mesh()
    win = SC_WINDOW
    assert n == _sc_rows(n)

    @pl.kernel(out_type=jax.ShapeDtypeStruct((n_out, w), x.dtype), mesh=mesh, scratch_types=[])
    def scatter(x_hbm, i1_hbm, i2_hbm, o_hbm):
        def body(x_vmem, i1_vmem, i2_vmem):
            pltpu.sync_copy(x_vmem, o_hbm.at[i1_vmem.at[0]])
            pltpu.sync_copy(x_vmem, o_hbm.at[i2_vmem.at[0]])

        idx_spec = pl.BlockSpec((1, win), lambda i: (0, i))
        pltpu.emit_pipeline(
            body, grid=(n // win,),
            in_specs=[pl.BlockSpec((win, w), lambda i: (i, 0)), idx_spec, idx_spec],
            out_specs=[], core_axis_name=('core', 'subcore'),
            dimension_semantics=(pltpu.PARALLEL,),
        )(x_hbm, i1_hbm, i2_hbm)

    return scatter(x, idx1.reshape(1, n), idx2.reshape(1, n))


def _gather_rows(x, idx):
    n = idx.shape[0]
    w = x.shape[1]
    mesh = _sc_mesh()
    win = SC_WINDOW
    assert n == _sc_rows(n)

    @pl.kernel(out_type=jax.ShapeDtypeStruct((n, w), x.dtype), mesh=mesh, scratch_types=[])
    def gather(x_hbm, i_hbm, o_hbm):
        def body(i_vmem, o_vmem):
            pltpu.sync_copy(x_hbm.at[i_vmem.at[0]], o_vmem)

        pltpu.emit_pipeline(
            body, grid=(n // win,),
            in_specs=[pl.BlockSpec((1, win), lambda i: (0, i))],
            out_specs=[pl.BlockSpec((win, w), lambda i: (i, 0))],
            core_axis_name=('core', 'subcore'), dimension_semantics=(pltpu.PARALLEL,),
        )(i_hbm, o_hbm)

    return gather(x, idx.reshape(1, n))


def _moe_layout(tile_counts, n_tok):
    cnt = tile_counts[:, 0, N_GROUPS:N_GROUPS + N_EXPERTS]
    tile_off = jnp.cumsum(cnt, axis=0) - cnt
    total = jnp.sum(cnt, axis=0)
    pcounts = (total + MOE_ROWS - 1) // MOE_ROWS * MOE_ROWS
    pends = jnp.cumsum(pcounts)
    base = (pends - pcounts)[None, :] + tile_off
    base = jnp.pad(base, ((0, 0), (N_GROUPS, LANES - N_GROUPS - N_EXPERTS)))[:, None, :]
    n_blocks = 2 * n_tok // MOE_ROWS + N_EXPERTS
    blk_start = jnp.arange(n_blocks, dtype=jnp.int32) * MOE_ROWS
    blk_e = jnp.minimum(jnp.sum((pends[None, :] <= blk_start[:, None]).astype(jnp.int32), axis=1),
                        N_EXPERTS - 1)
    blk_valid = jnp.clip((pends - pcounts + total)[blk_e] - blk_start, 0, MOE_ROWS)
    return (base.astype(jnp.int32), blk_e.astype(jnp.int32), blk_valid.astype(jnp.int32),
            n_blocks * MOE_ROWS)


def _rope_tables(pos):
    half = MLA_ROPE_DIM // 2
    freqs = ROPE_THETA ** (-jnp.arange(half, dtype=F32) / half)
    ang = pos.astype(F32)[:, None] * freqs
    cos, sin = jnp.cos(ang), jnp.sin(ang)
    pad = jnp.zeros((pos.shape[0], LANES - MLA_ROPE_DIM), F32)
    return (jnp.concatenate([cos, cos, pad], -1), jnp.concatenate([-sin, sin, pad], -1))


def _swap_halves(w):
    half = MLA_ROPE_DIM // 2
    return jnp.concatenate([w[..., half:], w[..., :half]], -1)


def _layer_params(l, w_in, g_qn, w_uq, g_kvn, w_uk, w_uv, mu_shift, w0, w_decay, a0, w_iclr,
                  w_gate_out, k_k, k_a, r_k, lnx_g, lnx_b, conv_w, w_out, ln1_g, ln1_b,
                  w_group_router, b_group_router, w_expert_router, b_expert_router,
                  w_exp_gate, w_exp_up, w_exp_down, ln2_g, ln2_b):
    d = w_in.shape[1]
    hi = lax.Precision.HIGHEST
    wi = w_in[l]
    zc = lambda c: jnp.zeros((d, c), F32)
    c0 = MLA_Q_RANK
    c1 = c0 + MLA_KV_RANK
    c2 = c1 + MLA_ROPE_DIM
    w_kr = wi[:, c1:c2]
    w_in_p = jnp.concatenate([
        wi[:, :c0], zc(QPAD - MLA_Q_RANK), wi[:, c0:c1],
        w_kr, _swap_halves(w_kr), zc(LANES - 2 * MLA_ROPE_DIM), wi[:, c2:]], -1).astype(BF16)
    gq_p = jnp.concatenate([g_qn[l], jnp.zeros((QPAD - MLA_Q_RANK,), F32)]).reshape(1, QPAD)
    uq = w_uq[l]
    q_lat = jnp.einsum('rhn,chn->rhc', uq[..., :MLA_NOPE_DIM], w_uk[l], precision=hi)
    q_rope = uq[..., MLA_NOPE_DIM:]
    wq = jnp.concatenate([q_lat, q_rope, _swap_halves(q_rope),
                          jnp.zeros((MLA_Q_RANK, MLA_HEADS, QHEAD - LANES - 2 * MLA_ROPE_DIM), F32)], -1)
    wq = (wq * (ATTN_SCALE * LOG2E)).reshape(MLA_Q_RANK, MLA_HEADS * QHEAD)
    wq = jnp.concatenate([wq, jnp.zeros((QPAD - MLA_Q_RANK, MLA_HEADS * QHEAD), F32)], 0).astype(BF16)
    wo = w_out[l]
    mla_dim = MLA_HEADS * MLA_V_DIM
    wo_att = jnp.einsum('chv,hvd->hcd', w_uv[l], wo[:mla_dim].reshape(MLA_HEADS, MLA_V_DIM, -1),
                        precision=hi).reshape(MLA_HEADS * MLA_KV_RANK, -1)
    w_o = jnp.concatenate([wo_att, wo[mla_dim:]], 0).astype(BF16)
    z64 = jnp.zeros((64, RWKV_DIM), F32)
    wdec_p = jnp.concatenate([w_decay[l], z64], 0).astype(BF16)
    wiclr_p = jnp.concatenate([z64, w_iclr[l]], 0).astype(BF16)
    w_router = jnp.concatenate(
        [w_group_router[l], w_expert_router[l],
         jnp.zeros((d, LANES - N_GROUPS - N_EXPERTS), F32)], -1).astype(BF16)
    b_router = jnp.concatenate(
        [b_group_router[l], b_expert_router[l],
         jnp.zeros((LANES - N_GROUPS - N_EXPERTS,), F32)]).reshape(1, LANES)
    w_gu = jnp.concatenate([w_exp_gate[l], w_exp_up[l]], -1).astype(BF16)
    return dict(
        w_in_p=w_in_p, gq_p=gq_p, wq=wq, g_kvn=g_kvn[l], mu=mu_shift[l], w0=w0[l], a0=a0[l],
        wdec_p=wdec_p, wiclr_p=wiclr_p, wgate=w_gate_out[l].astype(BF16), conv_w=conv_w[l],
        k_k=k_k[l], k_a=k_a[l], r_k=r_k[l].reshape(-1), lnx_g=lnx_g[l], lnx_b=lnx_b[l],
        w_o=w_o, ln1_g=ln1_g[l], ln1_b=ln1_b[l], w_router=w_router, b_router=b_router,
        w_gu=w_gu, w_d=w_exp_down[l].astype(BF16), ln2_g=ln2_g[l], ln2_b=ln2_b[l])


def _head_tile(p, n_seq):
    t = p.reshape(RWKV_HEADS, RWKV_HEAD_DIM).T
    return jnp.tile(t, (1, n_seq))


def _layer(source, src, lp, *, n_seq, t_len, time_major, ctab, stab, attend, shift0, wkv0, conv0):
    n, d = src[0].shape
    tm = min(512, n)
    h, q, kcat, ckv, kr, p_r, p_c = _in_proj(source, src, lp['w_in_p'], lp['gq_p'], lp['wq'],
                                             lp['g_kvn'], ctab, stab, tm)
    o_lat = attend(q, kcat)

    r, k, v, w, a, g, c_out, cu = _mix_prep(
        p_r, p_c, shift0, conv0, lp['mu'], lp['w0'], lp['a0'], lp['wdec_p'], lp['wiclr_p'],
        lp['wgate'], lp['conv_w'], n_seq=n_seq, tm=tm, shift=n_seq if time_major else 1)

    names = ('k_k', 'k_a', 'r_k', 'lnx_g', 'lnx_b')
    if not time_major and t_len % WKV_BLOCK == 0:
        par = jnp.stack([lp[nm] for nm in names] + [jnp.zeros_like(lp['k_k'])] * 3)
        head_eye = jnp.eye(RWKV_HEADS, dtype=F32)
        s0_bd = jnp.einsum('bhvk,hg->bhkgv', wkv0, head_eye).reshape(n_seq, RWKV_DIM, RWKV_DIM)
        y, s_fin = _wkv_chunked(r, k, v, w, a, par, s0_bd, n_seq)
        wkv_new = jnp.einsum(
            'bhkgv,hg->bhvk',
            s_fin.reshape(n_seq, RWKV_HEADS, RWKV_HEAD_DIM, RWKV_HEADS, RWKV_HEAD_DIM), head_eye)
    else:
        lanes = n_seq * RWKV_HEADS

        def to_scan(x):
            if time_major:
                x = x.reshape(t_len, n_seq, RWKV_HEADS, RWKV_HEAD_DIM).transpose(0, 3, 1, 2)
            else:
                x = x.reshape(n_seq, t_len, RWKV_HEADS, RWKV_HEAD_DIM).transpose(1, 3, 0, 2)
            return x.reshape(t_len, RWKV_HEAD_DIM, lanes)

        tiles = [_head_tile(lp[nm], n_seq) for nm in names]
        s0 = wkv0.transpose(3, 2, 0, 1).reshape(RWKV_HEAD_DIM, RWKV_HEAD_DIM, lanes)
        y, s_fin = _wkv(*[to_scan(x) for x in (r, k, v, w, a)], *tiles, s0, min(16, t_len))
        y = y.reshape(t_len, RWKV_HEAD_DIM, n_seq, RWKV_HEADS)
        y = (y.transpose(0, 2, 3, 1) if time_major else y.transpose(2, 0, 3, 1)).reshape(n, RWKV_DIM)
        wkv_new = s_fin.reshape(RWKV_HEAD_DIM, RWKV_HEAD_DIM, n_seq, RWKV_HEADS).transpose(2, 3, 1, 0)

    x, xp, gate, route, tile_counts = _mix_out(
        h, o_lat, y, g, c_out, lp['w_o'], lp['ln1_g'], lp['ln1_b'], lp['w_router'], lp['b_router'], tm)
    base, blk_e, blk_valid, cap = _moe_layout(tile_counts, n)
    pos = _moe_pos(route, base, tm)
    pos1, pos2 = pos[:, 0], pos[:, 1]
    n_src = _sc_rows(2 * n)
    spare = 2 * cap + jnp.arange(n_src - 2 * n, dtype=jnp.int32)
    xp = jnp.pad(xp.reshape(2 * n, d // 4), ((0, n_src - 2 * n), (0, 0)))
    xs = _scatter_rows2(xp, jnp.concatenate([pos1, pos1 + cap, spare]),
                        jnp.concatenate([pos2, pos2 + cap, spare]), 2 * cap + n_src - 2 * n)
    yb = _moe_experts(blk_e, blk_valid, xs, cap, lp['w_gu'], lp['w_d'])
    n_dst = _sc_rows(4 * n)
    y12 = _gather_rows(yb.reshape(2 * cap, d // 4), jnp.concatenate(
        [pos1, pos1 + cap, pos2, pos2 + cap, jnp.zeros((n_dst - 4 * n,), jnp.int32)]))
    return (x, y12, gate, lp['ln2_g'], lp['ln2_b']), ckv, kr, p_r, cu, wkv_new


def kernel(x_prompt, x_sample, cache_ckv, cache_krope, state_wkv, state_shift, state_conv, page_table, ln_in_g, ln_in_b, w_in, g_qn, w_uq, g_kvn, w_uk, w_uv, mu_shift, w0, w_decay, a0, w_iclr, w_gate_out, k_k, k_a, r_k, lnx_g, lnx_b, conv_w, w_out, ln1_g, ln1_b, w_group_router, b_group_router, w_expert_router, b_expert_router, w_exp_gate, w_exp_up, w_exp_down, ln2_g, ln2_b):
    bp, sp, d = x_prompt.shape
    bd, td, _ = x_sample.shape
    past_len = page_table.shape[1] * PAGE_SIZE
    depth = w_in.shape[0]
    np_, ns = bp * sp, bd * td

    ctab_p, stab_p = _rope_tables(jnp.arange(sp))
    ctab_s, stab_s = _rope_tables(jnp.repeat(past_len + jnp.arange(td), bd))

    src_p = ('ln_in', (x_prompt.reshape(np_, d), ln_in_g, ln_in_b))
    src_s = ('ln_in', (jnp.swapaxes(x_sample, 0, 1).reshape(ns, d), ln_in_g, ln_in_b))

    krope_t = jnp.swapaxes(cache_krope, 2, 3)
    zero_shift = jnp.zeros((bp, RWKV_PROJ), F32)
    zero_wkv = jnp.zeros((bp, RWKV_HEADS, RWKV_HEAD_DIM, RWKV_HEAD_DIM), F32)
    zero_conv = jnp.zeros((bp, 2, CONV_DIM), F32)

    outs_p = [[] for _ in range(5)]
    outs_s = [[] for _ in range(5)]
    for l in range(depth):
        lp = _layer_params(l, w_in, g_qn, w_uq, g_kvn, w_uk, w_uv, mu_shift, w0, w_decay, a0,
                           w_iclr, w_gate_out, k_k, k_a, r_k, lnx_g, lnx_b, conv_w, w_out, ln1_g,
                           ln1_b, w_group_router, b_group_router, w_expert_router, b_expert_router,
                           w_exp_gate, w_exp_up, w_exp_down, ln2_g, ln2_b)

        def attend_s(q, kcat, l=l):
            qs = q.reshape(MLA_HEADS, td, bd, QHEAD).transpose(2, 0, 1, 3).reshape(
                bd, MLA_HEADS * td, QHEAD)
            kn = kcat.reshape(td, bd, KCAT).transpose(1, 0, 2)
            o = _mla_sample(qs, kn, cache_ckv, krope_t, page_table, l)
            return o.reshape(bd, MLA_HEADS, td, MLA_KV_RANK).transpose(2, 0, 1, 3).reshape(
                ns, MLA_HEADS * MLA_KV_RANK)

        pend_s, ckv, kr, p_r, cu, wkv = _layer(
            *src_s, lp, n_seq=bd, t_len=td, time_major=True, ctab=ctab_s, stab=stab_s,
            attend=attend_s, shift0=state_shift[l], wkv0=state_wkv[l], conv0=state_conv[l])
        tmaj = lambda x: jnp.swapaxes(x.reshape(td, bd, -1), 0, 1)
        outs_s[0].append(tmaj(ckv))
        outs_s[1].append(tmaj(kr))
        outs_s[2].append(wkv)
        outs_s[3].append(p_r.reshape(td, bd, RWKV_PROJ)[-1])
        outs_s[4].append(tmaj(cu)[:, -2:])
        src_s = ('ln2', pend_s)

        pend_p, ckv, kr, p_r, cu, wkv = _layer(
            *src_p, lp, n_seq=bp, t_len=sp, time_major=False, ctab=ctab_p, stab=stab_p,
            attend=lambda q, kcat: _mla_prompt(q, kcat, bp, sp),
            shift0=zero_shift, wkv0=zero_wkv, conv0=zero_conv)
        outs_p[0].append(ckv.reshape(-1, PAGE_SIZE, MLA_KV_RANK))
        outs_p[1].append(kr.reshape(-1, PAGE_SIZE, MLA_ROPE_DIM))
        outs_p[2].append(wkv)
        outs_p[3].append(p_r.reshape(bp, sp, RWKV_PROJ)[:, -1])
        outs_p[4].append(cu.reshape(bp, sp, CONV_DIM)[:, -2:])
        src_p = ('ln2', pend_p)

    hp = _ln2(*pend_p, min(512, np_))
    hs = _ln2(*pend_s, min(512, ns))
    y_p = hp.reshape(bp, sp, d)
    y_s = jnp.swapaxes(hs.reshape(td, bd, d), 0, 1)
    return (y_p, y_s) + tuple(jnp.stack(o) for o in outs_p) + tuple(jnp.stack(o) for o in outs_s)
```
